```python
import jax, jax.numpy as jnp
from jax import lax
import numpy as np

D_MODEL = 1024
BATCH = 8
SEQ = 2048
DEPTH = 2
DEC_BATCH = 128
DEC_SEQ = 4
PAST_LEN = 16384
PAGE_SIZE = 128

PLE_DIM = 256
NORM_EPS = 1e-6

HG_HEADS = 4
HG_KDIM = 128
HG_VDIM = 128
HG_KW = HG_HEADS * HG_KDIM
HG_VW = HG_HEADS * HG_VDIM
HG_CHUNK = 16

S5_WIDTH = 512
S5_GROUP = 16
S5_GROUPS = S5_WIDTH // S5_GROUP
S5_STATE = 64
S5_DT_MIN = 1e-3
S5_DT_MAX = 1e-1

M2_INNER = 1024
M2_HEADDIM = 64
M2_HEADS = M2_INNER // M2_HEADDIM
M2_NGROUPS = 4
M2_HPG = M2_HEADS // M2_NGROUPS
M2_STATE = 128
M2_CONV = 4
M2_CHUNK = 64
M2_CONV_DIM = M2_INNER + 2 * M2_NGROUPS * M2_STATE

N_BRANCH = 3
IN_SIZES = (HG_KW, HG_KW, HG_VW, HG_VW, S5_WIDTH, M2_INNER, M2_CONV_DIM, M2_HEADS, N_BRANCH * D_MODEL)
IN_COLS = 2 * HG_KW + 2 * HG_VW + S5_WIDTH + M2_INNER + M2_CONV_DIM + M2_HEADS + N_BRANCH * D_MODEL

E_GROUPS = 4
E_PER_GROUP = 4
N_EXPERTS = E_GROUPS * E_PER_GROUP
E_TOPK = 2
E_FF = 256

kernel_name = 'hybrid_hgrn2_s5_ssd_hmoe_step'


def rmsnorm(x, w):
    xf = x.astype(jnp.float32)
    xf = xf * lax.rsqrt(jnp.mean(xf * xf, axis=-1, keepdims=True) + NORM_EPS)
    return xf * w.astype(jnp.float32)


def split_cols(a, sizes):
    offsets = [int(o) for o in np.cumsum(sizes)[:-1]]
    return jnp.split(a, offsets, axis=-1)


def chunk_pad(a, chunk):
    bsz, t = a.shape[0], a.shape[1]
    nc = -(-t // chunk)
    a = jnp.pad(a, ((0, 0), (0, nc * chunk - t)) + ((0, 0),) * (a.ndim - 2))
    return jnp.moveaxis(a.reshape((bsz, nc, chunk) + a.shape[2:]), 1, 0)


def hgrn2_recurrence(q, k, v, logf, s0):
    bsz, t = q.shape[0], q.shape[1]
    L = HG_CHUNK
    causal = jnp.tril(jnp.ones((L, L), dtype=bool))[None, :, :, None, None]

    def step(s, blk):
        qb, kb, vb, fb = blk
        b = jnp.cumsum(fb, axis=1)
        b_end = b[:, -1]
        dec = jnp.exp(jnp.where(causal, b[:, :, None] - b[:, None, :], -jnp.inf))
        sc = jnp.einsum('blhk,bshk,blshk->bhls', qb, kb, dec)
        o = (jnp.einsum('bhls,bshv->blhv', sc, vb)
             + jnp.einsum('blhk,bhkv->blhv', qb * jnp.exp(b), s))
        s_new = (jnp.exp(b_end)[..., None] * s
                 + jnp.einsum('bshk,bshv->bhkv', kb * jnp.exp(b_end[:, None] - b), vb))
        return s_new, o

    s_fin, o = lax.scan(step, s0, (chunk_pad(q, L), chunk_pad(k, L), chunk_pad(v, L), chunk_pad(logf, L)))
    o = jnp.moveaxis(o, 0, 1)
    o = o.reshape((bsz, -1) + o.shape[3:])[:, :t]
    return o, s_fin


def hgrn2_branch(zq, zf, zi, zg, lb, gnorm_w, s0):
    bsz, t = zq.shape[:2]
    shp_k = (bsz, t, HG_HEADS, HG_KDIM)
    shp_v = (bsz, t, HG_HEADS, HG_VDIM)
    q = jax.nn.silu(zq).reshape(shp_k)
    zf = zf.reshape(shp_k)
    lb = lb.astype(jnp.float32).reshape(HG_HEADS, HG_KDIM)
    f = lb + (1.0 - lb) * jax.nn.sigmoid(zf)
    logf = jnp.log(f)
    k = (1.0 - lb) * jax.nn.sigmoid(-zf)
    v = zi.reshape(shp_v)
    o, s = hgrn2_recurrence(q, k, v, logf, s0.astype(jnp.float32))
    o = rmsnorm(o, gnorm_w) * jax.nn.silu(zg.reshape(shp_v))
    return o.reshape(bsz, t, HG_VW), s


def s5_branch(u, a_re, a_im, b_re, b_im, c_re, c_im, d, log_dt, w_glu, b_glu, h0_re, h0_im):
    bsz, t = u.shape[:2]
    ug = u.astype(jnp.float32).reshape(bsz, t, S5_GROUPS, S5_GROUP)
    lam = lax.complex(a_re.astype(jnp.float32), a_im.astype(jnp.float32))
    dt = jnp.exp(log_dt.astype(jnp.float32))[:, None]
    a_bar = jnp.exp(lam * dt)
    b_bar = ((a_bar - 1.0) / lam)[..., None] * lax.complex(b_re.astype(jnp.float32), b_im.astype(jnp.float32))
    bu = jnp.einsum('gnc,btgc->btgn', b_bar, ug.astype(jnp.complex64))
    h0 = lax.complex(h0_re.astype(jnp.float32), h0_im.astype(jnp.float32))
    bu = bu.at[:, 0].add(a_bar * h0)
    a_seq = jnp.broadcast_to(a_bar, bu.shape)

    def combine(e1, e2):
        a1, b1 = e1
        a2, b2 = e2
        return a2 * a1, a2 * b1 + b2

    _, h = lax.associative_scan(combine, (a_seq, bu), axis=1)
    c = lax.complex(c_re.astype(jnp.float32), c_im.astype(jnp.float32))
    y = jnp.einsum('gcn,btgn->btgc', c, h).real + d.astype(jnp.float32) * ug
    g = jax.nn.gelu(y.reshape(bsz, t, S5_WIDTH))
    gv, gg = jnp.split(g @ w_glu + b_glu, 2, axis=-1)
    h_last = h[:, -1]
    return gv * jax.nn.sigmoid(gg), h_last.real, h_last.imag


def ssd_scan(x, dt, a, bm, cm, s0):
    bsz, t = x.shape[:2]
    L = M2_CHUNK
    xc = chunk_pad(x, L).reshape(-1, bsz, L, M2_NGROUPS, M2_HPG, M2_HEADDIM)
    dtc = chunk_pad(dt, L).reshape(-1, bsz, L, M2_NGROUPS, M2_HPG)
    bc = chunk_pad(bm, L)
    cc = chunk_pad(cm, L)
    acs = jnp.cumsum(dtc * a.reshape(M2_NGROUPS, M2_HPG), axis=2)
    xdt = xc * dtc[..., None]
    acs_h = jnp.moveaxis(acs, 2, -1)
    causal = jnp.tril(jnp.ones((L, L), dtype=bool))
    lmat = jnp.exp(jnp.where(causal, acs_h[..., :, None] - acs_h[..., None, :], -jnp.inf))
    cb = jnp.einsum('cblgn,cbsgn->cbgls', cc, bc)
    y_diag = jnp.einsum('cbgrls,cbsgrp->cblgrp', cb[:, :, :, None] * lmat, xdt)
    decay_end = jnp.exp(acs[:, :, -1:] - acs)
    chunk_states = jnp.einsum('cblgn,cblgr,cblgrp->cbgrpn', bc, decay_end, xdt)
    chunk_decay = jnp.exp(acs[:, :, -1])

    def step(s, blk):
        dec, st = blk
        return dec[..., None, None] * s + st, s

    s_init = s0.reshape(bsz, M2_NGROUPS, M2_HPG, M2_HEADDIM, M2_STATE)
    s_fin, s_in = lax.scan(step, s_init, (chunk_decay, chunk_states))
    y_off = jnp.einsum('cblgn,cbgrpn,cblgr->cblgrp', cc, s_in, jnp.exp(acs))
    y = jnp.moveaxis(y_diag + y_off, 0, 1).reshape(bsz, -1, M2_HEADS, M2_HEADDIM)[:, :t]
    return y, s_fin.reshape(bsz, M2_HEADS, M2_HEADDIM, M2_STATE)


def mamba2_branch(z, xbc, dt_raw, conv_w, conv_b, dt_bias, a_log, d_skip, norm_w, s0, conv0):
    bsz, t = z.shape[:2]
    full = jnp.concatenate([conv0.astype(jnp.float32), xbc], axis=1)
    conv = lax.conv_general_dilated(full, conv_w.astype(jnp.float32)[:, None, :], (1,), 'VALID',
                                    dimension_numbers=('NWC', 'WIO', 'NWC'),
                                    feature_group_count=M2_CONV_DIM)
    xbc_act = jax.nn.silu(conv + conv_b.astype(jnp.float32))
    xs, bm, cm = split_cols(xbc_act, (M2_INNER, M2_NGROUPS * M2_STATE, M2_NGROUPS * M2_STATE))
    x = xs.reshape(bsz, t, M2_HEADS, M2_HEADDIM)
    dt = jax.nn.softplus(dt_raw + dt_bias.astype(jnp.float32))
    a = -jnp.exp(a_log.astype(jnp.float32))
    y, s = ssd_scan(x, dt, a, bm.reshape(bsz, t, M2_NGROUPS, M2_STATE),
                    cm.reshape(bsz, t, M2_NGROUPS, M2_STATE), s0.astype(jnp.float32))
    y = y + d_skip.astype(jnp.float32)[:, None] * x
    yg = (y.reshape(bsz, t, M2_INNER) * jax.nn.silu(z)).reshape(bsz, t, M2_NGROUPS, M2_INNER // M2_NGROUPS)
    y = rmsnorm(yg, norm_w.reshape(M2_NGROUPS, M2_INNER // M2_NGROUPS)).reshape(bsz, t, M2_INNER)
    return y, s, full[:, full.shape[1] - (M2_CONV - 1):]


def hier_moe(x, w_rg, b_rg, w_re, b_re, w_gate, w_up, w_down):
    bsz, t, _ = x.shape
    xf = x.reshape(-1, D_MODEL)
    g_logits = (xf @ w_rg + b_rg).astype(jnp.float32)
    g_prob = jax.nn.softmax(g_logits, axis=-1)
    g_val, g_sel = lax.top_k(g_logits, 1)
    p_group = jnp.take_along_axis(g_prob, g_sel, axis=-1)
    e_logits = (xf @ w_re + b_re).astype(jnp.float32).reshape(-1, E_GROUPS, E_PER_GROUP)
    e_in = jnp.take_along_axis(e_logits, g_sel[:, :, None], axis=1)[:, 0]
    top_v, top_i = lax.top_k(e_in, E_TOPK)
    w = jax.nn.softmax(top_v, axis=-1) * p_group
    eid = g_sel * E_PER_GROUP + top_i
    combine = jnp.sum(jax.nn.one_hot(eid, N_EXPERTS, dtype=jnp.float32) * w[..., None], axis=1)
    out = jnp.zeros(xf.shape, jnp.float32)
    for e in range(N_EXPERTS):
        hid = jax.nn.silu(xf @ w_gate[e]) * (xf @ w_up[e])
        out = out + combine[:, e:e + 1] * (hid @ w_down[e])
    return out.reshape(bsz, t, D_MODEL)


def layer(h, p_i, lp, lb, st):
    hg0, s5re0, s5im0, ssm0, conv0 = st
    xn = rmsnorm(h, lp['norm_mix'])
    zq, zf, zi, zg, zu, zz, zxbc, zdt, zgate = split_cols(xn @ lp['w_in'], IN_SIZES)
    o_a, s_hg = hgrn2_branch(zq, zf, zi, zg, lb, lp['hg_gnorm'], hg0)
    o_b, s5_re, s5_im = s5_branch(zu, lp['s5_a_re'], lp['s5_a_im'], lp['s5_b_re'], lp['s5_b_im'],
                                  lp['s5_c_re'], lp['s5_c_im'], lp['s5_d'], lp['s5_log_dt'],
                                  lp['s5_w_glu'], lp['s5_b_glu'], s5re0, s5im0)
    o_c, s_ssm, conv_new = mamba2_branch(zz, zxbc, zdt, lp['m2_conv_w'], lp['m2_conv_b'], lp['m2_dt_bias'],
                                         lp['m2_a_log'], lp['m2_d'], lp['m2_norm'], ssm0, conv0)
    ga, gb, gc = jnp.split(jax.nn.sigmoid(zgate), N_BRANCH, axis=-1)
    merged = ga * (o_a @ lp['w_br_hg']) + gb * (o_b @ lp['w_br_s5']) + gc * (o_c @ lp['w_br_m2'])
    h = h + merged @ lp['w_out']
    h = h + hier_moe(rmsnorm(h, lp['norm_ffn']), lp['w_rg'], lp['b_rg'], lp['w_re'], lp['b_re'],
                     lp['w_e_gate'], lp['w_e_up'], lp['w_e_down'])
    h = h + jax.nn.sigmoid(rmsnorm(h, lp['norm_ple']) @ lp['w_ple_gate']) * (p_i @ lp['w_ple_proj'])
    return h, (s_hg, s5_re, s5_im, s_ssm, conv_new)


def trunk(x, p, states, layers, lbs, norm_final):
    h = x.astype(jnp.float32)
    new = []
    for i in range(DEPTH):
        h, ns = layer(h, p[i], layers[i], lbs[i], tuple(s[i] for s in states))
        new.append(ns)
    stacked = tuple(jnp.stack([n[j] for n in new]) for j in range(5))
    return rmsnorm(h, norm_final), stacked


def setup_inputs(seed: int = 0) -> dict:
    key = jax.random.key(seed)
    ks = iter(jax.random.split(key, 64))

    def nrm(shape, scale=1.0):
        return scale * jax.random.normal(next(ks), shape, jnp.float32)

    def gain(shape):
        return 1.0 + 0.05 * jax.random.normal(next(ks), shape, jnp.float32)

    def unif(shape, lo, hi):
        return jax.random.uniform(next(ks), shape, jnp.float32, lo, hi)

    G, N = S5_GROUPS, S5_STATE
    a_im0 = jnp.pi * jnp.arange(N, dtype=jnp.float32)
    dt_m2 = jnp.exp(unif((DEPTH, M2_HEADS), float(np.log(1e-3)), float(np.log(1e-1))))
    return {
        'x_prompt': nrm((BATCH, SEQ, D_MODEL)),
        'x_sample': nrm((DEC_BATCH, DEC_SEQ, D_MODEL)),
        'state_hgrn': nrm((DEPTH, DEC_BATCH, HG_HEADS, HG_KDIM, HG_VDIM), 0.5),
        'state_s5_re': nrm((DEPTH, DEC_BATCH, G, N), 0.1),
        'state_s5_im': nrm((DEPTH, DEC_BATCH, G, N), 0.1),
        'state_ssm': nrm((DEPTH, DEC_BATCH, M2_HEADS, M2_HEADDIM, M2_STATE), 0.1),
        'state_conv': nrm((DEPTH, DEC_BATCH, M2_CONV - 1, M2_CONV_DIM)),
        'p_prompt': nrm((DEPTH, BATCH, SEQ, PLE_DIM)),
        'p_sample': nrm((DEPTH, DEC_BATCH, DEC_SEQ, PLE_DIM)),
        'norm_mix': gain((DEPTH, D_MODEL)),
        'w_in': nrm((DEPTH, D_MODEL, IN_COLS), D_MODEL ** -0.5),
        'hg_lb_logits': nrm((DEPTH, HG_KW), 0.5),
        'hg_gnorm': gain((DEPTH, HG_VDIM)),
        'w_br_hg': nrm((DEPTH, HG_VW, D_MODEL), HG_VW ** -0.5),
        's5_a_re': -0.5 + nrm((DEPTH, G, N), 0.01),
        's5_a_im': a_im0 + nrm((DEPTH, G, N), 0.01),
        's5_b_re': nrm((DEPTH, G, N, S5_GROUP), (2 * S5_GROUP) ** -0.5),
        's5_b_im': nrm((DEPTH, G, N, S5_GROUP), (2 * S5_GROUP) ** -0.5),
        's5_c_re': nrm((DEPTH, G, S5_GROUP, N), N ** -0.5),
        's5_c_im': nrm((DEPTH, G, S5_GROUP, N), N ** -0.5),
        's5_d': nrm((DEPTH, G, S5_GROUP)),
        's5_log_dt': unif((DEPTH, G), float(np.log(S5_DT_MIN)), float(np.log(S5_DT_MAX))),
        's5_w_glu': nrm((DEPTH, S5_WIDTH, 2 * S5_WIDTH), S5_WIDTH ** -0.5),
        's5_b_glu': nrm((DEPTH, 2 * S5_WIDTH), 0.01),
        'w_br_s5': nrm((DEPTH, S5_WIDTH, D_MODEL), S5_WIDTH ** -0.5),
        'm2_conv_w': nrm((DEPTH, M2_CONV, M2_CONV_DIM), M2_CONV ** -0.5),
        'm2_conv_b': nrm((DEPTH, M2_CONV_DIM), 0.01),
        'm2_dt_bias': dt_m2 + jnp.log(-jnp.expm1(-dt_m2)),
        'm2_a_log': jnp.log(unif((DEPTH, M2_HEADS), 1.0, 16.0)),
        'm2_d': 1.0 + nrm((DEPTH, M2_HEADS), 0.1),
        'm2_norm': gain((DEPTH, M2_INNER)),
        'w_br_m2': nrm((DEPTH, M2_INNER, D_MODEL), M2_INNER ** -0.5),
        'w_out': nrm((DEPTH, D_MODEL, D_MODEL), D_MODEL ** -0.5),
        'norm_ffn': gain((DEPTH, D_MODEL)),
        'w_rg': nrm((DEPTH, D_MODEL, E_GROUPS), D_MODEL ** -0.5),
        'b_rg': nrm((DEPTH, E_GROUPS), 0.01),
        'w_re': nrm((DEPTH, D_MODEL, N_EXPERTS), D_MODEL ** -0.5),
        'b_re': nrm((DEPTH, N_EXPERTS), 0.01),
        'w_e_gate': nrm((DEPTH, N_EXPERTS, D_MODEL, E_FF), D_MODEL ** -0.5),
        'w_e_up': nrm((DEPTH, N_EXPERTS, D_MODEL, E_FF), D_MODEL ** -0.5),
        'w_e_down': nrm((DEPTH, N_EXPERTS, E_FF, D_MODEL), E_FF ** -0.5),
        'norm_ple': gain((DEPTH, D_MODEL)),
        'w_ple_gate': nrm((DEPTH, D_MODEL, D_MODEL), D_MODEL ** -0.5),
        'w_ple_proj': nrm((DEPTH, PLE_DIM, D_MODEL), PLE_DIM ** -0.5),
        'norm_final': gain((D_MODEL,)),
    }


def reference(x_prompt, x_sample, state_hgrn, state_s5_re, state_s5_im, state_ssm, state_conv,
              p_prompt, p_sample,
              norm_mix, w_in, hg_lb_logits, hg_gnorm, w_br_hg,
              s5_a_re, s5_a_im, s5_b_re, s5_b_im, s5_c_re, s5_c_im, s5_d, s5_log_dt, s5_w_glu, s5_b_glu, w_br_s5,
              m2_conv_w, m2_conv_b, m2_dt_bias, m2_a_log, m2_d, m2_norm, w_br_m2,
              w_out,
              norm_ffn, w_rg, b_rg, w_re, b_re, w_e_gate, w_e_up, w_e_down,
              norm_ple, w_ple_gate, w_ple_proj,
              norm_final):
    lbp = jax.nn.softmax(hg_lb_logits.astype(jnp.float32), axis=0)
    lbs = jnp.cumsum(lbp, axis=0) - lbp[0]
    layers = [dict(norm_mix=norm_mix[i], w_in=w_in[i], hg_gnorm=hg_gnorm[i], w_br_hg=w_br_hg[i],
                   s5_a_re=s5_a_re[i], s5_a_im=s5_a_im[i], s5_b_re=s5_b_re[i], s5_b_im=s5_b_im[i],
                   s5_c_re=s5_c_re[i], s5_c_im=s5_c_im[i], s5_d=s5_d[i], s5_log_dt=s5_log_dt[i],
                   s5_w_glu=s5_w_glu[i], s5_b_glu=s5_b_glu[i], w_br_s5=w_br_s5[i],
                   m2_conv_w=m2_conv_w[i], m2_conv_b=m2_conv_b[i], m2_dt_bias=m2_dt_bias[i],
                   m2_a_log=m2_a_log[i], m2_d=m2_d[i], m2_norm=m2_norm[i], w_br_m2=w_br_m2[i],
                   w_out=w_out[i], norm_ffn=norm_ffn[i], w_rg=w_rg[i], b_rg=b_rg[i], w_re=w_re[i], b_re=b_re[i],
                   w_e_gate=w_e_gate[i], w_e_up=w_e_up[i], w_e_down=w_e_down[i],
                   norm_ple=norm_ple[i], w_ple_gate=w_ple_gate[i], w_ple_proj=w_ple_proj[i])
              for i in range(DEPTH)]
    bp = x_prompt.shape[0]
    zero_states = (jnp.zeros((DEPTH, bp, HG_HEADS, HG_KDIM, HG_VDIM), jnp.float32),
                   jnp.zeros((DEPTH, bp, S5_GROUPS, S5_STATE), jnp.float32),
                   jnp.zeros((DEPTH, bp, S5_GROUPS, S5_STATE), jnp.float32),
                   jnp.zeros((DEPTH, bp, M2_HEADS, M2_HEADDIM, M2_STATE), jnp.float32),
                   jnp.zeros((DEPTH, bp, M2_CONV - 1, M2_CONV_DIM), jnp.float32))
    y_prompt, (hg_p, s5re_p, s5im_p, ssm_p, conv_p) = trunk(x_prompt, p_prompt, zero_states, layers, lbs, norm_final)
    past = (state_hgrn, state_s5_re, state_s5_im, state_ssm, state_conv)
    y_sample, (hg_s, s5re_s, s5im_s, ssm_s, conv_s) = trunk(x_sample, p_sample, past, layers, lbs, norm_final)
    return (y_prompt, y_sample, hg_p, s5re_p, s5im_p, ssm_p, conv_p, hg_s, s5re_s, s5im_s, ssm_s, conv_s)
```

```python
import functools

import jax
import jax.numpy as jnp
from jax import lax
from jax.experimental import pallas as pl
from jax.experimental.pallas import tpu as pltpu

F32 = jnp.float32
BF16 = jnp.bfloat16

D_MODEL = 1024
DEPTH = 2
PLE_DIM = 256
NORM_EPS = 1e-6

HG_HEADS = 4
HG_DIM = 128
HG_W = HG_HEADS * HG_DIM

S5_WIDTH = 512
S5_GROUP = 16
S5_GROUPS = S5_WIDTH // S5_GROUP
S5_STATE = 64
S5_HW = S5_GROUPS * S5_STATE

M2_INNER = 1024
M2_HEADDIM = 64
M2_HEADS = M2_INNER // M2_HEADDIM
M2_NGROUPS = 4
M2_STATE = 128
M2_CONV = 4
M2_BC = M2_NGROUPS * M2_STATE
M2_CONV_DIM = M2_INNER + 2 * M2_BC
M2_PAIRS = M2_HEADS // 2
M2_NORM_W = M2_INNER // M2_NGROUPS

N_BRANCH = 3
E_GROUPS = 4
E_PER_GROUP = 4
N_EXPERTS = E_GROUPS * E_PER_GROUP
E_FF = 256

LANE = 128
SUBLANE = 8

COL_XBC = 0
COL_Z = COL_XBC + M2_CONV_DIM
COL_GATE = COL_Z + M2_INNER
COL_Q = COL_GATE + N_BRANCH * D_MODEL
COL_F = COL_Q + HG_W
COL_I = COL_F + HG_W
COL_G = COL_I + HG_W
COL_U = COL_G + HG_W
COL_DT = COL_U + S5_WIDTH
Z_TN = 1280
Z_COLS = 7 * Z_TN
assert COL_DT + LANE <= Z_COLS and COL_DT % LANE == 0

_O_Q, _O_F, _O_I, _O_G, _O_U = 0, 512, 1024, 1536, 2048
_O_Z = 2560
_O_XBC = _O_Z + M2_INNER
_O_DT = _O_XBC + M2_CONV_DIM
_O_GATE = _O_DT + M2_HEADS
IN_COLS = _O_GATE + N_BRANCH * D_MODEL

VMEM_LIMIT = 56 * 1024 * 1024


def _cparams(sem):
    return pltpu.CompilerParams(dimension_semantics=sem, vmem_limit_bytes=VMEM_LIMIT)


def _rms_scale(x):
    return lax.rsqrt(jnp.mean(x * x, axis=-1, keepdims=True) + NORM_EPS)


def _silu(x):
    return x * jax.nn.sigmoid(x)


def _dot(a, b):
    return jnp.dot(a, b, preferred_element_type=F32)


def _dot_nt(a, b):
    return lax.dot_general(a, b, (((1,), (1,)), ((), ())), preferred_element_type=F32)


def _dot_tn(a, b):
    return lax.dot_general(a, b, (((0,), (0,)), ((), ())), preferred_element_type=F32)


def _cumsum_rows(x):
    n = x.shape[0]
    row = lax.broadcasted_iota(jnp.int32, x.shape, 0)
    s = 1
    while s < n:
        x = x + jnp.where(row >= s, pltpu.roll(x, s, 0), 0.0)
        s *= 2
    return x


def _row_tile(n, pref):
    t = min(pref, n)
    assert n % t == 0
    return t


def _norm_matmul_kernel(x_ref, nw_ref, w_ref, o_ref, xn_ref):
    @pl.when(pl.program_id(1) == 0)
    def _():
        x = x_ref[...]
        xn_ref[...] = (x * _rms_scale(x) * nw_ref[...]).astype(BF16)

    o_ref[...] = _dot(xn_ref[...], w_ref[...])


def _norm_matmul(h, nw, w):
    n = h.shape[0]
    cols = w.shape[1]
    tm = _row_tile(n, 1024)
    tn = Z_TN
    return pl.pallas_call(
        _norm_matmul_kernel,
        grid=(n // tm, cols // tn),
        in_specs=[
            pl.BlockSpec((tm, D_MODEL), lambda i, j: (i, 0)),
            pl.BlockSpec((1, D_MODEL), lambda i, j: (0, 0)),
            pl.BlockSpec((D_MODEL, tn), lambda i, j: (0, j)),
        ],
        out_specs=pl.BlockSpec((tm, tn), lambda i, j: (i, j)),
        out_shape=jax.ShapeDtypeStruct((n, cols), F32),
        scratch_shapes=[pltpu.VMEM((tm, D_MODEL), BF16)],
        compiler_params=_cparams(("parallel", "arbitrary")),
    )(h, nw, w)


def _hgrn_kernel(zq_ref, zf_ref, zi_ref, zg_ref, lbl_ref, gw_ref, s0_ref, o_ref, s_ref, st_ref,
                 *, layer, chunk, t_valid, t_total):
    c = pl.program_id(2)
    C = chunk

    @pl.when(c == 0)
    def _():
        st_ref[...] = s0_ref[...].T

    lg = lbl_ref[...]
    e = jnp.exp(lg - jnp.max(lg, axis=0, keepdims=True))
    prob = e / jnp.sum(e, axis=0, keepdims=True)
    lb = jnp.zeros((1, HG_DIM), F32)
    for j in range(1, layer + 1):
        lb = lb + prob[j:j + 1]

    zf = zf_ref[...]
    q = _silu(zq_ref[...])
    v = zi_ref[...]
    logf = jnp.log(lb + (1.0 - lb) * jax.nn.sigmoid(zf))
    k = (1.0 - lb) * jax.nn.sigmoid(-zf)
    if t_valid < t_total:
        valid = (c * C + lax.broadcasted_iota(jnp.int32, (C, HG_DIM), 0)) < t_valid
        logf = jnp.where(valid, logf, 0.0)
        k = jnp.where(valid, k, 0.0)

    b = _cumsum_rows(logf)
    b_end = b[C - 1:C, :]
    st = st_ref[...]
    st16 = st.astype(BF16)
    o = _dot_nt((q * jnp.exp(b)).astype(BF16), st16)

    rr = lax.broadcasted_iota(jnp.int32, (C, C), 0)
    cc = lax.broadcasted_iota(jnp.int32, (C, C), 1)
    xr = rr ^ cc
    lev = jnp.where(rr > cc, xr, -1)
    dcode = jnp.where(xr < SUBLANE, rr - cc, -1)

    sc = jnp.where(dcode == 0, jnp.sum(q * k, axis=1, keepdims=True), 0.0)
    for j in range(1, min(SUBLANE, C)):
        kr = pltpu.roll(k, j, 0)
        br = pltpu.roll(b, j, 0)
        tj = jnp.sum(q * kr * jnp.exp(jnp.minimum(b - br, 0.0)), axis=1, keepdims=True)
        sc = jnp.where(dcode == j, tj, sc)

    m = C // 2
    while m >= SUBLANE:
        parts = []
        for p0 in range(0, C, 2 * m):
            parts.append(jnp.broadcast_to(b[p0 + m - 1:p0 + m, :], (2 * m, HG_DIM)))
        r = parts[0] if len(parts) == 1 else jnp.concatenate(parts, axis=0)
        qe = q * jnp.exp(jnp.minimum(b - r, 0.0))
        ke = k * jnp.exp(jnp.minimum(r - b, 0.0))
        pm = _dot_nt(qe.astype(BF16), ke.astype(BF16))
        sc = jnp.where((lev >> (m.bit_length() - 1)) == 1, pm, sc)
        m //= 2

    o = o + _dot(sc.astype(BF16), v.astype(BF16))
    o = o * _rms_scale(o) * gw_ref[...] * _silu(zg_ref[...])
    o_ref[...] = o

    ke = k * jnp.exp(b_end - b)
    st_new = st * jnp.exp(b_end) + _dot_tn(v.astype(BF16), ke.astype(BF16))
    st_ref[...] = st_new

    @pl.when(c == pl.num_programs(2) - 1)
    def _():
        s_ref[...] = st_new.T


def _hgrn(z3, lb_logits, gw, s0, layer, state_layer, chunk, t_valid):
    bsz, t, _ = z3.shape
    nc = t // chunk
    qb, fb, ib, gb = (COL_Q // LANE, COL_F // LANE, COL_I // LANE, COL_G // LANE)
    kern = functools.partial(_hgrn_kernel, layer=layer, chunk=chunk, t_valid=t_valid, t_total=t)

    def zspec(base):
        return pl.BlockSpec((None, chunk, HG_DIM), lambda b, h, c: (b, c, base + h))

    return pl.pallas_call(
        kern,
        grid=(bsz, HG_HEADS, nc),
        in_specs=[
            zspec(qb), zspec(fb), zspec(ib), zspec(gb),
            pl.BlockSpec((DEPTH, HG_DIM), lambda b, h, c: (0, h)),
            pl.BlockSpec((1, HG_DIM), lambda b, h, c: (0, 0)),
            pl.BlockSpec((None, None, None, HG_DIM, HG_DIM), lambda b, h, c: (state_layer, b, h, 0, 0)),
        ],
        out_specs=[
            pl.BlockSpec((None, chunk, HG_DIM), lambda b, h, c: (b, c, h)),
            pl.BlockSpec((None, None, HG_DIM, HG_DIM), lambda b, h, c: (b, h, 0, 0)),
        ],
        out_shape=[
            jax.ShapeDtypeStruct((bsz, t, HG_W), F32),
            jax.ShapeDtypeStruct((bsz, HG_HEADS, HG_DIM, HG_DIM), F32),
        ],
        scratch_shapes=[pltpu.VMEM((HG_DIM, HG_DIM), F32)],
        compiler_params=_cparams(("parallel", "parallel", "arbitrary")),
    )(z3, z3, z3, z3, lb_logits, gw, s0)


def _s5_kernel(u_ref, bmat_ref, cmat_ref, a_ref, d_ref, wglu_ref, bglu_ref, hre0_ref, him0_ref,
               o_ref, hre_ref, him_ref, hs_ref, hst_ref, *, tc, bt, steps, lane_chunk):
    c = pl.program_id(1)

    @pl.when(c == 0)
    def _():
        hst_ref[:, 0:S5_HW] = hre0_ref[...]
        hst_ref[:, S5_HW:2 * S5_HW] = him0_ref[...]

    u = u_ref[...].reshape(tc * bt, S5_WIDTH)
    hs_ref[...] = _dot(u.astype(BF16), bmat_ref[...])

    for l0 in range(0, S5_HW, lane_chunk):
        re_sl = slice(l0, l0 + lane_chunk)
        im_sl = slice(S5_HW + l0, S5_HW + l0 + lane_chunk)
        are = a_ref[:, re_sl]
        aim = a_ref[:, im_sl]

        def step(t, carry):
            hre, him = carry
            r0 = pl.multiple_of(t * bt, bt)
            nre = are * hre - aim * him + hs_ref[pl.ds(r0, bt), re_sl]
            nim = are * him + aim * hre + hs_ref[pl.ds(r0, bt), im_sl]
            hs_ref[pl.ds(r0, bt), re_sl] = nre
            hs_ref[pl.ds(r0, bt), im_sl] = nim
            return nre, nim

        hre, him = lax.fori_loop(0, steps, step, (hst_ref[:, re_sl], hst_ref[:, im_sl]))
        hst_ref[:, re_sl] = hre
        hst_ref[:, im_sl] = him

    y = _dot(hs_ref[...].astype(BF16), cmat_ref[...]) + d_ref[...] * u
    g = jax.nn.gelu(y)
    gl = _dot(g.astype(BF16), wglu_ref[...]) + bglu_ref[...]
    o = gl[:, :S5_WIDTH] * jax.nn.sigmoid(gl[:, S5_WIDTH:])
    o_ref[...] = o.reshape(tc, bt, S5_WIDTH)

    @pl.when(c == pl.num_programs(1) - 1)
    def _():
        hre_ref[...] = hst_ref[:, 0:S5_HW]
        him_ref[...] = hst_ref[:, S5_HW:2 * S5_HW]


def _s5(u_tm, bmat, cmat, a_vec, d_vec, wglu, bglu, hre0, him0, state_layer, tc, bt, t_valid):
    t, bsz, _ = u_tm.shape
    nc = t // tc
    if t_valid < t:
        assert nc == 1
    steps = tc if t_valid == t else t_valid
    lane_chunk = min(S5_HW, max(LANE, (8 * SUBLANE * LANE) // bt))
    kern = functools.partial(_s5_kernel, tc=tc, bt=bt, steps=steps, lane_chunk=lane_chunk)
    full = lambda shape: pl.BlockSpec(shape, lambda b, c: (0,) * len(shape))
    return pl.pallas_call(
        kern,
        grid=(bsz // bt, nc),
        in_specs=[
            pl.BlockSpec((tc, bt, S5_WIDTH), lambda b, c: (c, b, 0)),
            full((S5_WIDTH, 2 * S5_HW)),
            full((2 * S5_HW, S5_WIDTH)),
            full((1, 2 * S5_HW)),
            full((1, S5_WIDTH)),
            full((S5_WIDTH, 2 * S5_WIDTH)),
            full((1, 2 * S5_WIDTH)),
            pl.BlockSpec((None, bt, S5_HW), lambda b, c: (state_layer, b, 0)),
            pl.BlockSpec((None, bt, S5_HW), lambda b, c: (state_layer, b, 0)),
        ],
        out_specs=[
            pl.BlockSpec((tc, bt, S5_WIDTH), lambda b, c: (c, b, 0)),
            pl.BlockSpec((bt, S5_HW), lambda b, c: (b, 0)),
            pl.BlockSpec((bt, S5_HW), lambda b, c: (b, 0)),
        ],
        out_shape=[
            jax.ShapeDtypeStruct((t, bsz, S5_WIDTH), F32),
            jax.ShapeDtypeStruct((bsz, S5_HW), F32),
            jax.ShapeDtypeStruct((bsz, S5_HW), F32),
        ],
        scratch_shapes=[
            pltpu.VMEM((tc * bt, 2 * S5_HW), F32),
            pltpu.VMEM((bt, 2 * S5_HW), F32),
        ],
        compiler_params=_cparams(("parallel", "arbitrary")),
    )(u_tm, bmat, cmat, a_vec, d_vec, wglu, bglu, hre0, him0)


def _s5_params(a_re, a_im, b_re, b_im, c_re, c_im, log_dt):
    g = S5_GROUPS
    dt = jnp.exp(log_dt)[:, None]
    er = jnp.exp(a_re * dt)
    abr = er * jnp.cos(a_im * dt)
    abi = er * jnp.sin(a_im * dt)
    den = a_re * a_re + a_im * a_im
    xr = abr - 1.0
    cr = (xr * a_re + abi * a_im) / den
    ci = (abi * a_re - xr * a_im) / den
    bbr = cr[..., None] * b_re - ci[..., None] * b_im
    bbi = cr[..., None] * b_im + ci[..., None] * b_re
    eye = jnp.eye(g, dtype=F32)
    bd_in = lambda x: jnp.einsum('gnc,gh->gchn', x, eye).reshape(S5_WIDTH, S5_HW)
    bd_out = lambda x: jnp.einsum('gcn,gh->gnhc', x, eye).reshape(S5_HW, S5_WIDTH)
    bmat = jnp.concatenate([bd_in(bbr), bd_in(bbi)], axis=1).astype(BF16)
    cmat = jnp.concatenate([bd_out(c_re), -bd_out(c_im)], axis=0).astype(BF16)
    a_vec = jnp.concatenate([abr.reshape(1, S5_HW), abi.reshape(1, S5_HW)], axis=1)
    return bmat, cmat, a_vec


def _ssd_kernel(xbc_ref, zz_ref, zdt_ref, cw_ref, cb_ref, dtb_ref, alog_ref, dsk_ref, nw_ref,
                conv0_ref, s0_ref, y_ref, s_ref, cn_ref, st_ref, tail_ref,
                *, chunk, t_valid, t_total):
    c = pl.program_id(1)
    nc = t_total // chunk
    L = chunk
    TAIL = SUBLANE

    @pl.when(c == 0)
    def _():
        st_ref[...] = s0_ref[...]
        tail_ref[...] = jnp.zeros((TAIL, M2_CONV_DIM), F32)
        tail_ref[TAIL - (M2_CONV - 1):TAIL, :] = conv0_ref[...]

    xbc = xbc_ref[...]
    ext = jnp.concatenate([tail_ref[...], xbc], axis=0)
    conv = cb_ref[...]
    for j in range(M2_CONV):
        o0 = TAIL - (M2_CONV - 1) + j
        conv = conv + cw_ref[j:j + 1, :] * ext[o0:o0 + L, :]
    tail_ref[...] = xbc[L - TAIL:L, :]

    last_valid = t_valid - (nc - 1) * L

    @pl.when(c == nc - 1)
    def _():
        cn_ref[...] = ext[TAIL + last_valid - (M2_CONV - 1):TAIL + last_valid, :]

    act = _silu(conv)
    xs = act[:, :M2_INNER]
    bm = act[:, M2_INNER:M2_INNER + M2_BC].astype(BF16)
    cm = act[:, M2_INNER + M2_BC:].astype(BF16)

    dtr = zdt_ref[...] + dtb_ref[...]
    dt = jnp.maximum(dtr, 0.0) + jnp.log1p(jnp.exp(-jnp.abs(dtr)))
    if t_valid < t_total:
        valid = (c * L + lax.broadcasted_iota(jnp.int32, (L, LANE), 0)) < t_valid
        dt = jnp.where(valid, dt, 0.0)
    acs = _cumsum_rows(dt * (-jnp.exp(alog_ref[...])))
    acs_t = acs.T
    acs_end = acs[L - 1:L, :]
    e_acs = jnp.exp(acs)
    e_end = jnp.exp(acs_end - acs)
    e_tot = jnp.exp(acs_end)

    causal = lax.broadcasted_iota(jnp.int32, (L, L), 0) >= lax.broadcasted_iota(jnp.int32, (L, L), 1)
    lo_lane = lax.broadcasted_iota(jnp.int32, (L, LANE), 1) < M2_HEADDIM
    lo_row = lax.broadcasted_iota(jnp.int32, (LANE, LANE), 0) < M2_HEADDIM

    def pair_cols(a, h0):
        return jnp.where(lo_lane, a[:, h0:h0 + 1], a[:, h0 + 1:h0 + 2])

    ys = []
    for g in range(M2_NGROUPS):
        bg = bm[:, g * M2_STATE:(g + 1) * M2_STATE]
        cg = cm[:, g * M2_STATE:(g + 1) * M2_STATE]
        cb = _dot_nt(cg, bg)
        for pr in range(2):
            p = g * 2 + pr
            h0 = 2 * p
            xp = xs[:, p * LANE:(p + 1) * LANE]
            xdt = xp * pair_cols(dt, h0)
            ydiag = None
            for hh in range(2):
                h = h0 + hh
                lm = jnp.where(causal, jnp.exp(jnp.minimum(acs[:, h:h + 1] - acs_t[h:h + 1, :], 0.0)), 0.0)
                mk = lo_lane if hh == 0 else jnp.logical_not(lo_lane)
                part = _dot((cb * lm).astype(BF16), jnp.where(mk, xdt, 0.0).astype(BF16))
                ydiag = part if ydiag is None else ydiag + part
            sp = st_ref[p]
            yoff = pair_cols(e_acs, h0) * _dot_nt(cg, sp.astype(BF16))
            ys.append(ydiag + yoff + dsk_ref[:, p * LANE:(p + 1) * LANE] * xp)
            xe = (xdt * pair_cols(e_end, h0)).astype(BF16)
            scale = jnp.where(lo_row, e_tot[:, h0:h0 + 1], e_tot[:, h0 + 1:h0 + 2])
            st_ref[p] = scale * sp + _dot_tn(xe, bg)

    y = jnp.concatenate(ys, axis=1) * _silu(zz_ref[...])
    nw = nw_ref[...]
    outs = []
    for g in range(M2_NGROUPS):
        sl = slice(g * M2_NORM_W, (g + 1) * M2_NORM_W)
        yg = y[:, sl]
        outs.append(yg * _rms_scale(yg) * nw[:, sl])
    y_ref[...] = jnp.concatenate(outs, axis=1)

    @pl.when(c == nc - 1)
    def _():
        s_ref[...] = st_ref[...]


def _ssd(z3, cw, cb, dtb, alog, dsk, nw, conv0, s0, state_layer, chunk, t_valid):
    bsz, t, _ = z3.shape
    nc = t // chunk
    assert chunk >= SUBLANE and t_valid - (nc - 1) * chunk >= 1
    kern = functools.partial(_ssd_kernel, chunk=chunk, t_valid=t_valid, t_total=t)
    full = lambda shape: pl.BlockSpec(shape, lambda b, c: (0,) * len(shape))
    return pl.pallas_call(
        kern,
        grid=(bsz, nc),
        in_specs=[
            pl.BlockSpec((None, chunk, M2_CONV_DIM), lambda b, c: (b, c, COL_XBC // M2_CONV_DIM)),
            pl.BlockSpec((None, chunk, M2_INNER), lambda b, c: (b, c, COL_Z // M2_INNER)),
            pl.BlockSpec((None, chunk, LANE), lambda b, c: (b, c, COL_DT // LANE)),
            full((M2_CONV, M2_CONV_DIM)),
            full((1, M2_CONV_DIM)),
            full((1, LANE)),
            full((1, LANE)),
            full((1, M2_INNER)),
            full((1, M2_INNER)),
            pl.BlockSpec((None, None, M2_CONV - 1, M2_CONV_DIM), lambda b, c: (state_layer, b, 0, 0)),
            pl.BlockSpec((None, None, M2_PAIRS, LANE, M2_STATE), lambda b, c: (state_layer, b, 0, 0, 0)),
        ],
        out_specs=[
            pl.BlockSpec((None, chunk, M2_INNER), lambda b, c: (b, c, 0)),
            pl.BlockSpec((None, M2_PAIRS, LANE, M2_STATE), lambda b, c: (b, 0, 0, 0)),
            pl.BlockSpec((None, M2_CONV - 1, M2_CONV_DIM), lambda b, c: (b, 0, 0)),
        ],
        out_shape=[
            jax.ShapeDtypeStruct((bsz, t, M2_INNER), F32),
            jax.ShapeDtypeStruct((bsz, M2_PAIRS, LANE, M2_STATE), F32),
            jax.ShapeDtypeStruct((bsz, M2_CONV - 1, M2_CONV_DIM), F32),
        ],
        scratch_shapes=[
            pltpu.VMEM((M2_PAIRS, LANE, M2_STATE), F32),
            pltpu.VMEM((SUBLANE, M2_CONV_DIM), F32),
        ],
        compiler_params=_cparams(("parallel", "arbitrary")),
    )(z3, z3, z3, cw, cb, dtb, alog, dsk, nw, conv0, s0)


def _merge_kernel(oa_ref, ob_ref, oc_ref, ga_ref, gb_ref, gc_ref, h_ref, wa_ref, wb_ref, wc_ref, wo_ref, o_ref):
    m = jax.nn.sigmoid(ga_ref[...]) * _dot(oa_ref[...].astype(BF16), wa_ref[...])
    m = m + jax.nn.sigmoid(gb_ref[...]) * _dot(ob_ref[...].astype(BF16), wb_ref[...])
    m = m + jax.nn.sigmoid(gc_ref[...]) * _dot(oc_ref[...].astype(BF16), wc_ref[...])
    o_ref[...] = h_ref[...] + _dot(m.astype(BF16), wo_ref[...])


def _merge(oa, ob, oc, z, h, wa, wb, wc, wo):
    n = h.shape[0]
    tm = _row_tile(n, 512)
    gblk = COL_GATE // D_MODEL
    row = lambda w, j=0: pl.BlockSpec((tm, w), lambda i: (i, j))
    full = lambda shape: pl.BlockSpec(shape, lambda i: (0,) * len(shape))
    return pl.pallas_call(
        _merge_kernel,
        grid=(n // tm,),
        in_specs=[
            row(HG_W), row(S5_WIDTH), row(M2_INNER),
            row(D_MODEL, gblk), row(D_MODEL, gblk + 1), row(D_MODEL, gblk + 2),
            row(D_MODEL),
            full((HG_W, D_MODEL)), full((S5_WIDTH, D_MODEL)), full((M2_INNER, D_MODEL)),
            full((D_MODEL, D_MODEL)),
        ],
        out_specs=row(D_MODEL),
        out_shape=jax.ShapeDtypeStruct((n, D_MODEL), F32),
        compiler_params=_cparams(("parallel",)),
    )(oa, ob, oc, z, z, z, h, wa, wb, wc, wo)


R_E0 = E_GROUPS


def _moe_kernel(h_ref, nw_ref, wr_ref, br_ref, wg_ref, wu_ref, wd_ref, o_ref, xn_ref, comb_ref, acc_ref):
    e = pl.program_id(1)
    tm = h_ref.shape[0]
    lane = lax.broadcasted_iota(jnp.int32, (tm, LANE), 1)
    neg = -jnp.inf

    @pl.when(e == 0)
    def _():
        x = h_ref[...]
        xn = x * _rms_scale(x) * nw_ref[...]
        xn_ref[...] = xn.astype(BF16)
        logits = jnp.dot(xn, wr_ref[...], preferred_element_type=F32,
                         precision=lax.Precision.HIGHEST) + br_ref[...]
        gl = jnp.where(lane < E_GROUPS, logits, neg)
        gmax = jnp.max(gl, axis=1, keepdims=True)
        gidx = jnp.min(jnp.where(gl == gmax, lane, LANE), axis=1, keepdims=True)
        p_group = 1.0 / jnp.sum(jnp.exp(gl - gmax), axis=1, keepdims=True)
        in_grp = (lane >= R_E0) & (lane < R_E0 + N_EXPERTS) & (((lane - R_E0) >> 2) == gidx)
        ev = jnp.where(in_grp, logits, neg)
        v1 = jnp.max(ev, axis=1, keepdims=True)
        i1 = jnp.min(jnp.where(ev == v1, lane, LANE), axis=1, keepdims=True)
        ev2 = jnp.where(lane == i1, neg, ev)
        v2 = jnp.max(ev2, axis=1, keepdims=True)
        i2 = jnp.min(jnp.where(ev2 == v2, lane, LANE), axis=1, keepdims=True)
        e2 = jnp.exp(v2 - v1)
        w1 = 1.0 / (1.0 + e2)
        comb_ref[...] = jnp.where(lane == i1, w1 * p_group, jnp.where(lane == i2, e2 * w1 * p_group, 0.0))
        acc_ref[...] = jnp.zeros_like(acc_ref)

    xn = xn_ref[...]
    hid = _silu(_dot(xn, wg_ref[...])) * _dot(xn, wu_ref[...])
    ce = jnp.sum(jnp.where(lane == R_E0 + e, comb_ref[...], 0.0), axis=1, keepdims=True)
    acc_ref[...] += ce * _dot(hid.astype(BF16), wd_ref[...])

    @pl.when(e == N_EXPERTS - 1)
    def _():
        o_ref[...] = h_ref[...] + acc_ref[...]


def _moe(h, nw, wr, br, wg, wu, wd, layer):
    n = h.shape[0]
    tm = _row_tile(n, 1024)
    return pl.pallas_call(
        _moe_kernel,
        grid=(n // tm, N_EXPERTS),
        in_specs=[
            pl.BlockSpec((tm, D_MODEL), lambda i, e: (i, 0)),
            pl.BlockSpec((1, D_MODEL), lambda i, e: (0, 0)),
            pl.BlockSpec((D_MODEL, LANE), lambda i, e: (0, 0)),
            pl.BlockSpec((1, LANE), lambda i, e: (0, 0)),
            pl.BlockSpec((None, None, D_MODEL, E_FF), lambda i, e: (layer, e, 0, 0)),
            pl.BlockSpec((None, None, D_MODEL, E_FF), lambda i, e: (layer, e, 0, 0)),
            pl.BlockSpec((None, None, E_FF, D_MODEL), lambda i, e: (layer, e, 0, 0)),
        ],
        out_specs=pl.BlockSpec((tm, D_MODEL), lambda i, e: (i, 0)),
        out_shape=jax.ShapeDtypeStruct((n, D_MODEL), F32),
        scratch_shapes=[
            pltpu.VMEM((tm, D_MODEL), BF16),
            pltpu.VMEM((tm, LANE), F32),
            pltpu.VMEM((tm, D_MODEL), F32),
        ],
        compiler_params=_cparams(("parallel", "arbitrary")),
    )(h, nw, wr, br, wg, wu, wd)


def _ple_kernel(h_ref, p_ref, nw_ref, wg_ref, wp_ref, nf_ref, o_ref, y_ref):
    x = h_ref[...]
    xn = (x * _rms_scale(x) * nw_ref[...]).astype(BF16)
    out = x + jax.nn.sigmoid(_dot(xn, wg_ref[...])) * _dot(p_ref[...].astype(BF16), wp_ref[...])
    o_ref[...] = out
    y_ref[...] = out * _rms_scale(out) * nf_ref[...]


def _ple(h, p, nw, wg, wp, nf):
    n = h.shape[0]
    tm = _row_tile(n, 512)
    row = lambda w: pl.BlockSpec((tm, w), lambda i: (i, 0))
    full = lambda shape: pl.BlockSpec(shape, lambda i: (0,) * len(shape))
    return pl.pallas_call(
        _ple_kernel,
        grid=(n // tm,),
        in_specs=[row(D_MODEL), row(PLE_DIM), full((1, D_MODEL)), full((D_MODEL, D_MODEL)),
                  full((PLE_DIM, D_MODEL)), full((1, D_MODEL))],
        out_specs=[row(D_MODEL), row(D_MODEL)],
        out_shape=[jax.ShapeDtypeStruct((n, D_MODEL), F32), jax.ShapeDtypeStruct((n, D_MODEL), F32)],
        compiler_params=_cparams(("parallel",)),
    )(h, p, nw, wg, wp, nf)


def _prep_weights(w):
    win = w['w_in']
    pad = jnp.zeros((DEPTH, D_MODEL, Z_COLS - IN_COLS), F32)
    win_r = jnp.concatenate([
        win[:, :, _O_XBC:_O_XBC + M2_CONV_DIM], win[:, :, _O_Z:_O_Z + M2_INNER],
        win[:, :, _O_GATE:], win[:, :, :_O_Z], win[:, :, _O_DT:_O_DT + M2_HEADS], pad], axis=2).astype(BF16)
    lane_pad = lambda a: jnp.pad(a, ((0, 0), (0, LANE - a.shape[1])))
    s5 = [_s5_params(w['s5_a_re'][i], w['s5_a_im'][i], w['s5_b_re'][i], w['s5_b_im'][i],
                     w['s5_c_re'][i], w['s5_c_im'][i], w['s5_log_dt'][i]) for i in range(DEPTH)]
    wr = jnp.concatenate([w['w_rg'], w['w_re'],
                          jnp.zeros((DEPTH, D_MODEL, LANE - E_GROUPS - N_EXPERTS), F32)], axis=2)
    br = jnp.concatenate([w['b_rg'], w['b_re'],
                          jnp.zeros((DEPTH, LANE - E_GROUPS - N_EXPERTS), F32)], axis=1)
    return dict(
        win=win_r,
        s5=s5,
        s5_d=w['s5_d'].reshape(DEPTH, 1, S5_WIDTH),
        wglu=w['s5_w_glu'].astype(BF16),
        bglu=w['s5_b_glu'].reshape(DEPTH, 1, 2 * S5_WIDTH),
        dtb=lane_pad(w['m2_dt_bias']).reshape(DEPTH, 1, LANE),
        alog=lane_pad(w['m2_a_log']).reshape(DEPTH, 1, LANE),
        dsk=jnp.repeat(w['m2_d'], M2_HEADDIM, axis=1).reshape(DEPTH, 1, M2_INNER),
        wa=w['w_br_hg'].astype(BF16), wb=w['w_br_s5'].astype(BF16), wc=w['w_br_m2'].astype(BF16),
        wo=w['w_out'].astype(BF16),
        wr=wr, br=br.reshape(DEPTH, 1, LANE),
        weg=w['w_e_gate'].astype(BF16), weu=w['w_e_up'].astype(BF16), wed=w['w_e_down'].astype(BF16),
        wpg=w['w_ple_gate'].astype(BF16), wpp=w['w_ple_proj'].astype(BF16),
    )


def _trunk(x, p, states, w, pw, t_valid, cfg):
    bsz, t, _ = x.shape
    n = bsz * t
    st_hg, st_re, st_im, st_ssm, st_conv = states
    per_layer = st_hg.shape[0] == DEPTH
    st_re = st_re.reshape(st_re.shape[0], bsz, S5_HW)
    st_im = st_im.reshape(st_im.shape[0], bsz, S5_HW)
    st_ssm = st_ssm.reshape(st_ssm.shape[0], bsz, M2_PAIRS, LANE, M2_STATE)
    h = x.reshape(n, D_MODEL)
    new = []
    y = None
    for i in range(DEPTH):
        sl = i if per_layer else 0
        z = _norm_matmul(h, w['norm_mix'][i].reshape(1, D_MODEL), pw['win'][i])
        z3 = z.reshape(bsz, t, Z_COLS)
        oa, s_hg = _hgrn(z3, w['hg_lb_logits'], w['hg_gnorm'][i].reshape(1, HG_DIM), st_hg, i, sl,
                         cfg['hg_chunk'], t_valid)
        u_tm = jnp.transpose(z3[:, :, COL_U:COL_U + S5_WIDTH], (1, 0, 2))
        bmat, cmat, a_vec = pw['s5'][i]
        ob_tm, s_re, s_im = _s5(u_tm, bmat, cmat, a_vec, pw['s5_d'][i], pw['wglu'][i], pw['bglu'][i],
                                st_re, st_im, sl, cfg['s5_tc'], cfg['s5_bt'], t_valid)
        ob = jnp.transpose(ob_tm, (1, 0, 2))
        oc, s_ssm, conv_new = _ssd(z3, w['m2_conv_w'][i], w['m2_conv_b'][i].reshape(1, M2_CONV_DIM),
                                   pw['dtb'][i], pw['alog'][i], pw['dsk'][i],
                                   w['m2_norm'][i].reshape(1, M2_INNER), st_conv, st_ssm, sl,
                                   cfg['m2_chunk'], t_valid)
        h = _merge(oa.reshape(n, HG_W), ob.reshape(n, S5_WIDTH), oc.reshape(n, M2_INNER), z, h,
                   pw['wa'][i], pw['wb'][i], pw['wc'][i], pw['wo'][i])
        h = _moe(h, w['norm_ffn'][i].reshape(1, D_MODEL), pw['wr'][i], pw['br'][i],
                 pw['weg'], pw['weu'], pw['wed'], i)
        h, y = _ple(h, p[i].reshape(n, PLE_DIM), w['norm_ple'][i].reshape(1, D_MODEL),
                    pw['wpg'][i], pw['wpp'][i], w['norm_final'].reshape(1, D_MODEL))
        new.append((s_hg, s_re.reshape(bsz, S5_GROUPS, S5_STATE), s_im.reshape(bsz, S5_GROUPS, S5_STATE),
                    s_ssm.reshape(bsz, M2_HEADS, M2_HEADDIM, M2_STATE), conv_new))
    stacked = tuple(jnp.stack([nl[j] for nl in new]) for j in range(5))
    return y.reshape(bsz, t, D_MODEL), stacked


def _zero_states(bsz):
    return (jnp.zeros((1, bsz, HG_HEADS, HG_DIM, HG_DIM), F32),
            jnp.zeros((1, bsz, S5_GROUPS, S5_STATE), F32),
            jnp.zeros((1, bsz, S5_GROUPS, S5_STATE), F32),
            jnp.zeros((1, bsz, M2_HEADS, M2_HEADDIM, M2_STATE), F32),
            jnp.zeros((1, bsz, M2_CONV - 1, M2_CONV_DIM), F32))


def _pad_time(a, axis, t_pad):
    t = a.shape[axis]
    if t == t_pad:
        return a
    widths = [(0, 0)] * a.ndim
    widths[axis] = (0, t_pad - t)
    return jnp.pad(a, widths)


def kernel(x_prompt, x_sample, state_hgrn, state_s5_re, state_s5_im, state_ssm, state_conv,
           p_prompt, p_sample,
           norm_mix, w_in, hg_lb_logits, hg_gnorm, w_br_hg,
           s5_a_re, s5_a_im, s5_b_re, s5_b_im, s5_c_re, s5_c_im, s5_d, s5_log_dt, s5_w_glu, s5_b_glu, w_br_s5,
           m2_conv_w, m2_conv_b, m2_dt_bias, m2_a_log, m2_d, m2_norm, w_br_m2,
           w_out,
           norm_ffn, w_rg, b_rg, w_re, b_re, w_e_gate, w_e_up, w_e_down,
           norm_ple, w_ple_gate, w_ple_proj,
           norm_final):
    w = dict(norm_mix=norm_mix, w_in=w_in, hg_lb_logits=hg_lb_logits, hg_gnorm=hg_gnorm, w_br_hg=w_br_hg,
             s5_a_re=s5_a_re, s5_a_im=s5_a_im, s5_b_re=s5_b_re, s5_b_im=s5_b_im, s5_c_re=s5_c_re,
             s5_c_im=s5_c_im, s5_d=s5_d, s5_log_dt=s5_log_dt, s5_w_glu=s5_w_glu, s5_b_glu=s5_b_glu,
             w_br_s5=w_br_s5, m2_conv_w=m2_conv_w, m2_conv_b=m2_conv_b, m2_dt_bias=m2_dt_bias,
             m2_a_log=m2_a_log, m2_d=m2_d, m2_norm=m2_norm, w_br_m2=w_br_m2, w_out=w_out,
             norm_ffn=norm_ffn, w_rg=w_rg, b_rg=b_rg, w_re=w_re, b_re=b_re, w_e_gate=w_e_gate,
             w_e_up=w_e_up, w_e_down=w_e_down, norm_ple=norm_ple, w_ple_gate=w_ple_gate,
             w_ple_proj=w_ple_proj, norm_final=norm_final)
    pw = _prep_weights(w)

    bp, tp, _ = x_prompt.shape
    cfg_p = dict(hg_chunk=min(64, tp), s5_tc=min(64, tp), s5_bt=SUBLANE, m2_chunk=min(128, tp))
    y_p, st_p = _trunk(x_prompt, p_prompt, _zero_states(bp), w, pw, tp, cfg_p)

    bs, ts, _ = x_sample.shape
    ts_pad = -(-ts // SUBLANE) * SUBLANE
    cfg_s = dict(hg_chunk=ts_pad, s5_tc=ts_pad, s5_bt=min(32, bs), m2_chunk=ts_pad)
    y_s, st_s = _trunk(_pad_time(x_sample, 1, ts_pad), _pad_time(p_sample, 2, ts_pad),
                       (state_hgrn, state_s5_re, state_s5_im, state_ssm, state_conv), w, pw, ts, cfg_s)
    return (y_p, y_s[:, :ts]) + st_p + st_s
```

```python
import functools

import jax
import jax.numpy as jnp
from jax import lax
from jax.experimental import pallas as pl
from jax.experimental.pallas import tpu as pltpu

F32 = jnp.float32
BF16 = jnp.bfloat16

D_MODEL = 1024
DEPTH = 2
PLE_DIM = 256
NORM_EPS = 1e-6

HG_HEADS = 4
HG_DIM = 128
HG_W = HG_HEADS * HG_DIM

S5_WIDTH = 512
S5_GROUP = 16
S5_GROUPS = S5_WIDTH // S5_GROUP
S5_STATE = 64
S5_HW = S5_GROUPS * S5_STATE

M2_INNER = 1024
M2_HEADDIM = 64
M2_HEADS = M2_INNER // M2_HEADDIM
M2_NGROUPS = 4
M2_STATE = 128
M2_CONV = 4
M2_BC = M2_NGROUPS * M2_STATE
M2_CONV_DIM = M2_INNER + 2 * M2_BC
M2_PAIRS = M2_HEADS // 2
M2_NORM_W = M2_INNER // M2_NGROUPS

N_BRANCH = 3
E_GROUPS = 4
E_PER_GROUP = 4
N_EXPERTS = E_GROUPS * E_PER_GROUP
E_FF = 256

LANE = 128
SUBLANE = 8

COL_XBC = 0
COL_Z = COL_XBC + M2_CONV_DIM
COL_GATE = COL_Z + M2_INNER
COL_Q = COL_GATE + N_BRANCH * D_MODEL
COL_F = COL_Q + HG_W
COL_I = COL_F + HG_W
COL_G = COL_I + HG_W
COL_U = COL_G + HG_W
COL_DT = COL_U + S5_WIDTH
Z_TN = 1280
Z_COLS = 7 * Z_TN
assert COL_DT + LANE <= Z_COLS and COL_DT % LANE == 0

_O_Q, _O_F, _O_I, _O_G, _O_U = 0, 512, 1024, 1536, 2048
_O_Z = 2560
_O_XBC = _O_Z + M2_INNER
_O_DT = _O_XBC + M2_CONV_DIM
_O_GATE = _O_DT + M2_HEADS
IN_COLS = _O_GATE + N_BRANCH * D_MODEL

VMEM_LIMIT = 56 * 1024 * 1024


def _cparams(sem):
    return pltpu.CompilerParams(dimension_semantics=sem, vmem_limit_bytes=VMEM_LIMIT)


def _rms_scale(x):
    return lax.rsqrt(jnp.mean(x * x, axis=-1, keepdims=True) + NORM_EPS)


def _silu(x):
    return x * jax.nn.sigmoid(x)


def _dot(a, b):
    return jnp.dot(a, b, preferred_element_type=F32)


def _dot_nt(a, b):
    return lax.dot_general(a, b, (((1,), (1,)), ((), ())), preferred_element_type=F32)


def _dot_tn(a, b):
    return lax.dot_general(a, b, (((0,), (0,)), ((), ())), preferred_element_type=F32)


def _cumsum_rows(x):
    n = x.shape[0]
    row = lax.broadcasted_iota(jnp.int32, x.shape, 0)
    s = 1
    while s < n:
        x = x + jnp.where(row >= s, pltpu.roll(x, s, 0), 0.0)
        s *= 2
    return x


def _row_tile(n, pref):
    t = min(pref, n)
    assert n % t == 0
    return t


def _norm_matmul_kernel(x_ref, nw_ref, w_ref, o_ref, xn_ref):
    @pl.when(pl.program_id(1) == 0)
    def _():
        x = x_ref[...]
        xn_ref[...] = (x * _rms_scale(x) * nw_ref[...]).astype(BF16)

    o_ref[...] = _dot(xn_ref[...], w_ref[...])


def _norm_matmul(h, nw, w, layer):
    n = h.shape[0]
    cols = w.shape[2]
    tm = _row_tile(n, 1024)
    tn = Z_TN
    return pl.pallas_call(
        _norm_matmul_kernel,
        grid=(n // tm, cols // tn),
        in_specs=[
            pl.BlockSpec((tm, D_MODEL), lambda i, j: (i, 0)),
            pl.BlockSpec((1, D_MODEL), lambda i, j: (0, 0)),
            pl.BlockSpec((None, D_MODEL, tn), lambda i, j: (layer, 0, j)),
        ],
        out_specs=pl.BlockSpec((tm, tn), lambda i, j: (i, j)),
        out_shape=jax.ShapeDtypeStruct((n, cols), F32),
        scratch_shapes=[pltpu.VMEM((tm, D_MODEL), BF16)],
        compiler_params=_cparams(("parallel", "arbitrary")),
    )(h, nw, w)


def _hgrn_chunk(q, k, v, logf, st, lev, dcode):
    C = q.shape[0]
    b = _cumsum_rows(logf)
    b_end = b[C - 1:C, :]
    o = _dot_nt((q * jnp.exp(b)).astype(BF16), st.astype(BF16))

    sc = jnp.where(dcode == 0, jnp.sum(q * k, axis=1, keepdims=True), 0.0)
    for j in range(1, min(SUBLANE, C)):
        kr = pltpu.roll(k, j, 0)
        br = pltpu.roll(b, j, 0)
        tj = jnp.sum(q * kr * jnp.exp(jnp.minimum(b - br, 0.0)), axis=1, keepdims=True)
        sc = jnp.where(dcode == j, tj, sc)

    m = C // 2
    while m >= SUBLANE:
        parts = []
        for p0 in range(0, C, 2 * m):
            parts.append(jnp.broadcast_to(b[p0 + m - 1:p0 + m, :], (2 * m, HG_DIM)))
        r = parts[0] if len(parts) == 1 else jnp.concatenate(parts, axis=0)
        qe = q * jnp.exp(jnp.minimum(b - r, 0.0))
        ke = k * jnp.exp(jnp.minimum(r - b, 0.0))
        pm = _dot_nt(qe.astype(BF16), ke.astype(BF16))
        sc = jnp.where((lev >> (m.bit_length() - 1)) == 1, pm, sc)
        m //= 2

    v16 = v.astype(BF16)
    o = o + _dot(sc.astype(BF16), v16)
    ke = k * jnp.exp(b_end - b)
    st_new = st * jnp.exp(b_end) + _dot_tn(v16, ke.astype(BF16))
    return o, st_new


def _hgrn_kernel(zq_ref, zf_ref, zi_ref, zg_ref, lbl_ref, gw_ref, s0_ref, *rest,
                 layer, chunk, bt, t_valid, t_total):
    o_ref, s_ref, st_ref = rest[-3:]
    c = pl.program_id(1)
    C = chunk

    @pl.when(c == 0)
    def _():
        for bi in range(bt):
            for h in range(HG_HEADS):
                st_ref[bi, h] = s0_ref[bi, h].T

    lg = lbl_ref[...]
    e = jnp.exp(lg - jnp.max(lg, axis=0, keepdims=True))
    prob = e / jnp.sum(e, axis=0, keepdims=True)
    lb_all = jnp.zeros((1, HG_W), F32)
    for j in range(1, layer + 1):
        lb_all = lb_all + prob[j:j + 1]

    rr = lax.broadcasted_iota(jnp.int32, (C, C), 0)
    cc = lax.broadcasted_iota(jnp.int32, (C, C), 1)
    xr = rr ^ cc
    lev = jnp.where(rr > cc, xr, -1)
    dcode = jnp.where(xr < SUBLANE, rr - cc, -1)
    if t_valid < t_total:
        valid = (c * C + lax.broadcasted_iota(jnp.int32, (C, HG_DIM), 0)) < t_valid
    gw = gw_ref[...]

    for bi in range(bt):
        for h in range(HG_HEADS):
            sl = slice(h * HG_DIM, (h + 1) * HG_DIM)
            lb = lb_all[:, sl]
            zf = zf_ref[bi, :, sl]
            q = _silu(zq_ref[bi, :, sl])
            logf = jnp.log(lb + (1.0 - lb) * jax.nn.sigmoid(zf))
            k = (1.0 - lb) * jax.nn.sigmoid(-zf)
            if t_valid < t_total:
                logf = jnp.where(valid, logf, 0.0)
                k = jnp.where(valid, k, 0.0)
            o, st_new = _hgrn_chunk(q, k, zi_ref[bi, :, sl], logf, st_ref[bi, h], lev, dcode)
            o_ref[bi, :, sl] = o * _rms_scale(o) * gw * _silu(zg_ref[bi, :, sl])
            st_ref[bi, h] = st_new

    @pl.when(c == pl.num_programs(1) - 1)
    def _():
        for bi in range(bt):
            for h in range(HG_HEADS):
                s_ref[bi, h] = st_ref[bi, h].T


def _hgrn(z3, lb_logits, gw, s0, prev, layer, state_layer, chunk, bt, t_valid):
    bsz, t, _ = z3.shape
    nc = t // chunk
    kern = functools.partial(_hgrn_kernel, layer=layer, chunk=chunk, bt=bt, t_valid=t_valid, t_total=t)

    def zspec(col):
        return pl.BlockSpec((bt, chunk, HG_W), lambda b, c: (b, c, col // HG_W))

    in_specs = [
        zspec(COL_Q), zspec(COL_F), zspec(COL_I), zspec(COL_G),
        pl.BlockSpec((DEPTH, HG_W), lambda b, c: (0, 0)),
        pl.BlockSpec((1, HG_DIM), lambda b, c: (0, 0)),
        pl.BlockSpec((None, bt, HG_HEADS, HG_DIM, HG_DIM), lambda b, c: (state_layer, b, 0, 0, 0)),
    ]
    args = [z3, z3, z3, z3, lb_logits, gw, s0]
    aliases = {}
    if prev is not None:
        in_specs.append(pl.BlockSpec(memory_space=pl.ANY))
        args.append(prev)
        aliases = {len(args) - 1: 1}
    return pl.pallas_call(
        kern,
        grid=(bsz // bt, nc),
        in_specs=in_specs,
        out_specs=[
            pl.BlockSpec((bt, chunk, HG_W), lambda b, c: (b, c, 0)),
            pl.BlockSpec((None, bt, HG_HEADS, HG_DIM, HG_DIM), lambda b, c: (layer, b, 0, 0, 0)),
        ],
        out_shape=[
            jax.ShapeDtypeStruct((bsz, t, HG_W), F32),
            jax.ShapeDtypeStruct((DEPTH, bsz, HG_HEADS, HG_DIM, HG_DIM), F32),
        ],
        scratch_shapes=[pltpu.VMEM((bt, HG_HEADS, HG_DIM, HG_DIM), F32)],
        input_output_aliases=aliases,
        compiler_params=_cparams(("parallel", "arbitrary")),
    )(*args)


def _s5_kernel(u_ref, bmat_ref, cmat_ref, a_ref, d_ref, wglu_ref, bglu_ref, hre0_ref, him0_ref,
               o_ref, hre_ref, him_ref, hs_ref, hst_ref, *, tc, bt, steps, lane_chunk):
    c = pl.program_id(1)

    @pl.when(c == 0)
    def _():
        hst_ref[:, 0:S5_HW] = hre0_ref[...]
        hst_ref[:, S5_HW:2 * S5_HW] = him0_ref[...]

    u = u_ref[...].reshape(tc * bt, S5_WIDTH)
    hs_ref[...] = _dot(u.astype(BF16), bmat_ref[...])

    for l0 in range(0, S5_HW, lane_chunk):
        re_sl = slice(l0, l0 + lane_chunk)
        im_sl = slice(S5_HW + l0, S5_HW + l0 + lane_chunk)
        are = a_ref[:, re_sl]
        aim = a_ref[:, im_sl]

        def step(t, carry):
            hre, him = carry
            r0 = pl.multiple_of(t * bt, bt)
            nre = are * hre - aim * him + hs_ref[pl.ds(r0, bt), re_sl]
            nim = are * him + aim * hre + hs_ref[pl.ds(r0, bt), im_sl]
            hs_ref[pl.ds(r0, bt), re_sl] = nre
            hs_ref[pl.ds(r0, bt), im_sl] = nim
            return nre, nim

        hre, him = lax.fori_loop(0, steps, step, (hst_ref[:, re_sl], hst_ref[:, im_sl]))
        hst_ref[:, re_sl] = hre
        hst_ref[:, im_sl] = him

    y = _dot(hs_ref[...].astype(BF16), cmat_ref[...]) + d_ref[...] * u
    g = jax.nn.gelu(y)
    gl = _dot(g.astype(BF16), wglu_ref[...]) + bglu_ref[...]
    o = gl[:, :S5_WIDTH] * jax.nn.sigmoid(gl[:, S5_WIDTH:])
    o_ref[...] = o.reshape(tc, bt, S5_WIDTH)

    @pl.when(c == pl.num_programs(1) - 1)
    def _():
        hre_ref[...] = hst_ref[:, 0:S5_HW]
        him_ref[...] = hst_ref[:, S5_HW:2 * S5_HW]


def _s5(u_tm, bmat, cmat, a_vec, d_vec, wglu, bglu, hre0, him0, state_layer, tc, bt, t_valid):
    t, bsz, _ = u_tm.shape
    nc = t // tc
    if t_valid < t:
        assert nc == 1
    steps = tc if t_valid == t else t_valid
    lane_chunk = min(S5_HW, max(LANE, (8 * SUBLANE * LANE) // bt))
    kern = functools.partial(_s5_kernel, tc=tc, bt=bt, steps=steps, lane_chunk=lane_chunk)
    full = lambda shape: pl.BlockSpec(shape, lambda b, c: (0,) * len(shape))
    return pl.pallas_call(
        kern,
        grid=(bsz // bt, nc),
        in_specs=[
            pl.BlockSpec((tc, bt, S5_WIDTH), lambda b, c: (c, b, 0)),
            full((S5_WIDTH, 2 * S5_HW)),
            full((2 * S5_HW, S5_WIDTH)),
            full((1, 2 * S5_HW)),
            full((1, S5_WIDTH)),
            full((S5_WIDTH, 2 * S5_WIDTH)),
            full((1, 2 * S5_WIDTH)),
            pl.BlockSpec((None, bt, S5_HW), lambda b, c: (state_layer, b, 0)),
            pl.BlockSpec((None, bt, S5_HW), lambda b, c: (state_layer, b, 0)),
        ],
        out_specs=[
            pl.BlockSpec((tc, bt, S5_WIDTH), lambda b, c: (c, b, 0)),
            pl.BlockSpec((bt, S5_HW), lambda b, c: (b, 0)),
            pl.BlockSpec((bt, S5_HW), lambda b, c: (b, 0)),
        ],
        out_shape=[
            jax.ShapeDtypeStruct((t, bsz, S5_WIDTH), F32),
            jax.ShapeDtypeStruct((bsz, S5_HW), F32),
            jax.ShapeDtypeStruct((bsz, S5_HW), F32),
        ],
        scratch_shapes=[
            pltpu.VMEM((tc * bt, 2 * S5_HW), F32),
            pltpu.VMEM((bt, 2 * S5_HW), F32),
        ],
        compiler_params=_cparams(("parallel", "arbitrary")),
    )(u_tm, bmat, cmat, a_vec, d_vec, wglu, bglu, hre0, him0)


def _s5_params(a_re, a_im, b_re, b_im, c_re, c_im, log_dt):
    g = S5_GROUPS
    dt = jnp.exp(log_dt)[:, None]
    er = jnp.exp(a_re * dt)
    abr = er * jnp.cos(a_im * dt)
    abi = er * jnp.sin(a_im * dt)
    den = a_re * a_re + a_im * a_im
    xr = abr - 1.0
    cr = (xr * a_re + abi * a_im) / den
    ci = (abi * a_re - xr * a_im) / den
    bbr = cr[..., None] * b_re - ci[..., None] * b_im
    bbi = cr[..., None] * b_im + ci[..., None] * b_re
    eye = jnp.eye(g, dtype=F32)
    bd_in = lambda x: jnp.einsum('gnc,gh->gchn', x, eye).reshape(S5_WIDTH, S5_HW)
    bd_out = lambda x: jnp.einsum('gcn,gh->gnhc', x, eye).reshape(S5_HW, S5_WIDTH)
    bmat = jnp.concatenate([bd_in(bbr), bd_in(bbi)], axis=1).astype(BF16)
    cmat = jnp.concatenate([bd_out(c_re), -bd_out(c_im)], axis=0).astype(BF16)
    a_vec = jnp.concatenate([abr.reshape(1, S5_HW), abi.reshape(1, S5_HW)], axis=1)
    return bmat, cmat, a_vec


def _ssd_kernel(xbc_ref, zz_ref, zdt_ref, cw_ref, cb_ref, dtb_ref, alog_ref, dsk_ref, nw_ref,
                conv0_ref, s0_ref, *rest, chunk, t_valid, t_total):
    y_ref, s_ref, cn_ref, st_ref, tail_ref = rest[-5:]
    c = pl.program_id(1)
    nc = t_total // chunk
    L = chunk
    TAIL = SUBLANE

    @pl.when(c == 0)
    def _():
        st_ref[...] = s0_ref[...]
        tail_ref[...] = jnp.zeros((TAIL, M2_CONV_DIM), F32)
        tail_ref[TAIL - (M2_CONV - 1):TAIL, :] = conv0_ref[...]

    xbc = xbc_ref[...]
    ext = jnp.concatenate([tail_ref[...], xbc], axis=0)
    conv = cb_ref[...]
    for j in range(M2_CONV):
        o0 = TAIL - (M2_CONV - 1) + j
        conv = conv + cw_ref[j:j + 1, :] * ext[o0:o0 + L, :]
    tail_ref[...] = xbc[L - TAIL:L, :]

    last_valid = t_valid - (nc - 1) * L

    @pl.when(c == nc - 1)
    def _():
        cn_ref[...] = ext[TAIL + last_valid - (M2_CONV - 1):TAIL + last_valid, :]

    act = _silu(conv)
    xs = act[:, :M2_INNER]
    bm = act[:, M2_INNER:M2_INNER + M2_BC].astype(BF16)
    cm = act[:, M2_INNER + M2_BC:].astype(BF16)

    dtr = zdt_ref[...] + dtb_ref[...]
    dt = jnp.maximum(dtr, 0.0) + jnp.log1p(jnp.exp(-jnp.abs(dtr)))
    if t_valid < t_total:
        valid = (c * L + lax.broadcasted_iota(jnp.int32, (L, LANE), 0)) < t_valid
        dt = jnp.where(valid, dt, 0.0)
    acs = _cumsum_rows(dt * (-jnp.exp(alog_ref[...])))
    acs_t = acs.T
    acs_end = acs[L - 1:L, :]
    e_acs = jnp.exp(acs)
    e_end = jnp.exp(acs_end - acs)
    e_tot = jnp.exp(acs_end)

    causal = lax.broadcasted_iota(jnp.int32, (L, L), 0) >= lax.broadcasted_iota(jnp.int32, (L, L), 1)
    lo_lane = lax.broadcasted_iota(jnp.int32, (L, LANE), 1) < M2_HEADDIM
    lo_row = lax.broadcasted_iota(jnp.int32, (LANE, LANE), 0) < M2_HEADDIM

    def pair_cols(a, h0):
        return jnp.where(lo_lane, a[:, h0:h0 + 1], a[:, h0 + 1:h0 + 2])

    ys = []
    for g in range(M2_NGROUPS):
        bg = bm[:, g * M2_STATE:(g + 1) * M2_STATE]
        cg = cm[:, g * M2_STATE:(g + 1) * M2_STATE]
        cb = _dot_nt(cg, bg)
        for pr in range(2):
            p = g * 2 + pr
            h0 = 2 * p
            xp = xs[:, p * LANE:(p + 1) * LANE]
            xdt = xp * pair_cols(dt, h0)
            ydiag = None
            for hh in range(2):
                h = h0 + hh
                lm = jnp.where(causal, jnp.exp(jnp.minimum(acs[:, h:h + 1] - acs_t[h:h + 1, :], 0.0)), 0.0)
                mk = lo_lane if hh == 0 else jnp.logical_not(lo_lane)
                part = _dot((cb * lm).astype(BF16), jnp.where(mk, xdt, 0.0).astype(BF16))
                ydiag = part if ydiag is None else ydiag + part
            sp = st_ref[p]
            yoff = pair_cols(e_acs, h0) * _dot_nt(cg, sp.astype(BF16))
            ys.append(ydiag + yoff + dsk_ref[:, p * LANE:(p + 1) * LANE] * xp)
            xe = (xdt * pair_cols(e_end, h0)).astype(BF16)
            scale = jnp.where(lo_row, e_tot[:, h0:h0 + 1], e_tot[:, h0 + 1:h0 + 2])
            st_ref[p] = scale * sp + _dot_tn(xe, bg)

    y = jnp.concatenate(ys, axis=1) * _silu(zz_ref[...])
    nw = nw_ref[...]
    outs = []
    for g in range(M2_NGROUPS):
        sl = slice(g * M2_NORM_W, (g + 1) * M2_NORM_W)
        yg = y[:, sl]
        outs.append(yg * _rms_scale(yg) * nw[:, sl])
    y_ref[...] = jnp.concatenate(outs, axis=1)

    @pl.when(c == nc - 1)
    def _():
        s_ref[...] = st_ref[...]


def _ssd(z3, cw, cb, dtb, alog, dsk, nw, conv0, s0, prev, layer, state_layer, chunk, t_valid):
    bsz, t, _ = z3.shape
    nc = t // chunk
    assert chunk >= SUBLANE and t_valid - (nc - 1) * chunk >= 1
    kern = functools.partial(_ssd_kernel, chunk=chunk, t_valid=t_valid, t_total=t)
    full = lambda shape: pl.BlockSpec(shape, lambda b, c: (0,) * len(shape))
    extra_specs, extra_args, aliases = [], [], {}
    if prev is not None:
        extra_specs = [pl.BlockSpec(memory_space=pl.ANY)] * 2
        extra_args = list(prev)
        aliases = {11: 1, 12: 2}
    return pl.pallas_call(
        kern,
        grid=(bsz, nc),
        input_output_aliases=aliases,
        in_specs=[
            pl.BlockSpec((None, chunk, M2_CONV_DIM), lambda b, c: (b, c, COL_XBC // M2_CONV_DIM)),
            pl.BlockSpec((None, chunk, M2_INNER), lambda b, c: (b, c, COL_Z // M2_INNER)),
            pl.BlockSpec((None, chunk, LANE), lambda b, c: (b, c, COL_DT // LANE)),
            full((M2_CONV, M2_CONV_DIM)),
            full((1, M2_CONV_DIM)),
            full((1, LANE)),
            full((1, LANE)),
            full((1, M2_INNER)),
            full((1, M2_INNER)),
            pl.BlockSpec((None, None, M2_CONV - 1, M2_CONV_DIM), lambda b, c: (state_layer, b, 0, 0)),
            pl.BlockSpec((None, None, M2_PAIRS, LANE, M2_STATE), lambda b, c: (state_layer, b, 0, 0, 0)),
        ] + extra_specs,
        out_specs=[
            pl.BlockSpec((None, chunk, M2_INNER), lambda b, c: (b, c, 0)),
            pl.BlockSpec((None, None, M2_PAIRS, LANE, M2_STATE), lambda b, c: (layer, b, 0, 0, 0)),
            pl.BlockSpec((None, None, M2_CONV - 1, M2_CONV_DIM), lambda b, c: (layer, b, 0, 0)),
        ],
        out_shape=[
            jax.ShapeDtypeStruct((bsz, t, M2_INNER), F32),
            jax.ShapeDtypeStruct((DEPTH, bsz, M2_PAIRS, LANE, M2_STATE), F32),
            jax.ShapeDtypeStruct((DEPTH, bsz, M2_CONV - 1, M2_CONV_DIM), F32),
        ],
        scratch_shapes=[
            pltpu.VMEM((M2_PAIRS, LANE, M2_STATE), F32),
            pltpu.VMEM((SUBLANE, M2_CONV_DIM), F32),
        ],
        compiler_params=_cparams(("parallel", "arbitrary")),
    )(z3, z3, z3, cw, cb, dtb, alog, dsk, nw, conv0, s0, *extra_args)


def _merge_kernel(oa_ref, ob_ref, oc_ref, ga_ref, gb_ref, gc_ref, h_ref, wa_ref, wb_ref, wc_ref, wo_ref, o_ref):
    m = jax.nn.sigmoid(ga_ref[...]) * _dot(oa_ref[...].astype(BF16), wa_ref[...])
    m = m + jax.nn.sigmoid(gb_ref[...]) * _dot(ob_ref[...].astype(BF16), wb_ref[...])
    m = m + jax.nn.sigmoid(gc_ref[...]) * _dot(oc_ref[...].astype(BF16), wc_ref[...])
    o_ref[...] = h_ref[...] + _dot(m.astype(BF16), wo_ref[...])


def _merge(oa, ob, oc, z, h, wa, wb, wc, wo, layer):
    n = h.shape[0]
    tm = _row_tile(n, 512)
    gblk = COL_GATE // D_MODEL
    row = lambda w, j=0: pl.BlockSpec((tm, w), lambda i: (i, j))
    full = lambda shape: pl.BlockSpec((None,) + shape, lambda i: (layer,) + (0,) * len(shape))
    return pl.pallas_call(
        _merge_kernel,
        grid=(n // tm,),
        in_specs=[
            row(HG_W), row(S5_WIDTH), row(M2_INNER),
            row(D_MODEL, gblk), row(D_MODEL, gblk + 1), row(D_MODEL, gblk + 2),
            row(D_MODEL),
            full((HG_W, D_MODEL)), full((S5_WIDTH, D_MODEL)), full((M2_INNER, D_MODEL)),
            full((D_MODEL, D_MODEL)),
        ],
        out_specs=row(D_MODEL),
        out_shape=jax.ShapeDtypeStruct((n, D_MODEL), F32),
        compiler_params=_cparams(("parallel",)),
    )(oa, ob, oc, z, z, z, h, wa, wb, wc, wo)


R_E0 = E_GROUPS


def _moe_kernel(h_ref, nw_ref, wr_ref, br_ref, wg_ref, wu_ref, wd_ref, o_ref, xn_ref, comb_ref, acc_ref):
    e = pl.program_id(1)
    tm = h_ref.shape[0]
    lane = lax.broadcasted_iota(jnp.int32, (tm, LANE), 1)
    neg = -jnp.inf

    @pl.when(e == 0)
    def _():
        x = h_ref[...]
        xn = x * _rms_scale(x) * nw_ref[...]
        xn_ref[...] = xn.astype(BF16)
        logits = jnp.dot(xn, wr_ref[...], preferred_element_type=F32,
                         precision=lax.Precision.HIGHEST) + br_ref[...]
        gl = jnp.where(lane < E_GROUPS, logits, neg)
        gmax = jnp.max(gl, axis=1, keepdims=True)
        gidx = jnp.min(jnp.where(gl == gmax, lane, LANE), axis=1, keepdims=True)
        p_group = 1.0 / jnp.sum(jnp.exp(gl - gmax), axis=1, keepdims=True)
        in_grp = (lane >= R_E0) & (lane < R_E0 + N_EXPERTS) & (((lane - R_E0) >> 2) == gidx)
        ev = jnp.where(in_grp, logits, neg)
        v1 = jnp.max(ev, axis=1, keepdims=True)
        i1 = jnp.min(jnp.where(ev == v1, lane, LANE), axis=1, keepdims=True)
        ev2 = jnp.where(lane == i1, neg, ev)
        v2 = jnp.max(ev2, axis=1, keepdims=True)
        i2 = jnp.min(jnp.where(ev2 == v2, lane, LANE), axis=1, keepdims=True)
        e2 = jnp.exp(v2 - v1)
        w1 = 1.0 / (1.0 + e2)
        comb_ref[...] = jnp.where(lane == i1, w1 * p_group, jnp.where(lane == i2, e2 * w1 * p_group, 0.0))
        acc_ref[...] = jnp.zeros_like(acc_ref)

    xn = xn_ref[...]
    hid = _silu(_dot(xn, wg_ref[...])) * _dot(xn, wu_ref[...])
    ce = jnp.sum(jnp.where(lane == R_E0 + e, comb_ref[...], 0.0), axis=1, keepdims=True)
    acc_ref[...] += ce * _dot(hid.astype(BF16), wd_ref[...])

    @pl.when(e == N_EXPERTS - 1)
    def _():
        o_ref[...] = h_ref[...] + acc_ref[...]


def _moe(h, nw, wr, br, wg, wu, wd, layer):
    n = h.shape[0]
    tm = _row_tile(n, 1024)
    return pl.pallas_call(
        _moe_kernel,
        grid=(n // tm, N_EXPERTS),
        in_specs=[
            pl.BlockSpec((tm, D_MODEL), lambda i, e: (i, 0)),
            pl.BlockSpec((1, D_MODEL), lambda i, e: (0, 0)),
            pl.BlockSpec((D_MODEL, LANE), lambda i, e: (0, 0)),
            pl.BlockSpec((1, LANE), lambda i, e: (0, 0)),
            pl.BlockSpec((None, None, D_MODEL, E_FF), lambda i, e: (layer, e, 0, 0)),
            pl.BlockSpec((None, None, D_MODEL, E_FF), lambda i, e: (layer, e, 0, 0)),
            pl.BlockSpec((None, None, E_FF, D_MODEL), lambda i, e: (layer, e, 0, 0)),
        ],
        out_specs=pl.BlockSpec((tm, D_MODEL), lambda i, e: (i, 0)),
        out_shape=jax.ShapeDtypeStruct((n, D_MODEL), F32),
        scratch_shapes=[
            pltpu.VMEM((tm, D_MODEL), BF16),
            pltpu.VMEM((tm, LANE), F32),
            pltpu.VMEM((tm, D_MODEL), F32),
        ],
        compiler_params=_cparams(("parallel", "arbitrary")),
    )(h, nw, wr, br, wg, wu, wd)


def _ple_kernel(h_ref, p_ref, nw_ref, wg_ref, wp_ref, nf_ref, o_ref, y_ref):
    x = h_ref[...]
    xn = (x * _rms_scale(x) * nw_ref[...]).astype(BF16)
    out = x + jax.nn.sigmoid(_dot(xn, wg_ref[...])) * _dot(p_ref[...].astype(BF16), wp_ref[...])
    o_ref[...] = out
    y_ref[...] = out * _rms_scale(out) * nf_ref[...]


def _ple(h, p, nw, wg, wp, nf, layer):
    n = h.shape[0]
    tm = _row_tile(n, 512)
    row = lambda w: pl.BlockSpec((tm, w), lambda i: (i, 0))
    full = lambda shape: pl.BlockSpec(shape, lambda i: (0,) * len(shape))
    lfull = lambda shape: pl.BlockSpec((None,) + shape, lambda i: (layer,) + (0,) * len(shape))
    return pl.pallas_call(
        _ple_kernel,
        grid=(n // tm,),
        in_specs=[row(D_MODEL), pl.BlockSpec((None, tm, PLE_DIM), lambda i: (layer, i, 0)),
                  full((1, D_MODEL)), lfull((D_MODEL, D_MODEL)),
                  lfull((PLE_DIM, D_MODEL)), full((1, D_MODEL))],
        out_specs=[row(D_MODEL), row(D_MODEL)],
        out_shape=[jax.ShapeDtypeStruct((n, D_MODEL), F32), jax.ShapeDtypeStruct((n, D_MODEL), F32)],
        compiler_params=_cparams(("parallel",)),
    )(h, p, nw, wg, wp, nf)


def _prep_weights(w):
    win = w['w_in'].astype(BF16)
    pad = jnp.zeros((DEPTH, D_MODEL, Z_COLS - IN_COLS), BF16)
    win_r = jnp.concatenate([
        win[:, :, _O_XBC:_O_XBC + M2_CONV_DIM], win[:, :, _O_Z:_O_Z + M2_INNER],
        win[:, :, _O_GATE:], win[:, :, :_O_Z], win[:, :, _O_DT:_O_DT + M2_HEADS], pad], axis=2)
    lane_pad = lambda a: jnp.pad(a, ((0, 0), (0, LANE - a.shape[1])))
    s5 = [_s5_params(w['s5_a_re'][i], w['s5_a_im'][i], w['s5_b_re'][i], w['s5_b_im'][i],
                     w['s5_c_re'][i], w['s5_c_im'][i], w['s5_log_dt'][i]) for i in range(DEPTH)]
    wr = jnp.concatenate([w['w_rg'], w['w_re'],
                          jnp.zeros((DEPTH, D_MODEL, LANE - E_GROUPS - N_EXPERTS), F32)], axis=2)
    br = jnp.concatenate([w['b_rg'], w['b_re'],
                          jnp.zeros((DEPTH, LANE - E_GROUPS - N_EXPERTS), F32)], axis=1)
    return dict(
        win=win_r,
        s5=s5,
        s5_d=w['s5_d'].reshape(DEPTH, 1, S5_WIDTH),
        wglu=w['s5_w_glu'].astype(BF16),
        bglu=w['s5_b_glu'].reshape(DEPTH, 1, 2 * S5_WIDTH),
        dtb=lane_pad(w['m2_dt_bias']).reshape(DEPTH, 1, LANE),
        alog=lane_pad(w['m2_a_log']).reshape(DEPTH, 1, LANE),
        dsk=jnp.repeat(w['m2_d'], M2_HEADDIM, axis=1).reshape(DEPTH, 1, M2_INNER),
        wa=w['w_br_hg'].astype(BF16), wb=w['w_br_s5'].astype(BF16), wc=w['w_br_m2'].astype(BF16),
        wo=w['w_out'].astype(BF16),
        wr=wr, br=br.reshape(DEPTH, 1, LANE),
        weg=w['w_e_gate'].astype(BF16), weu=w['w_e_up'].astype(BF16), wed=w['w_e_down'].astype(BF16),
        wpg=w['w_ple_gate'].astype(BF16), wpp=w['w_ple_proj'].astype(BF16),
    )


def _trunk(x, p, states, w, pw, t_valid, cfg):
    bsz, t, _ = x.shape
    n = bsz * t
    st_hg, st_re, st_im, st_ssm, st_conv = states
    per_layer = st_hg.shape[0] == DEPTH
    st_re = st_re.reshape(st_re.shape[0], bsz, S5_HW)
    st_im = st_im.reshape(st_im.shape[0], bsz, S5_HW)
    st_ssm = st_ssm.reshape(st_ssm.shape[0], bsz, M2_PAIRS, LANE, M2_STATE)
    h = x.reshape(n, D_MODEL)
    new = []
    y = None
    s_hg = None
    ssd_prev = None
    for i in range(DEPTH):
        sl = i if per_layer else 0
        z = _norm_matmul(h, w['norm_mix'][i].reshape(1, D_MODEL), pw['win'], i)
        z3 = z.reshape(bsz, t, Z_COLS)
        oa, s_hg = _hgrn(z3, w['hg_lb_logits'], w['hg_gnorm'][i].reshape(1, HG_DIM), st_hg, s_hg, i, sl,
                         cfg['hg_chunk'], cfg['hg_bt'], t_valid)
        u_tm = jnp.transpose(z3[:, :, COL_U:COL_U + S5_WIDTH], (1, 0, 2))
        bmat, cmat, a_vec = pw['s5'][i]
        ob_tm, s_re, s_im = _s5(u_tm, bmat, cmat, a_vec, pw['s5_d'][i], pw['wglu'][i], pw['bglu'][i],
                                st_re, st_im, sl, cfg['s5_tc'], cfg['s5_bt'], t_valid)
        ob = jnp.transpose(ob_tm, (1, 0, 2))
        oc, s_ssm, conv_new = _ssd(z3, w['m2_conv_w'][i], w['m2_conv_b'][i].reshape(1, M2_CONV_DIM),
                                   pw['dtb'][i], pw['alog'][i], pw['dsk'][i],
                                   w['m2_norm'][i].reshape(1, M2_INNER), st_conv, st_ssm, ssd_prev, i, sl,
                                   cfg['m2_chunk'], t_valid)
        ssd_prev = (s_ssm, conv_new)
        h = _merge(oa.reshape(n, HG_W), ob.reshape(n, S5_WIDTH), oc.reshape(n, M2_INNER), z, h,
                   pw['wa'], pw['wb'], pw['wc'], pw['wo'], i)
        h = _moe(h, w['norm_ffn'][i].reshape(1, D_MODEL), pw['wr'][i], pw['br'][i],
                 pw['weg'], pw['weu'], pw['wed'], i)
        h, y = _ple(h, p.reshape(DEPTH, n, PLE_DIM), w['norm_ple'][i].reshape(1, D_MODEL),
                    pw['wpg'], pw['wpp'], w['norm_final'].reshape(1, D_MODEL), i)
        new.append((s_re.reshape(bsz, S5_GROUPS, S5_STATE), s_im.reshape(bsz, S5_GROUPS, S5_STATE)))
    s5_re, s5_im = (jnp.stack([nl[j] for nl in new]) for j in range(2))
    stacked = (s_hg, s5_re, s5_im, s_ssm.reshape(DEPTH, bsz, M2_HEADS, M2_HEADDIM, M2_STATE), conv_new)
    return y.reshape(bsz, t, D_MODEL), stacked


def _zero_states(bsz):
    return (jnp.zeros((1, bsz, HG_HEADS, HG_DIM, HG_DIM), F32),
            jnp.zeros((1, bsz, S5_GROUPS, S5_STATE), F32),
            jnp.zeros((1, bsz, S5_GROUPS, S5_STATE), F32),
            jnp.zeros((1, bsz, M2_HEADS, M2_HEADDIM, M2_STATE), F32),
            jnp.zeros((1, bsz, M2_CONV - 1, M2_CONV_DIM), F32))


def _pad_time(a, axis, t_pad):
    t = a.shape[axis]
    if t == t_pad:
        return a
    widths = [(0, 0)] * a.ndim
    widths[axis] = (0, t_pad - t)
    return jnp.pad(a, widths)


def kernel(x_prompt, x_sample, state_hgrn, state_s5_re, state_s5_im, state_ssm, state_conv,
           p_prompt, p_sample,
           norm_mix, w_in, hg_lb_logits, hg_gnorm, w_br_hg,
           s5_a_re, s5_a_im, s5_b_re, s5_b_im, s5_c_re, s5_c_im, s5_d, s5_log_dt, s5_w_glu, s5_b_glu, w_br_s5,
           m2_conv_w, m2_conv_b, m2_dt_bias, m2_a_log, m2_d, m2_norm, w_br_m2,
           w_out,
           norm_ffn, w_rg, b_rg, w_re, b_re, w_e_gate, w_e_up, w_e_down,
           norm_ple, w_ple_gate, w_ple_proj,
           norm_final):
    w = dict(norm_mix=norm_mix, w_in=w_in, hg_lb_logits=hg_lb_logits, hg_gnorm=hg_gnorm, w_br_hg=w_br_hg,
             s5_a_re=s5_a_re, s5_a_im=s5_a_im, s5_b_re=s5_b_re, s5_b_im=s5_b_im, s5_c_re=s5_c_re,
             s5_c_im=s5_c_im, s5_d=s5_d, s5_log_dt=s5_log_dt, s5_w_glu=s5_w_glu, s5_b_glu=s5_b_glu,
             w_br_s5=w_br_s5, m2_conv_w=m2_conv_w, m2_conv_b=m2_conv_b, m2_dt_bias=m2_dt_bias,
             m2_a_log=m2_a_log, m2_d=m2_d, m2_norm=m2_norm, w_br_m2=w_br_m2, w_out=w_out,
             norm_ffn=norm_ffn, w_rg=w_rg, b_rg=b_rg, w_re=w_re, b_re=b_re, w_e_gate=w_e_gate,
             w_e_up=w_e_up, w_e_down=w_e_down, norm_ple=norm_ple, w_ple_gate=w_ple_gate,
             w_ple_proj=w_ple_proj, norm_final=norm_final)
    pw = _prep_weights(w)

    bp, tp, _ = x_prompt.shape
    cfg_p = dict(hg_chunk=min(64, tp), hg_bt=1, s5_tc=min(64, tp), s5_bt=SUBLANE, m2_chunk=min(128, tp))
    y_p, st_p = _trunk(x_prompt, p_prompt, _zero_states(bp), w, pw, tp, cfg_p)

    bs, ts, _ = x_sample.shape
    ts_pad = -(-ts // SUBLANE) * SUBLANE
    cfg_s = dict(hg_chunk=ts_pad, hg_bt=min(4, bs), s5_tc=ts_pad, s5_bt=min(32, bs), m2_chunk=ts_pad)
    y_s, st_s = _trunk(_pad_time(x_sample, 1, ts_pad), _pad_time(p_sample, 2, ts_pad),
                       (state_hgrn, state_s5_re, state_s5_im, state_ssm, state_conv), w, pw, ts, cfg_s)
    return (y_p, y_s[:, :ts]) + st_p + st_s
```

```python
import functools

import jax
import jax.numpy as jnp
from jax import lax
from jax.experimental import pallas as pl
from jax.experimental.pallas import tpu as pltpu

F32 = jnp.float32
BF16 = jnp.bfloat16

D_MODEL = 1024
DEPTH = 2
PLE_DIM = 256
NORM_EPS = 1e-6

HG_HEADS = 4
HG_DIM = 128
HG_W = HG_HEADS * HG_DIM

S5_WIDTH = 512
S5_GROUP = 16
S5_GROUPS = S5_WIDTH // S5_GROUP
S5_STATE = 64
S5_HW = S5_GROUPS * S5_STATE
S5_SLABS = 2
S5_SLAB_U = S5_WIDTH // S5_SLABS
S5_SLAB_H = S5_HW // S5_SLABS

M2_INNER = 1024
M2_HEADDIM = 64
M2_HEADS = M2_INNER // M2_HEADDIM
M2_NGROUPS = 4
M2_STATE = 128
M2_CONV = 4
M2_BC = M2_NGROUPS * M2_STATE
M2_CONV_DIM = M2_INNER + 2 * M2_BC
M2_PAIRS = M2_HEADS // 2
M2_NORM_W = M2_INNER // M2_NGROUPS

N_BRANCH = 3
E_GROUPS = 4
E_PER_GROUP = 4
N_EXPERTS = E_GROUPS * E_PER_GROUP
E_FF = 256

LANE = 128
SUBLANE = 8

COL_XBC = 0
COL_Z = COL_XBC + M2_CONV_DIM
COL_GATE = COL_Z + M2_INNER
COL_Q = COL_GATE + N_BRANCH * D_MODEL
COL_F = COL_Q + HG_W
COL_I = COL_F + HG_W
COL_G = COL_I + HG_W
COL_U = COL_G + HG_W
COL_DT = COL_U + S5_WIDTH
Z_TN = 1280
Z_COLS = 7 * Z_TN
assert COL_DT + LANE <= Z_COLS and COL_DT % LANE == 0

_O_Q, _O_F, _O_I, _O_G, _O_U = 0, 512, 1024, 1536, 2048
_O_Z = 2560
_O_XBC = _O_Z + M2_INNER
_O_DT = _O_XBC + M2_CONV_DIM
_O_GATE = _O_DT + M2_HEADS
IN_COLS = _O_GATE + N_BRANCH * D_MODEL

VMEM_LIMIT = 56 * 1024 * 1024


def _cparams(sem):
    return pltpu.CompilerParams(dimension_semantics=sem, vmem_limit_bytes=VMEM_LIMIT)


def _rms_scale(x):
    return lax.rsqrt(jnp.mean(x * x, axis=-1, keepdims=True) + NORM_EPS)


def _silu(x):
    return x * jax.nn.sigmoid(x)


def _dot(a, b):
    return jnp.dot(a, b, preferred_element_type=F32)


def _dot_nt(a, b):
    return lax.dot_general(a, b, (((1,), (1,)), ((), ())), preferred_element_type=F32)


def _dot_tn(a, b):
    return lax.dot_general(a, b, (((0,), (0,)), ((), ())), preferred_element_type=F32)


def _cumsum_rows(x):
    n = x.shape[0]
    row = lax.broadcasted_iota(jnp.int32, x.shape, 0)
    s = 1
    while s < n:
        x = x + jnp.where(row >= s, pltpu.roll(x, s, 0), 0.0)
        s *= 2
    return x


def _row_tile(n, pref):
    t = min(pref, n)
    assert n % t == 0
    return t


def _norm_matmul_kernel(x_ref, nw_ref, w_ref, o_ref, xn_ref):
    @pl.when(pl.program_id(1) == 0)
    def _():
        x = x_ref[...]
        xn_ref[...] = (x * _rms_scale(x) * nw_ref[...]).astype(BF16)

    o_ref[...] = _dot(xn_ref[...], w_ref[...])


def _norm_matmul(h, nw, w, layer):
    n = h.shape[0]
    cols = w.shape[2]
    tm = _row_tile(n, 1024)
    tn = Z_TN
    return pl.pallas_call(
        _norm_matmul_kernel,
        grid=(n // tm, cols // tn),
        in_specs=[
            pl.BlockSpec((tm, D_MODEL), lambda i, j: (i, 0)),
            pl.BlockSpec((1, D_MODEL), lambda i, j: (0, 0)),
            pl.BlockSpec((None, D_MODEL, tn), lambda i, j: (layer, 0, j)),
        ],
        out_specs=pl.BlockSpec((tm, tn), lambda i, j: (i, j)),
        out_shape=jax.ShapeDtypeStruct((n, cols), F32),
        scratch_shapes=[pltpu.VMEM((tm, D_MODEL), BF16)],
        compiler_params=_cparams(("parallel", "arbitrary")),
    )(h, nw, w)


def _hgrn_chunk(q, k, v, logf, st, lev, dcode):
    C = q.shape[0]
    b = _cumsum_rows(logf)
    b_end = b[C - 1:C, :]
    o = _dot((q * jnp.exp(b)).astype(BF16), st.astype(BF16))

    sc = jnp.where(dcode == 0, jnp.sum(q * k, axis=1, keepdims=True), 0.0)
    for j in range(1, min(SUBLANE, C)):
        kr = pltpu.roll(k, j, 0)
        br = pltpu.roll(b, j, 0)
        tj = jnp.sum(q * kr * jnp.exp(jnp.minimum(b - br, 0.0)), axis=1, keepdims=True)
        sc = jnp.where(dcode == j, tj, sc)

    m = C // 2
    while m >= SUBLANE:
        parts = []
        for p0 in range(0, C, 2 * m):
            parts.append(jnp.broadcast_to(b[p0 + m - 1:p0 + m, :], (2 * m, HG_DIM)))
        r = parts[0] if len(parts) == 1 else jnp.concatenate(parts, axis=0)
        qe = q * jnp.exp(jnp.minimum(b - r, 0.0))
        ke = k * jnp.exp(jnp.minimum(r - b, 0.0))
        pm = _dot_nt(qe.astype(BF16), ke.astype(BF16))
        sc = jnp.where((lev >> (m.bit_length() - 1)) == 1, pm, sc)
        m //= 2

    v16 = v.astype(BF16)
    o = o + _dot(sc.astype(BF16), v16)
    ke = k * jnp.exp(b_end - b)
    decay = jnp.transpose(jnp.broadcast_to(jnp.exp(b_end), (SUBLANE, HG_DIM)))[:, 0:1]
    st_new = st * decay + _dot_tn(ke.astype(BF16), v16)
    return o, st_new


def _hgrn_kernel(zq_ref, zf_ref, zi_ref, zg_ref, lbl_ref, gw_ref, s0_ref, *rest,
                 layer, chunk, bt, t_valid, t_total):
    o_ref, s_ref, st_ref = rest[-3:]
    c = pl.program_id(1)
    C = chunk

    @pl.when(c == 0)
    def _():
        st_ref[...] = s0_ref[...]

    lg = lbl_ref[...]
    e = jnp.exp(lg - jnp.max(lg, axis=0, keepdims=True))
    prob = e / jnp.sum(e, axis=0, keepdims=True)
    lb_all = jnp.zeros((1, HG_W), F32)
    for j in range(1, layer + 1):
        lb_all = lb_all + prob[j:j + 1]

    rr = lax.broadcasted_iota(jnp.int32, (C, C), 0)
    cc = lax.broadcasted_iota(jnp.int32, (C, C), 1)
    xr = rr ^ cc
    lev = jnp.where(rr > cc, xr, -1)
    dcode = jnp.where(xr < SUBLANE, rr - cc, -1)
    if t_valid < t_total:
        valid = (c * C + lax.broadcasted_iota(jnp.int32, (C, HG_DIM), 0)) < t_valid
    gw = gw_ref[...]

    for bi in range(bt):
        for h in range(HG_HEADS):
            sl = slice(h * HG_DIM, (h + 1) * HG_DIM)
            lb = lb_all[:, sl]
            zf = zf_ref[bi, :, sl]
            q = _silu(zq_ref[bi, :, sl])
            logf = jnp.log(lb + (1.0 - lb) * jax.nn.sigmoid(zf))
            k = (1.0 - lb) * jax.nn.sigmoid(-zf)
            if t_valid < t_total:
                logf = jnp.where(valid, logf, 0.0)
                k = jnp.where(valid, k, 0.0)
            o, st_new = _hgrn_chunk(q, k, zi_ref[bi, :, sl], logf, st_ref[bi, h], lev, dcode)
            o_ref[bi, :, sl] = o * _rms_scale(o) * gw * _silu(zg_ref[bi, :, sl])
            st_ref[bi, h] = st_new

    @pl.when(c == pl.num_programs(1) - 1)
    def _():
        s_ref[...] = st_ref[...]


def _hgrn(z3, lb_logits, gw, s0, prev, layer, state_layer, chunk, bt, t_valid):
    bsz, t, _ = z3.shape
    nc = t // chunk
    kern = functools.partial(_hgrn_kernel, layer=layer, chunk=chunk, bt=bt, t_valid=t_valid, t_total=t)

    def zspec(col):
        return pl.BlockSpec((bt, chunk, HG_W), lambda b, c: (b, c, col // HG_W))

    in_specs = [
        zspec(COL_Q), zspec(COL_F), zspec(COL_I), zspec(COL_G),
        pl.BlockSpec((DEPTH, HG_W), lambda b, c: (0, 0)),
        pl.BlockSpec((1, HG_DIM), lambda b, c: (0, 0)),
        pl.BlockSpec((None, bt, HG_HEADS, HG_DIM, HG_DIM), lambda b, c: (state_layer, b, 0, 0, 0)),
    ]
    args = [z3, z3, z3, z3, lb_logits, gw, s0]
    aliases = {}
    if prev is not None:
        in_specs.append(pl.BlockSpec(memory_space=pl.ANY))
        args.append(prev)
        aliases = {len(args) - 1: 1}
    return pl.pallas_call(
        kern,
        grid=(bsz // bt, nc),
        in_specs=in_specs,
        out_specs=[
            pl.BlockSpec((bt, chunk, HG_W), lambda b, c: (b, c, 0)),
            pl.BlockSpec((None, bt, HG_HEADS, HG_DIM, HG_DIM), lambda b, c: (layer, b, 0, 0, 0)),
        ],
        out_shape=[
            jax.ShapeDtypeStruct((bsz, t, HG_W), F32),
            jax.ShapeDtypeStruct((DEPTH, bsz, HG_HEADS, HG_DIM, HG_DIM), F32),
        ],
        scratch_shapes=[pltpu.VMEM((bt, HG_HEADS, HG_DIM, HG_DIM), F32)],
        input_output_aliases=aliases,
        compiler_params=_cparams(("parallel", "arbitrary")),
    )(*args)


def _s5_kernel(u_ref, bmat_ref, cmat_ref, a_ref, d_ref, wglu_ref, bglu_ref, hre0_ref, him0_ref,
               o_ref, hre_ref, him_ref, hs_ref, hst_ref, *, tc, bt, steps, lane_chunk):
    c = pl.program_id(1)

    def re_off(s):
        return 2 * s * S5_SLAB_H

    def im_off(s):
        return (2 * s + 1) * S5_SLAB_H

    @pl.when(c == 0)
    def _():
        for s in range(S5_SLABS):
            src = slice(s * S5_SLAB_H, (s + 1) * S5_SLAB_H)
            hst_ref[:, re_off(s):re_off(s) + S5_SLAB_H] = hre0_ref[:, src]
            hst_ref[:, im_off(s):im_off(s) + S5_SLAB_H] = him0_ref[:, src]

    u = u_ref[...].reshape(tc * bt, S5_WIDTH)
    u16 = u.astype(BF16)
    for s in range(S5_SLABS):
        hs_ref[:, re_off(s):re_off(s + 1)] = _dot(u16[:, s * S5_SLAB_U:(s + 1) * S5_SLAB_U], bmat_ref[s])

    for s in range(S5_SLABS):
        for l0 in range(0, S5_SLAB_H, lane_chunk):
            re_sl = slice(re_off(s) + l0, re_off(s) + l0 + lane_chunk)
            im_sl = slice(im_off(s) + l0, im_off(s) + l0 + lane_chunk)
            are = a_ref[:, re_sl]
            aim = a_ref[:, im_sl]

            def step(t, carry, re_sl=re_sl, im_sl=im_sl, are=are, aim=aim):
                hre, him = carry
                r0 = pl.multiple_of(t * bt, bt)
                nre = are * hre - aim * him + hs_ref[pl.ds(r0, bt), re_sl]
                nim = are * him + aim * hre + hs_ref[pl.ds(r0, bt), im_sl]
                hs_ref[pl.ds(r0, bt), re_sl] = nre
                hs_ref[pl.ds(r0, bt), im_sl] = nim
                return nre, nim

            hre, him = lax.fori_loop(0, steps, step, (hst_ref[:, re_sl], hst_ref[:, im_sl]))
            hst_ref[:, re_sl] = hre
            hst_ref[:, im_sl] = him

    ys = [_dot(hs_ref[:, re_off(s):re_off(s + 1)].astype(BF16), cmat_ref[s]) for s in range(S5_SLABS)]
    y = jnp.concatenate(ys, axis=1) + d_ref[...] * u
    g = jax.nn.gelu(y)
    gl = _dot(g.astype(BF16), wglu_ref[...]) + bglu_ref[...]
    o = gl[:, :S5_WIDTH] * jax.nn.sigmoid(gl[:, S5_WIDTH:])
    o_ref[...] = o.reshape(tc, bt, S5_WIDTH)

    @pl.when(c == pl.num_programs(1) - 1)
    def _():
        for s in range(S5_SLABS):
            dst = slice(s * S5_SLAB_H, (s + 1) * S5_SLAB_H)
            hre_ref[:, dst] = hst_ref[:, re_off(s):re_off(s) + S5_SLAB_H]
            him_ref[:, dst] = hst_ref[:, im_off(s):im_off(s) + S5_SLAB_H]


def _s5(u_tm, bmat, cmat, a_vec, d_vec, wglu, bglu, hre0, him0, state_layer, tc, bt, t_valid):
    t, bsz, _ = u_tm.shape
    nc = t // tc
    if t_valid < t:
        assert nc == 1
    steps = tc if t_valid == t else t_valid
    lane_chunk = min(S5_SLAB_H, max(LANE, (8 * SUBLANE * LANE) // bt))
    kern = functools.partial(_s5_kernel, tc=tc, bt=bt, steps=steps, lane_chunk=lane_chunk)
    full = lambda shape: pl.BlockSpec(shape, lambda b, c: (0,) * len(shape))
    return pl.pallas_call(
        kern,
        grid=(bsz // bt, nc),
        in_specs=[
            pl.BlockSpec((tc, bt, S5_WIDTH), lambda b, c: (c, b, 0)),
            full((S5_SLABS, S5_SLAB_U, 2 * S5_SLAB_H)),
            full((S5_SLABS, 2 * S5_SLAB_H, S5_SLAB_U)),
            full((1, 2 * S5_HW)),
            full((1, S5_WIDTH)),
            full((S5_WIDTH, 2 * S5_WIDTH)),
            full((1, 2 * S5_WIDTH)),
            pl.BlockSpec((None, bt, S5_HW), lambda b, c: (state_layer, b, 0)),
            pl.BlockSpec((None, bt, S5_HW), lambda b, c: (state_layer, b, 0)),
        ],
        out_specs=[
            pl.BlockSpec((tc, bt, S5_WIDTH), lambda b, c: (c, b, 0)),
            pl.BlockSpec((bt, S5_HW), lambda b, c: (b, 0)),
            pl.BlockSpec((bt, S5_HW), lambda b, c: (b, 0)),
        ],
        out_shape=[
            jax.ShapeDtypeStruct((t, bsz, S5_WIDTH), F32),
            jax.ShapeDtypeStruct((bsz, S5_HW), F32),
            jax.ShapeDtypeStruct((bsz, S5_HW), F32),
        ],
        scratch_shapes=[
            pltpu.VMEM((tc * bt, 2 * S5_HW), F32),
            pltpu.VMEM((bt, 2 * S5_HW), F32),
        ],
        compiler_params=_cparams(("parallel", "arbitrary")),
    )(u_tm, bmat, cmat, a_vec, d_vec, wglu, bglu, hre0, him0)


def _s5_params(a_re, a_im, b_re, b_im, c_re, c_im, log_dt):
    g = S5_GROUPS
    dt = jnp.exp(log_dt)[:, None]
    er = jnp.exp(a_re * dt)
    abr = er * jnp.cos(a_im * dt)
    abi = er * jnp.sin(a_im * dt)
    den = a_re * a_re + a_im * a_im
    xr = abr - 1.0
    cr = (xr * a_re + abi * a_im) / den
    ci = (abi * a_re - xr * a_im) / den
    bbr = cr[..., None] * b_re - ci[..., None] * b_im
    bbi = cr[..., None] * b_im + ci[..., None] * b_re
    sg = g // S5_SLABS
    eye = jnp.eye(sg, dtype=F32)
    slab = lambda x: x.reshape((S5_SLABS, sg) + x.shape[1:])
    bd_in = lambda x: jnp.einsum('sgnc,gh->sgchn', slab(x), eye).reshape(S5_SLABS, S5_SLAB_U, S5_SLAB_H)
    bd_out = lambda x: jnp.einsum('sgcn,gh->sgnhc', slab(x), eye).reshape(S5_SLABS, S5_SLAB_H, S5_SLAB_U)
    bmat = jnp.concatenate([bd_in(bbr), bd_in(bbi)], axis=2).astype(BF16)
    cmat = jnp.concatenate([bd_out(c_re), -bd_out(c_im)], axis=1).astype(BF16)
    lanes = lambda x: x.reshape(S5_SLABS, 1, S5_SLAB_H)
    a_vec = jnp.concatenate([lanes(abr), lanes(abi)], axis=2).reshape(1, 2 * S5_HW)
    return bmat, cmat, a_vec


def _ssd_kernel(xbc_ref, zz_ref, zdt_ref, cw_ref, cb_ref, dtb_ref, alog_ref, dsk_ref, nw_ref,
                conv0_ref, s0_ref, *rest, chunk, bt, t_valid, t_total):
    y_ref, s_ref, cn_ref, st_ref, tail_ref = rest[-5:]
    c = pl.program_id(1)
    nc = t_total // chunk
    L = chunk
    TAIL = SUBLANE
    last_valid = t_valid - (nc - 1) * L

    @pl.when(c == 0)
    def _():
        st_ref[...] = s0_ref[...]
        tail_ref[...] = jnp.zeros((bt, TAIL, M2_CONV_DIM), F32)
        tail_ref[:, TAIL - (M2_CONV - 1):TAIL, :] = conv0_ref[...]

    causal = lax.broadcasted_iota(jnp.int32, (L, L), 0) >= lax.broadcasted_iota(jnp.int32, (L, L), 1)
    lo_lane = lax.broadcasted_iota(jnp.int32, (L, LANE), 1) < M2_HEADDIM
    lo_row = lax.broadcasted_iota(jnp.int32, (LANE, LANE), 0) < M2_HEADDIM
    neg_a = -jnp.exp(alog_ref[...])
    nw = nw_ref[...]
    if t_valid < t_total:
        valid = (c * L + lax.broadcasted_iota(jnp.int32, (L, LANE), 0)) < t_valid

    def pair_cols(a, h0):
        return jnp.where(lo_lane, a[:, h0:h0 + 1], a[:, h0 + 1:h0 + 2])

    for bi in range(bt):
        xbc = xbc_ref[bi]
        ext = jnp.concatenate([tail_ref[bi], xbc], axis=0)
        conv = cb_ref[...]
        for j in range(M2_CONV):
            o0 = TAIL - (M2_CONV - 1) + j
            conv = conv + cw_ref[j:j + 1, :] * ext[o0:o0 + L, :]
        tail_ref[bi] = xbc[L - TAIL:L, :]

        @pl.when(c == nc - 1)
        def _(ext=ext, bi=bi):
            cn_ref[bi] = ext[TAIL + last_valid - (M2_CONV - 1):TAIL + last_valid, :]

        act = _silu(conv)
        xs = act[:, :M2_INNER]
        bm = act[:, M2_INNER:M2_INNER + M2_BC].astype(BF16)
        cm = act[:, M2_INNER + M2_BC:].astype(BF16)

        dtr = zdt_ref[bi] + dtb_ref[...]
        dt = jnp.maximum(dtr, 0.0) + jnp.log1p(jnp.exp(-jnp.abs(dtr)))
        if t_valid < t_total:
            dt = jnp.where(valid, dt, 0.0)
        acs = _cumsum_rows(dt * neg_a)
        acs_t = acs.T
        acs_end = acs[L - 1:L, :]
        e_acs = jnp.exp(acs)
        e_end = jnp.exp(acs_end - acs)
        e_tot = jnp.exp(acs_end)

        ys = []
        for g in range(M2_NGROUPS):
            bg = bm[:, g * M2_STATE:(g + 1) * M2_STATE]
            cg = cm[:, g * M2_STATE:(g + 1) * M2_STATE]
            cb = _dot_nt(cg, bg)
            for pr in range(2):
                p = g * 2 + pr
                h0 = 2 * p
                xp = xs[:, p * LANE:(p + 1) * LANE]
                xdt = xp * pair_cols(dt, h0)
                ydiag = None
                for hh in range(2):
                    h = h0 + hh
                    lm = jnp.where(causal, jnp.exp(jnp.minimum(acs[:, h:h + 1] - acs_t[h:h + 1, :], 0.0)), 0.0)
                    mk = lo_lane if hh == 0 else jnp.logical_not(lo_lane)
                    part = _dot((cb * lm).astype(BF16), jnp.where(mk, xdt, 0.0).astype(BF16))
                    ydiag = part if ydiag is None else ydiag + part
                sp = st_ref[bi, p]
                yoff = pair_cols(e_acs, h0) * _dot_nt(cg, sp.astype(BF16))
                ys.append(ydiag + yoff + dsk_ref[:, p * LANE:(p + 1) * LANE] * xp)
                xe = (xdt * pair_cols(e_end, h0)).astype(BF16)
                scale = jnp.where(lo_row, e_tot[:, h0:h0 + 1], e_tot[:, h0 + 1:h0 + 2])
                st_ref[bi, p] = scale * sp + _dot_tn(xe, bg)

        y = jnp.concatenate(ys, axis=1) * _silu(zz_ref[bi])
        outs = []
        for g in range(M2_NGROUPS):
            sl = slice(g * M2_NORM_W, (g + 1) * M2_NORM_W)
            yg = y[:, sl]
            outs.append(yg * _rms_scale(yg) * nw[:, sl])
        y_ref[bi] = jnp.concatenate(outs, axis=1)

    @pl.when(c == nc - 1)
    def _():
        s_ref[...] = st_ref[...]


def _ssd(z3, cw, cb, dtb, alog, dsk, nw, conv0, s0, prev, layer, state_layer, chunk, bt, t_valid):
    bsz, t, _ = z3.shape
    nc = t // chunk
    assert chunk >= SUBLANE and t_valid - (nc - 1) * chunk >= 1
    kern = functools.partial(_ssd_kernel, chunk=chunk, bt=bt, t_valid=t_valid, t_total=t)
    full = lambda shape: pl.BlockSpec(shape, lambda b, c: (0,) * len(shape))
    extra_specs, extra_args, aliases = [], [], {}
    if prev is not None:
        extra_specs = [pl.BlockSpec(memory_space=pl.ANY)] * 2
        extra_args = list(prev)
        aliases = {11: 1, 12: 2}
    return pl.pallas_call(
        kern,
        grid=(bsz // bt, nc),
        input_output_aliases=aliases,
        in_specs=[
            pl.BlockSpec((bt, chunk, M2_CONV_DIM), lambda b, c: (b, c, COL_XBC // M2_CONV_DIM)),
            pl.BlockSpec((bt, chunk, M2_INNER), lambda b, c: (b, c, COL_Z // M2_INNER)),
            pl.BlockSpec((bt, chunk, LANE), lambda b, c: (b, c, COL_DT // LANE)),
            full((M2_CONV, M2_CONV_DIM)),
            full((1, M2_CONV_DIM)),
            full((1, LANE)),
            full((1, LANE)),
            full((1, M2_INNER)),
            full((1, M2_INNER)),
            pl.BlockSpec((None, bt, M2_CONV - 1, M2_CONV_DIM), lambda b, c: (state_layer, b, 0, 0)),
            pl.BlockSpec((None, bt, M2_PAIRS, LANE, M2_STATE), lambda b, c: (state_layer, b, 0, 0, 0)),
        ] + extra_specs,
        out_specs=[
            pl.BlockSpec((bt, chunk, M2_INNER), lambda b, c: (b, c, 0)),
            pl.BlockSpec((None, bt, M2_PAIRS, LANE, M2_STATE), lambda b, c: (layer, b, 0, 0, 0)),
            pl.BlockSpec((None, bt, M2_CONV - 1, M2_CONV_DIM), lambda b, c: (layer, b, 0, 0)),
        ],
        out_shape=[
            jax.ShapeDtypeStruct((bsz, t, M2_INNER), F32),
            jax.ShapeDtypeStruct((DEPTH, bsz, M2_PAIRS, LANE, M2_STATE), F32),
            jax.ShapeDtypeStruct((DEPTH, bsz, M2_CONV - 1, M2_CONV_DIM), F32),
        ],
        scratch_shapes=[
            pltpu.VMEM((bt, M2_PAIRS, LANE, M2_STATE), F32),
            pltpu.VMEM((bt, SUBLANE, M2_CONV_DIM), F32),
        ],
        compiler_params=_cparams(("parallel", "arbitrary")),
    )(z3, z3, z3, cw, cb, dtb, alog, dsk, nw, conv0, s0, *extra_args)


def _merge_kernel(oa_ref, ob_ref, oc_ref, ga_ref, gb_ref, gc_ref, h_ref, wa_ref, wb_ref, wc_ref, wo_ref, o_ref):
    m = jax.nn.sigmoid(ga_ref[...]) * _dot(oa_ref[...].astype(BF16), wa_ref[...])
    m = m + jax.nn.sigmoid(gb_ref[...]) * _dot(ob_ref[...].astype(BF16), wb_ref[...])
    m = m + jax.nn.sigmoid(gc_ref[...]) * _dot(oc_ref[...].astype(BF16), wc_ref[...])
    o_ref[...] = h_ref[...] + _dot(m.astype(BF16), wo_ref[...])


def _merge(oa, ob, oc, z, h, wa, wb, wc, wo, layer):
    n = h.shape[0]
    tm = _row_tile(n, 512)
    gblk = COL_GATE // D_MODEL
    row = lambda w, j=0: pl.BlockSpec((tm, w), lambda i: (i, j))
    full = lambda shape: pl.BlockSpec((None,) + shape, lambda i: (layer,) + (0,) * len(shape))
    return pl.pallas_call(
        _merge_kernel,
        grid=(n // tm,),
        in_specs=[
            row(HG_W), row(S5_WIDTH), row(M2_INNER),
            row(D_MODEL, gblk), row(D_MODEL, gblk + 1), row(D_MODEL, gblk + 2),
            row(D_MODEL),
            full((HG_W, D_MODEL)), full((S5_WIDTH, D_MODEL)), full((M2_INNER, D_MODEL)),
            full((D_MODEL, D_MODEL)),
        ],
        out_specs=row(D_MODEL),
        out_shape=jax.ShapeDtypeStruct((n, D_MODEL), F32),
        compiler_params=_cparams(("parallel",)),
    )(oa, ob, oc, z, z, z, h, wa, wb, wc, wo)


R_E0 = E_GROUPS


def _moe_kernel(h_ref, nw_ref, wr_ref, br_ref, wg_ref, wu_ref, wd_ref, o_ref, xn_ref, comb_ref, acc_ref):
    e = pl.program_id(1)
    tm = h_ref.shape[0]
    lane = lax.broadcasted_iota(jnp.int32, (tm, LANE), 1)
    neg = -jnp.inf

    @pl.when(e == 0)
    def _():
        x = h_ref[...]
        xn = x * _rms_scale(x) * nw_ref[...]
        xn_ref[...] = xn.astype(BF16)
        logits = jnp.dot(xn, wr_ref[...], preferred_element_type=F32,
                         precision=lax.Precision.HIGHEST) + br_ref[...]
        gl = jnp.where(lane < E_GROUPS, logits, neg)
        gmax = jnp.max(gl, axis=1, keepdims=True)
        gidx = jnp.min(jnp.where(gl == gmax, lane, LANE), axis=1, keepdims=True)
        p_group = 1.0 / jnp.sum(jnp.exp(gl - gmax), axis=1, keepdims=True)
        in_grp = (lane >= R_E0) & (lane < R_E0 + N_EXPERTS) & (((lane - R_E0) >> 2) == gidx)
        ev = jnp.where(in_grp, logits, neg)
        v1 = jnp.max(ev, axis=1, keepdims=True)
        i1 = jnp.min(jnp.where(ev == v1, lane, LANE), axis=1, keepdims=True)
        ev2 = jnp.where(lane == i1, neg, ev)
        v2 = jnp.max(ev2, axis=1, keepdims=True)
        i2 = jnp.min(jnp.where(ev2 == v2, lane, LANE), axis=1, keepdims=True)
        e2 = jnp.exp(v2 - v1)
        w1 = 1.0 / (1.0 + e2)
        comb_ref[...] = jnp.where(lane == i1, w1 * p_group, jnp.where(lane == i2, e2 * w1 * p_group, 0.0))
        acc_ref[...] = jnp.zeros_like(acc_ref)

    xn = xn_ref[...]
    hid = _silu(_dot(xn, wg_ref[...])) * _dot(xn, wu_ref[...])
    ce = jnp.sum(jnp.where(lane == R_E0 + e, comb_ref[...], 0.0), axis=1, keepdims=True)
    acc_ref[...] += ce * _dot(hid.astype(BF16), wd_ref[...])

    @pl.when(e == N_EXPERTS - 1)
    def _():
        o_ref[...] = h_ref[...] + acc_ref[...]


def _moe(h, nw, wr, br, wg, wu, wd, layer):
    n = h.shape[0]
    tm = _row_tile(n, 1024)
    return pl.pallas_call(
        _moe_kernel,
        grid=(n // tm, N_EXPERTS),
        in_specs=[
            pl.BlockSpec((tm, D_MODEL), lambda i, e: (i, 0)),
            pl.BlockSpec((1, D_MODEL), lambda i, e: (0, 0)),
            pl.BlockSpec((D_MODEL, LANE), lambda i, e: (0, 0)),
            pl.BlockSpec((1, LANE), lambda i, e: (0, 0)),
            pl.BlockSpec((None, None, D_MODEL, E_FF), lambda i, e: (layer, e, 0, 0)),
            pl.BlockSpec((None, None, D_MODEL, E_FF), lambda i, e: (layer, e, 0, 0)),
            pl.BlockSpec((None, None, E_FF, D_MODEL), lambda i, e: (layer, e, 0, 0)),
        ],
        out_specs=pl.BlockSpec((tm, D_MODEL), lambda i, e: (i, 0)),
        out_shape=jax.ShapeDtypeStruct((n, D_MODEL), F32),
        scratch_shapes=[
            pltpu.VMEM((tm, D_MODEL), BF16),
            pltpu.VMEM((tm, LANE), F32),
            pltpu.VMEM((tm, D_MODEL), F32),
        ],
        compiler_params=_cparams(("parallel", "arbitrary")),
    )(h, nw, wr, br, wg, wu, wd)


def _ple_kernel(h_ref, p_ref, nw_ref, wg_ref, wp_ref, nf_ref, o_ref, y_ref):
    x = h_ref[...]
    xn = (x * _rms_scale(x) * nw_ref[...]).astype(BF16)
    out = x + jax.nn.sigmoid(_dot(xn, wg_ref[...])) * _dot(p_ref[...].astype(BF16), wp_ref[...])
    o_ref[...] = out
    y_ref[...] = out * _rms_scale(out) * nf_ref[...]


def _ple(h, p, nw, wg, wp, nf, layer):
    n = h.shape[0]
    tm = _row_tile(n, 512)
    row = lambda w: pl.BlockSpec((tm, w), lambda i: (i, 0))
    full = lambda shape: pl.BlockSpec(shape, lambda i: (0,) * len(shape))
    lfull = lambda shape: pl.BlockSpec((None,) + shape, lambda i: (layer,) + (0,) * len(shape))
    return pl.pallas_call(
        _ple_kernel,
        grid=(n // tm,),
        in_specs=[row(D_MODEL), pl.BlockSpec((None, tm, PLE_DIM), lambda i: (layer, i, 0)),
                  full((1, D_MODEL)), lfull((D_MODEL, D_MODEL)),
                  lfull((PLE_DIM, D_MODEL)), full((1, D_MODEL))],
        out_specs=[row(D_MODEL), row(D_MODEL)],
        out_shape=[jax.ShapeDtypeStruct((n, D_MODEL), F32), jax.ShapeDtypeStruct((n, D_MODEL), F32)],
        compiler_params=_cparams(("parallel",)),
    )(h, p, nw, wg, wp, nf)


def _prep_weights(w):
    win = w['w_in'].astype(BF16)
    pad = jnp.zeros((DEPTH, D_MODEL, Z_COLS - IN_COLS), BF16)
    win_r = jnp.concatenate([
        win[:, :, _O_XBC:_O_XBC + M2_CONV_DIM], win[:, :, _O_Z:_O_Z + M2_INNER],
        win[:, :, _O_GATE:], win[:, :, :_O_Z], win[:, :, _O_DT:_O_DT + M2_HEADS], pad], axis=2)
    lane_pad = lambda a: jnp.pad(a, ((0, 0), (0, LANE - a.shape[1])))
    s5 = [_s5_params(w['s5_a_re'][i], w['s5_a_im'][i], w['s5_b_re'][i], w['s5_b_im'][i],
                     w['s5_c_re'][i], w['s5_c_im'][i], w['s5_log_dt'][i]) for i in range(DEPTH)]
    wr = jnp.concatenate([w['w_rg'], w['w_re'],
                          jnp.zeros((DEPTH, D_MODEL, LANE - E_GROUPS - N_EXPERTS), F32)], axis=2)
    br = jnp.concatenate([w['b_rg'], w['b_re'],
                          jnp.zeros((DEPTH, LANE - E_GROUPS - N_EXPERTS), F32)], axis=1)
    return dict(
        win=win_r,
        s5=s5,
        s5_d=w['s5_d'].reshape(DEPTH, 1, S5_WIDTH),
        wglu=w['s5_w_glu'].astype(BF16),
        bglu=w['s5_b_glu'].reshape(DEPTH, 1, 2 * S5_WIDTH),
        dtb=lane_pad(w['m2_dt_bias']).reshape(DEPTH, 1, LANE),
        alog=lane_pad(w['m2_a_log']).reshape(DEPTH, 1, LANE),
        dsk=jnp.repeat(w['m2_d'], M2_HEADDIM, axis=1).reshape(DEPTH, 1, M2_INNER),
        wa=w['w_br_hg'].astype(BF16), wb=w['w_br_s5'].astype(BF16), wc=w['w_br_m2'].astype(BF16),
        wo=w['w_out'].astype(BF16),
        wr=wr, br=br.reshape(DEPTH, 1, LANE),
        weg=w['w_e_gate'].astype(BF16), weu=w['w_e_up'].astype(BF16), wed=w['w_e_down'].astype(BF16),
        wpg=w['w_ple_gate'].astype(BF16), wpp=w['w_ple_proj'].astype(BF16),
    )


def _trunk(x, p, states, w, pw, t_valid, cfg):
    bsz, t, _ = x.shape
    n = bsz * t
    st_hg, st_re, st_im, st_ssm, st_conv = states
    per_layer = st_hg.shape[0] == DEPTH
    st_re = st_re.reshape(st_re.shape[0], bsz, S5_HW)
    st_im = st_im.reshape(st_im.shape[0], bsz, S5_HW)
    st_ssm = st_ssm.reshape(st_ssm.shape[0], bsz, M2_PAIRS, LANE, M2_STATE)
    h = x.reshape(n, D_MODEL)
    new = []
    y = None
    s_hg = None
    ssd_prev = None
    for i in range(DEPTH):
        sl = i if per_layer else 0
        z = _norm_matmul(h, w['norm_mix'][i].reshape(1, D_MODEL), pw['win'], i)
        z3 = z.reshape(bsz, t, Z_COLS)
        oa, s_hg = _hgrn(z3, w['hg_lb_logits'], w['hg_gnorm'][i].reshape(1, HG_DIM), st_hg, s_hg, i, sl,
                         cfg['hg_chunk'], cfg['hg_bt'], t_valid)
        u_tm = jnp.transpose(z3[:, :, COL_U:COL_U + S5_WIDTH], (1, 0, 2))
        bmat, cmat, a_vec = pw['s5'][i]
        ob_tm, s_re, s_im = _s5(u_tm, bmat, cmat, a_vec, pw['s5_d'][i], pw['wglu'][i], pw['bglu'][i],
                                st_re, st_im, sl, cfg['s5_tc'], cfg['s5_bt'], t_valid)
        ob = jnp.transpose(ob_tm, (1, 0, 2))
        oc, s_ssm, conv_new = _ssd(z3, w['m2_conv_w'][i], w['m2_conv_b'][i].reshape(1, M2_CONV_DIM),
                                   pw['dtb'][i], pw['alog'][i], pw['dsk'][i],
                                   w['m2_norm'][i].reshape(1, M2_INNER), st_conv, st_ssm, ssd_prev, i, sl,
                                   cfg['m2_chunk'], cfg['m2_bt'], t_valid)
        ssd_prev = (s_ssm, conv_new)
        h = _merge(oa.reshape(n, HG_W), ob.reshape(n, S5_WIDTH), oc.reshape(n, M2_INNER), z, h,
                   pw['wa'], pw['wb'], pw['wc'], pw['wo'], i)
        h = _moe(h, w['norm_ffn'][i].reshape(1, D_MODEL), pw['wr'][i], pw['br'][i],
                 pw['weg'], pw['weu'], pw['wed'], i)
        h, y = _ple(h, p.reshape(DEPTH, n, PLE_DIM), w['norm_ple'][i].reshape(1, D_MODEL),
                    pw['wpg'], pw['wpp'], w['norm_final'].reshape(1, D_MODEL), i)
        new.append((s_re.reshape(bsz, S5_GROUPS, S5_STATE), s_im.reshape(bsz, S5_GROUPS, S5_STATE)))
    s5_re, s5_im = (jnp.stack([nl[j] for nl in new]) for j in range(2))
    stacked = (s_hg, s5_re, s5_im, s_ssm.reshape(DEPTH, bsz, M2_HEADS, M2_HEADDIM, M2_STATE), conv_new)
    return y.reshape(bsz, t, D_MODEL), stacked


def _zero_states(bsz):
    return (jnp.zeros((1, bsz, HG_HEADS, HG_DIM, HG_DIM), F32),
            jnp.zeros((1, bsz, S5_GROUPS, S5_STATE), F32),
            jnp.zeros((1, bsz, S5_GROUPS, S5_STATE), F32),
            jnp.zeros((1, bsz, M2_HEADS, M2_HEADDIM, M2_STATE), F32),
            jnp.zeros((1, bsz, M2_CONV - 1, M2_CONV_DIM), F32))


def _pad_time(a, axis, t_pad):
    t = a.shape[axis]
    if t == t_pad:
        return a
    widths = [(0, 0)] * a.ndim
    widths[axis] = (0, t_pad - t)
    return jnp.pad(a, widths)


def kernel(x_prompt, x_sample, state_hgrn, state_s5_re, state_s5_im, state_ssm, state_conv,
           p_prompt, p_sample,
           norm_mix, w_in, hg_lb_logits, hg_gnorm, w_br_hg,
           s5_a_re, s5_a_im, s5_b_re, s5_b_im, s5_c_re, s5_c_im, s5_d, s5_log_dt, s5_w_glu, s5_b_glu, w_br_s5,
           m2_conv_w, m2_conv_b, m2_dt_bias, m2_a_log, m2_d, m2_norm, w_br_m2,
           w_out,
           norm_ffn, w_rg, b_rg, w_re, b_re, w_e_gate, w_e_up, w_e_down,
           norm_ple, w_ple_gate, w_ple_proj,
           norm_final):
    w = dict(norm_mix=norm_mix, w_in=w_in, hg_lb_logits=hg_lb_logits, hg_gnorm=hg_gnorm, w_br_hg=w_br_hg,
             s5_a_re=s5_a_re, s5_a_im=s5_a_im, s5_b_re=s5_b_re, s5_b_im=s5_b_im, s5_c_re=s5_c_re,
             s5_c_im=s5_c_im, s5_d=s5_d, s5_log_dt=s5_log_dt, s5_w_glu=s5_w_glu, s5_b_glu=s5_b_glu,
             w_br_s5=w_br_s5, m2_conv_w=m2_conv_w, m2_conv_b=m2_conv_b, m2_dt_bias=m2_dt_bias,
             m2_a_log=m2_a_log, m2_d=m2_d, m2_norm=m2_norm, w_br_m2=w_br_m2, w_out=w_out,
             norm_ffn=norm_ffn, w_rg=w_rg, b_rg=b_rg, w_re=w_re, b_re=b_re, w_e_gate=w_e_gate,
             w_e_up=w_e_up, w_e_down=w_e_down, norm_ple=norm_ple, w_ple_gate=w_ple_gate,
             w_ple_proj=w_ple_proj, norm_final=norm_final)
    pw = _prep_weights(w)

    bp, tp, _ = x_prompt.shape
    cfg_p = dict(hg_chunk=min(64, tp), hg_bt=min(2, bp), s5_tc=min(64, tp), s5_bt=SUBLANE,
                 m2_chunk=min(128, tp), m2_bt=1)
    y_p, st_p = _trunk(x_prompt, p_prompt, _zero_states(bp), w, pw, tp, cfg_p)

    bs, ts, _ = x_sample.shape
    ts_pad = -(-ts // SUBLANE) * SUBLANE
    cfg_s = dict(hg_chunk=ts_pad, hg_bt=min(8, bs), s5_tc=ts_pad, s5_bt=min(32, bs),
                 m2_chunk=ts_pad, m2_bt=min(4, bs))
    y_s, st_s = _trunk(_pad_time(x_sample, 1, ts_pad), _pad_time(p_sample, 2, ts_pad),
                       (state_hgrn, state_s5_re, state_s5_im, state_ssm, state_conv), w, pw, ts, cfg_s)
    return (y_p, y_s[:, :ts]) + st_p + st_s
```

```python
import functools

import jax
import jax.numpy as jnp
from jax import lax
from jax.experimental import pallas as pl
from jax.experimental.pallas import tpu as pltpu

F32 = jnp.float32
BF16 = jnp.bfloat16

D_MODEL = 1024
DEPTH = 2
PLE_DIM = 256
NORM_EPS = 1e-6

HG_HEADS = 4
HG_DIM = 128
HG_W = HG_HEADS * HG_DIM

S5_WIDTH = 512
S5_GROUP = 16
S5_GROUPS = S5_WIDTH // S5_GROUP
S5_STATE = 64
S5_HW = S5_GROUPS * S5_STATE
S5_SLABS = 2
S5_SLAB_U = S5_WIDTH // S5_SLABS
S5_SLAB_H = S5_HW // S5_SLABS

M2_INNER = 1024
M2_HEADDIM = 64
M2_HEADS = M2_INNER // M2_HEADDIM
M2_NGROUPS = 4
M2_STATE = 128
M2_CONV = 4
M2_BC = M2_NGROUPS * M2_STATE
M2_CONV_DIM = M2_INNER + 2 * M2_BC
M2_PAIRS = M2_HEADS // 2
M2_NORM_W = M2_INNER // M2_NGROUPS

N_BRANCH = 3
E_GROUPS = 4
E_PER_GROUP = 4
N_EXPERTS = E_GROUPS * E_PER_GROUP
E_FF = 256

LANE = 128
SUBLANE = 8

COL_XBC = 0
COL_Z = COL_XBC + M2_CONV_DIM
COL_GATE = COL_Z + M2_INNER
COL_Q = COL_GATE + N_BRANCH * D_MODEL
COL_F = COL_Q + HG_W
COL_I = COL_F + HG_W
COL_G = COL_I + HG_W
COL_U = COL_G + HG_W
COL_DT = COL_U + S5_WIDTH
Z_TN = 1280
Z_COLS = 7 * Z_TN
assert COL_DT + LANE <= Z_COLS and COL_DT % LANE == 0

_O_Q, _O_F, _O_I, _O_G, _O_U = 0, 512, 1024, 1536, 2048
_O_Z = 2560
_O_XBC = _O_Z + M2_INNER
_O_DT = _O_XBC + M2_CONV_DIM
_O_GATE = _O_DT + M2_HEADS
IN_COLS = _O_GATE + N_BRANCH * D_MODEL

VMEM_LIMIT = 56 * 1024 * 1024


def _cparams(sem):
    return pltpu.CompilerParams(dimension_semantics=sem, vmem_limit_bytes=VMEM_LIMIT)


def _rms_scale(x):
    return lax.rsqrt(jnp.mean(x * x, axis=-1, keepdims=True) + NORM_EPS)


def _silu(x):
    return x * jax.nn.sigmoid(x)


def _dot(a, b):
    return jnp.dot(a, b, preferred_element_type=F32)


def _dot_nt(a, b):
    return lax.dot_general(a, b, (((1,), (1,)), ((), ())), preferred_element_type=F32)


def _dot_tn(a, b):
    return lax.dot_general(a, b, (((0,), (0,)), ((), ())), preferred_element_type=F32)


def _cumsum_rows(x):
    n = x.shape[0]
    row = lax.broadcasted_iota(jnp.int32, x.shape, 0)
    s = 1
    while s < n:
        x = x + jnp.where(row >= s, pltpu.roll(x, s, 0), 0.0)
        s *= 2
    return x


def _row_tile(n, pref):
    t = min(pref, n)
    assert n % t == 0
    return t


def _win_prep_kernel(w_ref, o_ref):
    w = w_ref[...]
    rows = w.shape[0]
    pieces = [
        w[:, _O_XBC:_O_XBC + M2_CONV_DIM], w[:, _O_Z:_O_Z + M2_INNER], w[:, _O_GATE:IN_COLS],
        w[:, :_O_Z], w[:, _O_DT:_O_DT + M2_HEADS], jnp.zeros((rows, Z_COLS - IN_COLS), F32)]
    o_ref[...] = jnp.concatenate(pieces, axis=1).astype(BF16)


def _win_prep(w_in):
    tr = 256
    return pl.pallas_call(
        _win_prep_kernel,
        grid=(DEPTH, D_MODEL // tr),
        in_specs=[pl.BlockSpec((None, tr, IN_COLS), lambda l, i: (l, i, 0))],
        out_specs=pl.BlockSpec((None, tr, Z_COLS), lambda l, i: (l, i, 0)),
        out_shape=jax.ShapeDtypeStruct((DEPTH, D_MODEL, Z_COLS), BF16),
        compiler_params=_cparams(("parallel", "parallel")),
    )(w_in)


def _norm_matmul_kernel(x_ref, nw_ref, w_ref, o_ref, xn_ref):
    @pl.when(pl.program_id(1) == 0)
    def _():
        x = x_ref[...]
        xn_ref[...] = (x * _rms_scale(x) * nw_ref[...]).astype(BF16)

    o_ref[...] = _dot(xn_ref[...], w_ref[...])


def _norm_matmul(h, nw, w, layer):
    n = h.shape[0]
    cols = w.shape[2]
    tm = _row_tile(n, 1024)
    tn = Z_TN
    return pl.pallas_call(
        _norm_matmul_kernel,
        grid=(n // tm, cols // tn),
        in_specs=[
            pl.BlockSpec((tm, D_MODEL), lambda i, j: (i, 0)),
            pl.BlockSpec((1, D_MODEL), lambda i, j: (0, 0)),
            pl.BlockSpec((None, D_MODEL, tn), lambda i, j: (layer, 0, j)),
        ],
        out_specs=pl.BlockSpec((tm, tn), lambda i, j: (i, j)),
        out_shape=jax.ShapeDtypeStruct((n, cols), F32),
        scratch_shapes=[pltpu.VMEM((tm, D_MODEL), BF16)],
        compiler_params=_cparams(("parallel", "arbitrary")),
    )(h, nw, w)


def _hgrn_chunk(q, k, v, logf, st, lev, dcode):
    C = q.shape[0]
    b = _cumsum_rows(logf)
    b_end = b[C - 1:C, :]
    o = _dot((q * jnp.exp(b)).astype(BF16), st.astype(BF16))

    sc = jnp.where(dcode == 0, jnp.sum(q * k, axis=1, keepdims=True), 0.0)
    for j in range(1, min(SUBLANE, C)):
        kr = pltpu.roll(k, j, 0)
        br = pltpu.roll(b, j, 0)
        tj = jnp.sum(q * kr * jnp.exp(jnp.minimum(b - br, 0.0)), axis=1, keepdims=True)
        sc = jnp.where(dcode == j, tj, sc)

    m = C // 2
    while m >= SUBLANE:
        parts = []
        for p0 in range(0, C, 2 * m):
            parts.append(jnp.broadcast_to(b[p0 + m - 1:p0 + m, :], (2 * m, HG_DIM)))
        r = parts[0] if len(parts) == 1 else jnp.concatenate(parts, axis=0)
        qe = q * jnp.exp(jnp.minimum(b - r, 0.0))
        ke = k * jnp.exp(jnp.minimum(r - b, 0.0))
        pm = _dot_nt(qe.astype(BF16), ke.astype(BF16))
        sc = jnp.where((lev >> (m.bit_length() - 1)) == 1, pm, sc)
        m //= 2

    v16 = v.astype(BF16)
    o = o + _dot(sc.astype(BF16), v16)
    ke = k * jnp.exp(b_end - b)
    decay = jnp.transpose(jnp.broadcast_to(jnp.exp(b_end), (SUBLANE, HG_DIM)))[:, 0:1]
    st_new = st * decay + _dot_tn(ke.astype(BF16), v16)
    return o, st_new


def _hgrn_kernel(zq_ref, zf_ref, zi_ref, zg_ref, lbl_ref, gw_ref, s0_ref, *rest,
                 layer, chunk, bt, t_valid, t_total):
    o_ref, s_ref, st_ref = rest[-3:]
    c = pl.program_id(1)
    C = chunk

    @pl.when(c == 0)
    def _():
        st_ref[...] = s0_ref[...]

    lg = lbl_ref[...]
    e = jnp.exp(lg - jnp.max(lg, axis=0, keepdims=True))
    prob = e / jnp.sum(e, axis=0, keepdims=True)
    lb_all = jnp.zeros((1, HG_W), F32)
    for j in range(1, layer + 1):
        lb_all = lb_all + prob[j:j + 1]

    rr = lax.broadcasted_iota(jnp.int32, (C, C), 0)
    cc = lax.broadcasted_iota(jnp.int32, (C, C), 1)
    xr = rr ^ cc
    lev = jnp.where(rr > cc, xr, -1)
    dcode = jnp.where(xr < SUBLANE, rr - cc, -1)
    if t_valid < t_total:
        valid = (c * C + lax.broadcasted_iota(jnp.int32, (C, HG_DIM), 0)) < t_valid
    gw = gw_ref[...]

    for bi in range(bt):
        for h in range(HG_HEADS):
            sl = slice(h * HG_DIM, (h + 1) * HG_DIM)
            lb = lb_all[:, sl]
            zf = zf_ref[bi, :, sl]
            q = _silu(zq_ref[bi, :, sl])
            logf = jnp.log(lb + (1.0 - lb) * jax.nn.sigmoid(zf))
            k = (1.0 - lb) * jax.nn.sigmoid(-zf)
            if t_valid < t_total:
                logf = jnp.where(valid, logf, 0.0)
                k = jnp.where(valid, k, 0.0)
            o, st_new = _hgrn_chunk(q, k, zi_ref[bi, :, sl], logf, st_ref[bi, h], lev, dcode)
            o_ref[bi, :, sl] = o * _rms_scale(o) * gw * _silu(zg_ref[bi, :, sl])
            st_ref[bi, h] = st_new

    @pl.when(c == pl.num_programs(1) - 1)
    def _():
        s_ref[...] = st_ref[...]


def _hgrn(z3, lb_logits, gw, s0, prev, layer, state_layer, chunk, bt, t_valid):
    bsz, t, _ = z3.shape
    nc = t // chunk
    kern = functools.partial(_hgrn_kernel, layer=layer, chunk=chunk, bt=bt, t_valid=t_valid, t_total=t)

    def zspec(col):
        return pl.BlockSpec((bt, chunk, HG_W), lambda b, c: (b, c, col // HG_W))

    in_specs = [
        zspec(COL_Q), zspec(COL_F), zspec(COL_I), zspec(COL_G),
        pl.BlockSpec((DEPTH, HG_W), lambda b, c: (0, 0)),
        pl.BlockSpec((1, HG_DIM), lambda b, c: (0, 0)),
        pl.BlockSpec((None, bt, HG_HEADS, HG_DIM, HG_DIM), lambda b, c: (state_layer, b, 0, 0, 0)),
    ]
    args = [z3, z3, z3, z3, lb_logits, gw, s0]
    aliases = {}
    if prev is not None:
        in_specs.append(pl.BlockSpec(memory_space=pl.ANY))
        args.append(prev)
        aliases = {len(args) - 1: 1}
    return pl.pallas_call(
        kern,
        grid=(bsz // bt, nc),
        in_specs=in_specs,
        out_specs=[
            pl.BlockSpec((bt, chunk, HG_W), lambda b, c: (b, c, 0)),
            pl.BlockSpec((None, bt, HG_HEADS, HG_DIM, HG_DIM), lambda b, c: (layer, b, 0, 0, 0)),
        ],
        out_shape=[
            jax.ShapeDtypeStruct((bsz, t, HG_W), F32),
            jax.ShapeDtypeStruct((DEPTH, bsz, HG_HEADS, HG_DIM, HG_DIM), F32),
        ],
        scratch_shapes=[pltpu.VMEM((bt, HG_HEADS, HG_DIM, HG_DIM), F32)],
        input_output_aliases=aliases,
        compiler_params=_cparams(("parallel", "arbitrary")),
    )(*args)


def _s5_kernel(u_ref, bmat_ref, cmat_ref, a_ref, d_ref, wglu_ref, bglu_ref, hre0_ref, him0_ref,
               o_ref, hre_ref, him_ref, hs_ref, hst_ref, *, tc, bt, steps, lane_chunk):
    c = pl.program_id(1)

    def re_off(s):
        return 2 * s * S5_SLAB_H

    def im_off(s):
        return (2 * s + 1) * S5_SLAB_H

    @pl.when(c == 0)
    def _():
        for s in range(S5_SLABS):
            src = slice(s * S5_SLAB_H, (s + 1) * S5_SLAB_H)
            hst_ref[:, re_off(s):re_off(s) + S5_SLAB_H] = hre0_ref[:, src]
            hst_ref[:, im_off(s):im_off(s) + S5_SLAB_H] = him0_ref[:, src]

    u = u_ref[...].reshape(tc * bt, S5_WIDTH)
    u16 = u.astype(BF16)
    for s in range(S5_SLABS):
        hs_ref[:, re_off(s):re_off(s + 1)] = _dot(u16[:, s * S5_SLAB_U:(s + 1) * S5_SLAB_U], bmat_ref[s])

    for s in range(S5_SLABS):
        for l0 in range(0, S5_SLAB_H, lane_chunk):
            re_sl = slice(re_off(s) + l0, re_off(s) + l0 + lane_chunk)
            im_sl = slice(im_off(s) + l0, im_off(s) + l0 + lane_chunk)
            are = a_ref[:, re_sl]
            aim = a_ref[:, im_sl]

            def step(t, carry, re_sl=re_sl, im_sl=im_sl, are=are, aim=aim):
                hre, him = carry
                r0 = pl.multiple_of(t * bt, bt)
                nre = are * hre - aim * him + hs_ref[pl.ds(r0, bt), re_sl]
                nim = are * him + aim * hre + hs_ref[pl.ds(r0, bt), im_sl]
                hs_ref[pl.ds(r0, bt), re_sl] = nre
                hs_ref[pl.ds(r0, bt), im_sl] = nim
                return nre, nim

            hre, him = lax.fori_loop(0, steps, step, (hst_ref[:, re_sl], hst_ref[:, im_sl]), unroll=True)
            hst_ref[:, re_sl] = hre
            hst_ref[:, im_sl] = him

    ys = [_dot(hs_ref[:, re_off(s):re_off(s + 1)].astype(BF16), cmat_ref[s]) for s in range(S5_SLABS)]
    y = jnp.concatenate(ys, axis=1) + d_ref[...] * u
    g = jax.nn.gelu(y)
    gl = _dot(g.astype(BF16), wglu_ref[...]) + bglu_ref[...]
    o = gl[:, :S5_WIDTH] * jax.nn.sigmoid(gl[:, S5_WIDTH:])
    o_ref[...] = o.reshape(tc, bt, S5_WIDTH)

    @pl.when(c == pl.num_programs(1) - 1)
    def _():
        for s in range(S5_SLABS):
            dst = slice(s * S5_SLAB_H, (s + 1) * S5_SLAB_H)
            hre_ref[:, dst] = hst_ref[:, re_off(s):re_off(s) + S5_SLAB_H]
            him_ref[:, dst] = hst_ref[:, im_off(s):im_off(s) + S5_SLAB_H]


def _s5(u_tm, bmat, cmat, a_vec, d_vec, wglu, bglu, hre0, him0, state_layer, tc, bt, t_valid):
    t, bsz, _ = u_tm.shape
    nc = t // tc
    if t_valid < t:
        assert nc == 1
    steps = tc if t_valid == t else t_valid
    lane_chunk = min(S5_SLAB_H, max(LANE, (8 * SUBLANE * LANE) // bt))
    kern = functools.partial(_s5_kernel, tc=tc, bt=bt, steps=steps, lane_chunk=lane_chunk)
    full = lambda shape: pl.BlockSpec(shape, lambda b, c: (0,) * len(shape))
    return pl.pallas_call(
        kern,
        grid=(bsz // bt, nc),
        in_specs=[
            pl.BlockSpec((tc, bt, S5_WIDTH), lambda b, c: (c, b, 0)),
            full((S5_SLABS, S5_SLAB_U, 2 * S5_SLAB_H)),
            full((S5_SLABS, 2 * S5_SLAB_H, S5_SLAB_U)),
            full((1, 2 * S5_HW)),
            full((1, S5_WIDTH)),
            full((S5_WIDTH, 2 * S5_WIDTH)),
            full((1, 2 * S5_WIDTH)),
            pl.BlockSpec((None, bt, S5_HW), lambda b, c: (state_layer, b, 0)),
            pl.BlockSpec((None, bt, S5_HW), lambda b, c: (state_layer, b, 0)),
        ],
        out_specs=[
            pl.BlockSpec((tc, bt, S5_WIDTH), lambda b, c: (c, b, 0)),
            pl.BlockSpec((bt, S5_HW), lambda b, c: (b, 0)),
            pl.BlockSpec((bt, S5_HW), lambda b, c: (b, 0)),
        ],
        out_shape=[
            jax.ShapeDtypeStruct((t, bsz, S5_WIDTH), F32),
            jax.ShapeDtypeStruct((bsz, S5_HW), F32),
            jax.ShapeDtypeStruct((bsz, S5_HW), F32),
        ],
        scratch_shapes=[
            pltpu.VMEM((tc * bt, 2 * S5_HW), F32),
            pltpu.VMEM((bt, 2 * S5_HW), F32),
        ],
        compiler_params=_cparams(("parallel", "arbitrary")),
    )(u_tm, bmat, cmat, a_vec, d_vec, wglu, bglu, hre0, him0)


def _s5_params(a_re, a_im, b_re, b_im, c_re, c_im, log_dt):
    g = S5_GROUPS
    dt = jnp.exp(log_dt)[:, None]
    er = jnp.exp(a_re * dt)
    abr = er * jnp.cos(a_im * dt)
    abi = er * jnp.sin(a_im * dt)
    den = a_re * a_re + a_im * a_im
    xr = abr - 1.0
    cr = (xr * a_re + abi * a_im) / den
    ci = (abi * a_re - xr * a_im) / den
    bbr = cr[..., None] * b_re - ci[..., None] * b_im
    bbi = cr[..., None] * b_im + ci[..., None] * b_re
    sg = g // S5_SLABS
    eye = jnp.eye(sg, dtype=F32)
    slab = lambda x: x.reshape((S5_SLABS, sg) + x.shape[1:])
    bd_in = lambda x: jnp.einsum('sgnc,gh->sgchn', slab(x), eye).reshape(S5_SLABS, S5_SLAB_U, S5_SLAB_H)
    bd_out = lambda x: jnp.einsum('sgcn,gh->sgnhc', slab(x), eye).reshape(S5_SLABS, S5_SLAB_H, S5_SLAB_U)
    bmat = jnp.concatenate([bd_in(bbr), bd_in(bbi)], axis=2).astype(BF16)
    cmat = jnp.concatenate([bd_out(c_re), -bd_out(c_im)], axis=1).astype(BF16)
    lanes = lambda x: x.reshape(S5_SLABS, 1, S5_SLAB_H)
    a_vec = jnp.concatenate([lanes(abr), lanes(abi)], axis=2).reshape(1, 2 * S5_HW)
    return bmat, cmat, a_vec


def _ssd_kernel(xbc_ref, zz_ref, zdt_ref, cw_ref, cb_ref, dtb_ref, alog_ref, dsk_ref, nw_ref,
                conv0_ref, s0_ref, *rest, chunk, bt, t_valid, t_total):
    y_ref, s_ref, cn_ref, st_ref, tail_ref = rest[-5:]
    c = pl.program_id(1)
    nc = t_total // chunk
    L = chunk
    TAIL = SUBLANE
    last_valid = t_valid - (nc - 1) * L

    @pl.when(c == 0)
    def _():
        st_ref[...] = s0_ref[...]
        tail_ref[...] = jnp.zeros((bt, TAIL, M2_CONV_DIM), F32)
        tail_ref[:, TAIL - (M2_CONV - 1):TAIL, :] = conv0_ref[...]

    causal = lax.broadcasted_iota(jnp.int32, (L, L), 0) >= lax.broadcasted_iota(jnp.int32, (L, L), 1)
    lo_lane = lax.broadcasted_iota(jnp.int32, (L, LANE), 1) < M2_HEADDIM
    lo_row = lax.broadcasted_iota(jnp.int32, (LANE, LANE), 0) < M2_HEADDIM
    neg_a = -jnp.exp(alog_ref[...])
    nw = nw_ref[...]
    if t_valid < t_total:
        valid = (c * L + lax.broadcasted_iota(jnp.int32, (L, LANE), 0)) < t_valid

    def pair_cols(a, h0):
        return jnp.where(lo_lane, a[:, h0:h0 + 1], a[:, h0 + 1:h0 + 2])

    for bi in range(bt):
        xbc = xbc_ref[bi]
        ext = jnp.concatenate([tail_ref[bi], xbc], axis=0)
        conv = cb_ref[...]
        for j in range(M2_CONV):
            o0 = TAIL - (M2_CONV - 1) + j
            conv = conv + cw_ref[j:j + 1, :] * ext[o0:o0 + L, :]
        tail_ref[bi] = xbc[L - TAIL:L, :]

        @pl.when(c == nc - 1)
        def _(ext=ext, bi=bi):
            cn_ref[bi] = ext[TAIL + last_valid - (M2_CONV - 1):TAIL + last_valid, :]

        act = _silu(conv)
        xs = act[:, :M2_INNER]
        bm = act[:, M2_INNER:M2_INNER + M2_BC].astype(BF16)
        cm = act[:, M2_INNER + M2_BC:].astype(BF16)

        dtr = zdt_ref[bi] + dtb_ref[...]
        dt = jnp.maximum(dtr, 0.0) + jnp.log1p(jnp.exp(-jnp.abs(dtr)))
        if t_valid < t_total:
            dt = jnp.where(valid, dt, 0.0)
        acs = _cumsum_rows(dt * neg_a)
        acs_t = acs.T
        acs_end = acs[L - 1:L, :]
        e_acs = jnp.exp(acs)
        e_end = jnp.exp(acs_end - acs)
        e_tot = jnp.exp(acs_end)

        ys = []
        for g in range(M2_NGROUPS):
            bg = bm[:, g * M2_STATE:(g + 1) * M2_STATE]
            cg = cm[:, g * M2_STATE:(g + 1) * M2_STATE]
            cb = _dot_nt(cg, bg)
            for pr in range(2):
                p = g * 2 + pr
                h0 = 2 * p
                xp = xs[:, p * LANE:(p + 1) * LANE]
                xdt = xp * pair_cols(dt, h0)
                ydiag = None
                for hh in range(2):
                    h = h0 + hh
                    lm = jnp.where(causal, jnp.exp(jnp.minimum(acs[:, h:h + 1] - acs_t[h:h + 1, :], 0.0)), 0.0)
                    mk = lo_lane if hh == 0 else jnp.logical_not(lo_lane)
                    part = _dot((cb * lm).astype(BF16), jnp.where(mk, xdt, 0.0).astype(BF16))
                    ydiag = part if ydiag is None else ydiag + part
                sp = st_ref[bi, p]
                yoff = pair_cols(e_acs, h0) * _dot_nt(cg, sp.astype(BF16))
                ys.append(ydiag + yoff + dsk_ref[:, p * LANE:(p + 1) * LANE] * xp)
                xe = (xdt * pair_cols(e_end, h0)).astype(BF16)
                scale = jnp.where(lo_row, e_tot[:, h0:h0 + 1], e_tot[:, h0 + 1:h0 + 2])
                st_ref[bi, p] = scale * sp + _dot_tn(xe, bg)

        y = jnp.concatenate(ys, axis=1) * _silu(zz_ref[bi])
        outs = []
        for g in range(M2_NGROUPS):
            sl = slice(g * M2_NORM_W, (g + 1) * M2_NORM_W)
            yg = y[:, sl]
            outs.append(yg * _rms_scale(yg) * nw[:, sl])
        y_ref[bi] = jnp.concatenate(outs, axis=1)

    @pl.when(c == nc - 1)
    def _():
        s_ref[...] = st_ref[...]


def _ssd(z3, cw, cb, dtb, alog, dsk, nw, conv0, s0, prev, layer, state_layer, chunk, bt, t_valid):
    bsz, t, _ = z3.shape
    nc = t // chunk
    assert chunk >= SUBLANE and t_valid - (nc - 1) * chunk >= 1
    kern = functools.partial(_ssd_kernel, chunk=chunk, bt=bt, t_valid=t_valid, t_total=t)
    full = lambda shape: pl.BlockSpec(shape, lambda b, c: (0,) * len(shape))
    extra_specs, extra_args, aliases = [], [], {}
    if prev is not None:
        extra_specs = [pl.BlockSpec(memory_space=pl.ANY)] * 2
        extra_args = list(prev)
        aliases = {11: 1, 12: 2}
    return pl.pallas_call(
        kern,
        grid=(bsz // bt, nc),
        input_output_aliases=aliases,
        in_specs=[
            pl.BlockSpec((bt, chunk, M2_CONV_DIM), lambda b, c: (b, c, COL_XBC // M2_CONV_DIM)),
            pl.BlockSpec((bt, chunk, M2_INNER), lambda b, c: (b, c, COL_Z // M2_INNER)),
            pl.BlockSpec((bt, chunk, LANE), lambda b, c: (b, c, COL_DT // LANE)),
            full((M2_CONV, M2_CONV_DIM)),
            full((1, M2_CONV_DIM)),
            full((1, LANE)),
            full((1, LANE)),
            full((1, M2_INNER)),
            full((1, M2_INNER)),
            pl.BlockSpec((None, bt, M2_CONV - 1, M2_CONV_DIM), lambda b, c: (state_layer, b, 0, 0)),
            pl.BlockSpec((None, bt, M2_PAIRS, LANE, M2_STATE), lambda b, c: (state_layer, b, 0, 0, 0)),
        ] + extra_specs,
        out_specs=[
            pl.BlockSpec((bt, chunk, M2_INNER), lambda b, c: (b, c, 0)),
            pl.BlockSpec((None, bt, M2_PAIRS, LANE, M2_STATE), lambda b, c: (layer, b, 0, 0, 0)),
            pl.BlockSpec((None, bt, M2_CONV - 1, M2_CONV_DIM), lambda b, c: (layer, b, 0, 0)),
        ],
        out_shape=[
            jax.ShapeDtypeStruct((bsz, t, M2_INNER), F32),
            jax.ShapeDtypeStruct((DEPTH, bsz, M2_PAIRS, LANE, M2_STATE), F32),
            jax.ShapeDtypeStruct((DEPTH, bsz, M2_CONV - 1, M2_CONV_DIM), F32),
        ],
        scratch_shapes=[
            pltpu.VMEM((bt, M2_PAIRS, LANE, M2_STATE), F32),
            pltpu.VMEM((bt, SUBLANE, M2_CONV_DIM), F32),
        ],
        compiler_params=_cparams(("parallel", "arbitrary")),
    )(z3, z3, z3, cw, cb, dtb, alog, dsk, nw, conv0, s0, *extra_args)


def _merge_kernel(oa_ref, ob_ref, oc_ref, ga_ref, gb_ref, gc_ref, h_ref, wa_ref, wb_ref, wc_ref, wo_ref, o_ref):
    m = jax.nn.sigmoid(ga_ref[...]) * _dot(oa_ref[...].astype(BF16), wa_ref[...])
    m = m + jax.nn.sigmoid(gb_ref[...]) * _dot(ob_ref[...].astype(BF16), wb_ref[...])
    m = m + jax.nn.sigmoid(gc_ref[...]) * _dot(oc_ref[...].astype(BF16), wc_ref[...])
    o_ref[...] = h_ref[...] + _dot(m.astype(BF16), wo_ref[...])


def _merge(oa, ob, oc, z, h, wa, wb, wc, wo, layer):
    n = h.shape[0]
    tm = _row_tile(n, 512)
    gblk = COL_GATE // D_MODEL
    row = lambda w, j=0: pl.BlockSpec((tm, w), lambda i: (i, j))
    full = lambda shape: pl.BlockSpec((None,) + shape, lambda i: (layer,) + (0,) * len(shape))
    return pl.pallas_call(
        _merge_kernel,
        grid=(n // tm,),
        in_specs=[
            row(HG_W), row(S5_WIDTH), row(M2_INNER),
            row(D_MODEL, gblk), row(D_MODEL, gblk + 1), row(D_MODEL, gblk + 2),
            row(D_MODEL),
            full((HG_W, D_MODEL)), full((S5_WIDTH, D_MODEL)), full((M2_INNER, D_MODEL)),
            full((D_MODEL, D_MODEL)),
        ],
        out_specs=row(D_MODEL),
        out_shape=jax.ShapeDtypeStruct((n, D_MODEL), F32),
        compiler_params=_cparams(("parallel",)),
    )(oa, ob, oc, z, z, z, h, wa, wb, wc, wo)


R_E0 = E_GROUPS


def _moe_kernel(h_ref, nw_ref, wr_ref, br_ref, wg_ref, wu_ref, wd_ref, o_ref, xn_ref, comb_ref, acc_ref):
    e = pl.program_id(1)
    tm = h_ref.shape[0]
    lane = lax.broadcasted_iota(jnp.int32, (tm, LANE), 1)
    neg = -jnp.inf

    @pl.when(e == 0)
    def _():
        x = h_ref[...]
        xn = x * _rms_scale(x) * nw_ref[...]
        xn_ref[...] = xn.astype(BF16)
        logits = jnp.dot(xn, wr_ref[...], preferred_element_type=F32,
                         precision=lax.Precision.HIGHEST) + br_ref[...]
        gl = jnp.where(lane < E_GROUPS, logits, neg)
        gmax = jnp.max(gl, axis=1, keepdims=True)
        gidx = jnp.min(jnp.where(gl == gmax, lane, LANE), axis=1, keepdims=True)
        p_group = 1.0 / jnp.sum(jnp.exp(gl - gmax), axis=1, keepdims=True)
        in_grp = (lane >= R_E0) & (lane < R_E0 + N_EXPERTS) & (((lane - R_E0) >> 2) == gidx)
        ev = jnp.where(in_grp, logits, neg)
        v1 = jnp.max(ev, axis=1, keepdims=True)
        i1 = jnp.min(jnp.where(ev == v1, lane, LANE), axis=1, keepdims=True)
        ev2 = jnp.where(lane == i1, neg, ev)
        v2 = jnp.max(ev2, axis=1, keepdims=True)
        i2 = jnp.min(jnp.where(ev2 == v2, lane, LANE), axis=1, keepdims=True)
        e2 = jnp.exp(v2 - v1)
        w1 = 1.0 / (1.0 + e2)
        comb_ref[...] = jnp.where(lane == i1, w1 * p_group, jnp.where(lane == i2, e2 * w1 * p_group, 0.0))
        acc_ref[...] = jnp.zeros_like(acc_ref)

    xn = xn_ref[...]
    hid = _silu(_dot(xn, wg_ref[...].astype(BF16))) * _dot(xn, wu_ref[...].astype(BF16))
    ce = jnp.sum(jnp.where(lane == R_E0 + e, comb_ref[...], 0.0), axis=1, keepdims=True)
    acc_ref[...] += ce * _dot(hid.astype(BF16), wd_ref[...].astype(BF16))

    @pl.when(e == N_EXPERTS - 1)
    def _():
        o_ref[...] = h_ref[...] + acc_ref[...]


def _moe(h, nw, wr, br, wg, wu, wd, layer):
    n = h.shape[0]
    tm = _row_tile(n, 1024)
    return pl.pallas_call(
        _moe_kernel,
        grid=(n // tm, N_EXPERTS),
        in_specs=[
            pl.BlockSpec((tm, D_MODEL), lambda i, e: (i, 0)),
            pl.BlockSpec((1, D_MODEL), lambda i, e: (0, 0)),
            pl.BlockSpec((D_MODEL, LANE), lambda i, e: (0, 0)),
            pl.BlockSpec((1, LANE), lambda i, e: (0, 0)),
            pl.BlockSpec((None, None, D_MODEL, E_FF), lambda i, e: (layer, e, 0, 0)),
            pl.BlockSpec((None, None, D_MODEL, E_FF), lambda i, e: (layer, e, 0, 0)),
            pl.BlockSpec((None, None, E_FF, D_MODEL), lambda i, e: (layer, e, 0, 0)),
        ],
        out_specs=pl.BlockSpec((tm, D_MODEL), lambda i, e: (i, 0)),
        out_shape=jax.ShapeDtypeStruct((n, D_MODEL), F32),
        scratch_shapes=[
            pltpu.VMEM((tm, D_MODEL), BF16),
            pltpu.VMEM((tm, LANE), F32),
            pltpu.VMEM((tm, D_MODEL), F32),
        ],
        compiler_params=_cparams(("parallel", "arbitrary")),
    )(h, nw, wr, br, wg, wu, wd)


def _ple_kernel(h_ref, p_ref, nw_ref, wg_ref, wp_ref, nf_ref, o_ref, y_ref):
    x = h_ref[...]
    xn = (x * _rms_scale(x) * nw_ref[...]).astype(BF16)
    out = x + jax.nn.sigmoid(_dot(xn, wg_ref[...])) * _dot(p_ref[...].astype(BF16), wp_ref[...])
    o_ref[...] = out
    y_ref[...] = out * _rms_scale(out) * nf_ref[...]


def _ple(h, p, nw, wg, wp, nf, layer):
    n = h.shape[0]
    tm = _row_tile(n, 1024)
    row = lambda w: pl.BlockSpec((tm, w), lambda i: (i, 0))
    full = lambda shape: pl.BlockSpec(shape, lambda i: (0,) * len(shape))
    lfull = lambda shape: pl.BlockSpec((None,) + shape, lambda i: (layer,) + (0,) * len(shape))
    return pl.pallas_call(
        _ple_kernel,
        grid=(n // tm,),
        in_specs=[row(D_MODEL), pl.BlockSpec((None, tm, PLE_DIM), lambda i: (layer, i, 0)),
                  full((1, D_MODEL)), lfull((D_MODEL, D_MODEL)),
                  lfull((PLE_DIM, D_MODEL)), full((1, D_MODEL))],
        out_specs=[row(D_MODEL), row(D_MODEL)],
        out_shape=[jax.ShapeDtypeStruct((n, D_MODEL), F32), jax.ShapeDtypeStruct((n, D_MODEL), F32)],
        compiler_params=_cparams(("parallel",)),
    )(h, p, nw, wg, wp, nf)


def _prep_weights(w):
    win_r = _win_prep(w['w_in'])
    lane_pad = lambda a: jnp.pad(a, ((0, 0), (0, LANE - a.shape[1])))
    s5 = [_s5_params(w['s5_a_re'][i], w['s5_a_im'][i], w['s5_b_re'][i], w['s5_b_im'][i],
                     w['s5_c_re'][i], w['s5_c_im'][i], w['s5_log_dt'][i]) for i in range(DEPTH)]
    wr = jnp.concatenate([w['w_rg'], w['w_re'],
                          jnp.zeros((DEPTH, D_MODEL, LANE - E_GROUPS - N_EXPERTS), F32)], axis=2)
    br = jnp.concatenate([w['b_rg'], w['b_re'],
                          jnp.zeros((DEPTH, LANE - E_GROUPS - N_EXPERTS), F32)], axis=1)
    return dict(
        win=win_r,
        s5=s5,
        s5_d=w['s5_d'].reshape(DEPTH, 1, S5_WIDTH),
        wglu=w['s5_w_glu'].astype(BF16),
        bglu=w['s5_b_glu'].reshape(DEPTH, 1, 2 * S5_WIDTH),
        dtb=lane_pad(w['m2_dt_bias']).reshape(DEPTH, 1, LANE),
        alog=lane_pad(w['m2_a_log']).reshape(DEPTH, 1, LANE),
        dsk=jnp.repeat(w['m2_d'], M2_HEADDIM, axis=1).reshape(DEPTH, 1, M2_INNER),
        wa=w['w_br_hg'].astype(BF16), wb=w['w_br_s5'].astype(BF16), wc=w['w_br_m2'].astype(BF16),
        wo=w['w_out'].astype(BF16),
        wr=wr, br=br.reshape(DEPTH, 1, LANE),
        wpg=w['w_ple_gate'].astype(BF16), wpp=w['w_ple_proj'].astype(BF16),
    )


def _trunk(x, p, states, w, pw, t_valid, cfg):
    bsz, t, _ = x.shape
    n = bsz * t
    st_hg, st_re, st_im, st_ssm, st_conv = states
    per_layer = st_hg.shape[0] == DEPTH
    st_re = st_re.reshape(st_re.shape[0], bsz, S5_HW)
    st_im = st_im.reshape(st_im.shape[0], bsz, S5_HW)
    st_ssm = st_ssm.reshape(st_ssm.shape[0], bsz, M2_PAIRS, LANE, M2_STATE)
    h = x.reshape(n, D_MODEL)
    new = []
    y = None
    s_hg = None
    ssd_prev = None
    for i in range(DEPTH):
        sl = i if per_layer else 0
        z = _norm_matmul(h, w['norm_mix'][i].reshape(1, D_MODEL), pw['win'], i)
        z3 = z.reshape(bsz, t, Z_COLS)
        oa, s_hg = _hgrn(z3, w['hg_lb_logits'], w['hg_gnorm'][i].reshape(1, HG_DIM), st_hg, s_hg, i, sl,
                         cfg['hg_chunk'], cfg['hg_bt'], t_valid)
        u_tm = jnp.transpose(z3[:, :, COL_U:COL_U + S5_WIDTH], (1, 0, 2))
        bmat, cmat, a_vec = pw['s5'][i]
        ob_tm, s_re, s_im = _s5(u_tm, bmat, cmat, a_vec, pw['s5_d'][i], pw['wglu'][i], pw['bglu'][i],
                                st_re, st_im, sl, cfg['s5_tc'], cfg['s5_bt'], t_valid)
        ob = jnp.transpose(ob_tm, (1, 0, 2))
        oc, s_ssm, conv_new = _ssd(z3, w['m2_conv_w'][i], w['m2_conv_b'][i].reshape(1, M2_CONV_DIM),
                                   pw['dtb'][i], pw['alog'][i], pw['dsk'][i],
                                   w['m2_norm'][i].reshape(1, M2_INNER), st_conv, st_ssm, ssd_prev, i, sl,
                                   cfg['m2_chunk'], cfg['m2_bt'], t_valid)
        ssd_prev = (s_ssm, conv_new)
        h = _merge(oa.reshape(n, HG_W), ob.reshape(n, S5_WIDTH), oc.reshape(n, M2_INNER), z, h,
                   pw['wa'], pw['wb'], pw['wc'], pw['wo'], i)
        h = _moe(h, w['norm_ffn'][i].reshape(1, D_MODEL), pw['wr'][i], pw['br'][i],
                 w['w_e_gate'], w['w_e_up'], w['w_e_down'], i)
        h, y = _ple(h, p.reshape(DEPTH, n, PLE_DIM), w['norm_ple'][i].reshape(1, D_MODEL),
                    pw['wpg'], pw['wpp'], w['norm_final'].reshape(1, D_MODEL), i)
        new.append((s_re.reshape(bsz, S5_GROUPS, S5_STATE), s_im.reshape(bsz, S5_GROUPS, S5_STATE)))
    s5_re, s5_im = (jnp.stack([nl[j] for nl in new]) for j in range(2))
    stacked = (s_hg, s5_re, s5_im, s_ssm.reshape(DEPTH, bsz, M2_HEADS, M2_HEADDIM, M2_STATE), conv_new)
    return y.reshape(bsz, t, D_MODEL), stacked


def _zero_states(bsz):
    return (jnp.zeros((1, bsz, HG_HEADS, HG_DIM, HG_DIM), F32),
            jnp.zeros((1, bsz, S5_GROUPS, S5_STATE), F32),
            jnp.zeros((1, bsz, S5_GROUPS, S5_STATE), F32),
            jnp.zeros((1, bsz, M2_HEADS, M2_HEADDIM, M2_STATE), F32),
            jnp.zeros((1, bsz, M2_CONV - 1, M2_CONV_DIM), F32))


def _pad_time(a, axis, t_pad):
    t = a.shape[axis]
    if t == t_pad:
        return a
    widths = [(0, 0)] * a.ndim
    widths[axis] = (0, t_pad - t)
    return jnp.pad(a, widths)


def kernel(x_prompt, x_sample, state_hgrn, state_s5_re, state_s5_im, state_ssm, state_conv,
           p_prompt, p_sample,
           norm_mix, w_in, hg_lb_logits, hg_gnorm, w_br_hg,
           s5_a_re, s5_a_im, s5_b_re, s5_b_im, s5_c_re, s5_c_im, s5_d, s5_log_dt, s5_w_glu, s5_b_glu, w_br_s5,
           m2_conv_w, m2_conv_b, m2_dt_bias, m2_a_log, m2_d, m2_norm, w_br_m2,
           w_out,
           norm_ffn, w_rg, b_rg, w_re, b_re, w_e_gate, w_e_up, w_e_down,
           norm_ple, w_ple_gate, w_ple_proj,
           norm_final):
    w = dict(norm_mix=norm_mix, w_in=w_in, hg_lb_logits=hg_lb_logits, hg_gnorm=hg_gnorm, w_br_hg=w_br_hg,
             s5_a_re=s5_a_re, s5_a_im=s5_a_im, s5_b_re=s5_b_re, s5_b_im=s5_b_im, s5_c_re=s5_c_re,
             s5_c_im=s5_c_im, s5_d=s5_d, s5_log_dt=s5_log_dt, s5_w_glu=s5_w_glu, s5_b_glu=s5_b_glu,
             w_br_s5=w_br_s5, m2_conv_w=m2_conv_w, m2_conv_b=m2_conv_b, m2_dt_bias=m2_dt_bias,
             m2_a_log=m2_a_log, m2_d=m2_d, m2_norm=m2_norm, w_br_m2=w_br_m2, w_out=w_out,
             norm_ffn=norm_ffn, w_rg=w_rg, b_rg=b_rg, w_re=w_re, b_re=b_re, w_e_gate=w_e_gate,
             w_e_up=w_e_up, w_e_down=w_e_down, norm_ple=norm_ple, w_ple_gate=w_ple_gate,
             w_ple_proj=w_ple_proj, norm_final=norm_final)
    pw = _prep_weights(w)

    bp, tp, _ = x_prompt.shape
    cfg_p = dict(hg_chunk=min(64, tp), hg_bt=min(2, bp), s5_tc=min(64, tp), s5_bt=SUBLANE,
                 m2_chunk=min(128, tp), m2_bt=1)
    y_p, st_p = _trunk(x_prompt, p_prompt, _zero_states(bp), w, pw, tp, cfg_p)

    bs, ts, _ = x_sample.shape
    ts_pad = -(-ts // SUBLANE) * SUBLANE
    cfg_s = dict(hg_chunk=ts_pad, hg_bt=min(8, bs), s5_tc=ts_pad, s5_bt=min(32, bs),
                 m2_chunk=ts_pad, m2_bt=min(4, bs))
    y_s, st_s = _trunk(_pad_time(x_sample, 1, ts_pad), _pad_time(p_sample, 2, ts_pad),
                       (state_hgrn, state_s5_re, state_s5_im, state_ssm, state_conv), w, pw, ts, cfg_s)
    return (y_p, y_s[:, :ts]) + st_p + st_s
```

```python
import functools

import jax
import jax.numpy as jnp
from jax import lax
from jax.experimental import pallas as pl
from jax.experimental.pallas import tpu as pltpu

F32 = jnp.float32
BF16 = jnp.bfloat16

D_MODEL = 1024
DEPTH = 2
PLE_DIM = 256
NORM_EPS = 1e-6

HG_HEADS = 4
HG_DIM = 128
HG_W = HG_HEADS * HG_DIM

S5_WIDTH = 512
S5_GROUP = 16
S5_GROUPS = S5_WIDTH // S5_GROUP
S5_STATE = 64
S5_HW = S5_GROUPS * S5_STATE
S5_SLABS = 2
S5_SLAB_U = S5_WIDTH // S5_SLABS
S5_SLAB_H = S5_HW // S5_SLABS

M2_INNER = 1024
M2_HEADDIM = 64
M2_HEADS = M2_INNER // M2_HEADDIM
M2_NGROUPS = 4
M2_STATE = 128
M2_CONV = 4
M2_BC = M2_NGROUPS * M2_STATE
M2_CONV_DIM = M2_INNER + 2 * M2_BC
M2_PAIRS = M2_HEADS // 2
M2_NORM_W = M2_INNER // M2_NGROUPS

N_BRANCH = 3
E_GROUPS = 4
E_PER_GROUP = 4
N_EXPERTS = E_GROUPS * E_PER_GROUP
E_FF = 256

LANE = 128
SUBLANE = 8

COL_XBC = 0
COL_Z = COL_XBC + M2_CONV_DIM
COL_GATE = COL_Z + M2_INNER
COL_Q = COL_GATE + N_BRANCH * D_MODEL
COL_F = COL_Q + HG_W
COL_I = COL_F + HG_W
COL_G = COL_I + HG_W
COL_U = COL_G + HG_W
COL_DT = COL_U + S5_WIDTH
Z_TN = 1280
Z_COLS = 7 * Z_TN
assert COL_DT + LANE <= Z_COLS and COL_DT % LANE == 0

_O_Q, _O_F, _O_I, _O_G, _O_U = 0, 512, 1024, 1536, 2048
_O_Z = 2560
_O_XBC = _O_Z + M2_INNER
_O_DT = _O_XBC + M2_CONV_DIM
_O_GATE = _O_DT + M2_HEADS
IN_COLS = _O_GATE + N_BRANCH * D_MODEL

VMEM_LIMIT = 56 * 1024 * 1024


def _cparams(sem):
    return pltpu.CompilerParams(dimension_semantics=sem, vmem_limit_bytes=VMEM_LIMIT)


def _rms_scale(x):
    return lax.rsqrt(jnp.mean(x * x, axis=-1, keepdims=True) + NORM_EPS)


def _silu(x):
    return x * jax.nn.sigmoid(x)


def _dot(a, b):
    return jnp.dot(a, b, preferred_element_type=F32)


def _dot_nt(a, b):
    return lax.dot_general(a, b, (((1,), (1,)), ((), ())), preferred_element_type=F32)


def _dot_tn(a, b):
    return lax.dot_general(a, b, (((0,), (0,)), ((), ())), preferred_element_type=F32)


def _cumsum_rows(x):
    n = x.shape[0]
    row = lax.broadcasted_iota(jnp.int32, x.shape, 0)
    s = 1
    while s < n:
        x = x + jnp.where(row >= s, pltpu.roll(x, s, 0), 0.0)
        s *= 2
    return x


def _row_tile(n, pref):
    t = min(pref, n)
    assert n % t == 0
    return t


def _win_prep_kernel(w_ref, o_ref):
    w = w_ref[...]
    rows = w.shape[0]
    pieces = [
        w[:, _O_XBC:_O_XBC + M2_CONV_DIM], w[:, _O_Z:_O_Z + M2_INNER], w[:, _O_GATE:IN_COLS],
        w[:, :_O_Z], w[:, _O_DT:_O_DT + M2_HEADS], jnp.zeros((rows, Z_COLS - IN_COLS), F32)]
    o_ref[...] = jnp.concatenate(pieces, axis=1).astype(BF16)


def _win_prep(w_in):
    tr = 256
    return pl.pallas_call(
        _win_prep_kernel,
        grid=(DEPTH, D_MODEL // tr),
        in_specs=[pl.BlockSpec((None, tr, IN_COLS), lambda l, i: (l, i, 0))],
        out_specs=pl.BlockSpec((None, tr, Z_COLS), lambda l, i: (l, i, 0)),
        out_shape=jax.ShapeDtypeStruct((DEPTH, D_MODEL, Z_COLS), BF16),
        compiler_params=_cparams(("parallel", "parallel")),
    )(w_in)


def _norm_matmul_kernel(x_ref, nw_ref, w_ref, o_ref, xn_ref):
    @pl.when(pl.program_id(1) == 0)
    def _():
        x = x_ref[...]
        xn_ref[...] = (x * _rms_scale(x) * nw_ref[...]).astype(BF16)

    o_ref[...] = _dot(xn_ref[...], w_ref[...])


def _norm_matmul(h, nw, w, layer):
    n = h.shape[0]
    cols = w.shape[2]
    tm = _row_tile(n, 1024)
    tn = Z_TN
    return pl.pallas_call(
        _norm_matmul_kernel,
        grid=(n // tm, cols // tn),
        in_specs=[
            pl.BlockSpec((tm, D_MODEL), lambda i, j: (i, 0)),
            pl.BlockSpec((1, D_MODEL), lambda i, j: (0, 0)),
            pl.BlockSpec((None, D_MODEL, tn), lambda i, j: (layer, 0, j)),
        ],
        out_specs=pl.BlockSpec((tm, tn), lambda i, j: (i, j)),
        out_shape=jax.ShapeDtypeStruct((n, cols), F32),
        scratch_shapes=[pltpu.VMEM((tm, D_MODEL), BF16)],
        compiler_params=_cparams(("parallel", "arbitrary")),
    )(h, nw, w)


def _hgrn_chunk(q, k, v, logf, st, lev, dcode):
    C = q.shape[0]
    b = _cumsum_rows(logf)
    b_end = b[C - 1:C, :]
    o = _dot((q * jnp.exp(b)).astype(BF16), st.astype(BF16))

    sc = jnp.where(dcode == 0, jnp.sum(q * k, axis=1, keepdims=True), 0.0)
    for j in range(1, min(SUBLANE, C)):
        kr = pltpu.roll(k, j, 0)
        br = pltpu.roll(b, j, 0)
        tj = jnp.sum(q * kr * jnp.exp(jnp.minimum(b - br, 0.0)), axis=1, keepdims=True)
        sc = jnp.where(dcode == j, tj, sc)

    m = C // 2
    while m >= SUBLANE:
        parts = []
        for p0 in range(0, C, 2 * m):
            parts.append(jnp.broadcast_to(b[p0 + m - 1:p0 + m, :], (2 * m, HG_DIM)))
        r = parts[0] if len(parts) == 1 else jnp.concatenate(parts, axis=0)
        qe = q * jnp.exp(jnp.minimum(b - r, 0.0))
        ke = k * jnp.exp(jnp.minimum(r - b, 0.0))
        pm = _dot_nt(qe.astype(BF16), ke.astype(BF16))
        sc = jnp.where((lev >> (m.bit_length() - 1)) == 1, pm, sc)
        m //= 2

    v16 = v.astype(BF16)
    o = o + _dot(sc.astype(BF16), v16)
    ke = k * jnp.exp(b_end - b)
    decay = jnp.transpose(jnp.broadcast_to(jnp.exp(b_end), (SUBLANE, HG_DIM)))[:, 0:1]
    st_new = st * decay + _dot_tn(ke.astype(BF16), v16)
    return o, st_new


def _hgrn_kernel(zq_ref, zf_ref, zi_ref, zg_ref, lbl_ref, gw_ref, s0_ref, *rest,
                 layer, chunk, bt, t_valid, t_total):
    o_ref, s_ref, st_ref = rest[-3:]
    c = pl.program_id(1)
    C = chunk

    @pl.when(c == 0)
    def _():
        st_ref[...] = s0_ref[...]

    lg = lbl_ref[...]
    e = jnp.exp(lg - jnp.max(lg, axis=0, keepdims=True))
    prob = e / jnp.sum(e, axis=0, keepdims=True)
    lb_all = jnp.zeros((1, HG_W), F32)
    for j in range(1, layer + 1):
        lb_all = lb_all + prob[j:j + 1]

    rr = lax.broadcasted_iota(jnp.int32, (C, C), 0)
    cc = lax.broadcasted_iota(jnp.int32, (C, C), 1)
    xr = rr ^ cc
    lev = jnp.where(rr > cc, xr, -1)
    dcode = jnp.where(xr < SUBLANE, rr - cc, -1)
    if t_valid < t_total:
        valid = (c * C + lax.broadcasted_iota(jnp.int32, (C, HG_DIM), 0)) < t_valid
    gw = gw_ref[...]

    for bi in range(bt):
        for h in range(HG_HEADS):
            sl = slice(h * HG_DIM, (h + 1) * HG_DIM)
            lb = lb_all[:, sl]
            zf = zf_ref[bi, :, sl]
            q = _silu(zq_ref[bi, :, sl])
            logf = jnp.log(lb + (1.0 - lb) * jax.nn.sigmoid(zf))
            k = (1.0 - lb) * jax.nn.sigmoid(-zf)
            if t_valid < t_total:
                logf = jnp.where(valid, logf, 0.0)
                k = jnp.where(valid, k, 0.0)
            o, st_new = _hgrn_chunk(q, k, zi_ref[bi, :, sl], logf, st_ref[bi, h], lev, dcode)
            o_ref[bi, :, sl] = o * _rms_scale(o) * gw * _silu(zg_ref[bi, :, sl])
            st_ref[bi, h] = st_new

    @pl.when(c == pl.num_programs(1) - 1)
    def _():
        s_ref[...] = st_ref[...]


def _hgrn(z3, lb_logits, gw, s0, prev, layer, state_layer, chunk, bt, t_valid):
    bsz, t, _ = z3.shape
    nc = t // chunk
    kern = functools.partial(_hgrn_kernel, layer=layer, chunk=chunk, bt=bt, t_valid=t_valid, t_total=t)

    def zspec(col):
        return pl.BlockSpec((bt, chunk, HG_W), lambda b, c: (b, c, col // HG_W))

    in_specs = [
        zspec(COL_Q), zspec(COL_F), zspec(COL_I), zspec(COL_G),
        pl.BlockSpec((DEPTH, HG_W), lambda b, c: (0, 0)),
        pl.BlockSpec((1, HG_DIM), lambda b, c: (0, 0)),
        pl.BlockSpec((None, bt, HG_HEADS, HG_DIM, HG_DIM), lambda b, c: (state_layer, b, 0, 0, 0)),
    ]
    args = [z3, z3, z3, z3, lb_logits, gw, s0]
    aliases = {}
    if prev is not None:
        in_specs.append(pl.BlockSpec(memory_space=pl.ANY))
        args.append(prev)
        aliases = {len(args) - 1: 1}
    return pl.pallas_call(
        kern,
        grid=(bsz // bt, nc),
        in_specs=in_specs,
        out_specs=[
            pl.BlockSpec((bt, chunk, HG_W), lambda b, c: (b, c, 0)),
            pl.BlockSpec((None, bt, HG_HEADS, HG_DIM, HG_DIM), lambda b, c: (layer, b, 0, 0, 0)),
        ],
        out_shape=[
            jax.ShapeDtypeStruct((bsz, t, HG_W), F32),
            jax.ShapeDtypeStruct((DEPTH, bsz, HG_HEADS, HG_DIM, HG_DIM), F32),
        ],
        scratch_shapes=[pltpu.VMEM((bt, HG_HEADS, HG_DIM, HG_DIM), F32)],
        input_output_aliases=aliases,
        compiler_params=_cparams(("parallel", "arbitrary")),
    )(*args)


def _s5_kernel(u_ref, bmat_ref, cmat_ref, a_ref, d_ref, wglu_ref, bglu_ref, hre0_ref, him0_ref,
               o_ref, hre_ref, him_ref, hs_ref, hst_ref, *, tc, bt, steps, lane_chunk):
    c = pl.program_id(1)

    def re_off(s):
        return 2 * s * S5_SLAB_H

    def im_off(s):
        return (2 * s + 1) * S5_SLAB_H

    @pl.when(c == 0)
    def _():
        for s in range(S5_SLABS):
            src = slice(s * S5_SLAB_H, (s + 1) * S5_SLAB_H)
            hst_ref[:, re_off(s):re_off(s) + S5_SLAB_H] = hre0_ref[:, src]
            hst_ref[:, im_off(s):im_off(s) + S5_SLAB_H] = him0_ref[:, src]

    u = pltpu.einshape("btd->tbd", u_ref[...]).reshape(tc * bt, S5_WIDTH)
    u16 = u.astype(BF16)
    for s in range(S5_SLABS):
        hs_ref[:, re_off(s):re_off(s + 1)] = _dot(u16[:, s * S5_SLAB_U:(s + 1) * S5_SLAB_U], bmat_ref[s])

    for s in range(S5_SLABS):
        for l0 in range(0, S5_SLAB_H, lane_chunk):
            re_sl = slice(re_off(s) + l0, re_off(s) + l0 + lane_chunk)
            im_sl = slice(im_off(s) + l0, im_off(s) + l0 + lane_chunk)
            are = a_ref[:, re_sl]
            aim = a_ref[:, im_sl]

            def step(t, carry, re_sl=re_sl, im_sl=im_sl, are=are, aim=aim):
                hre, him = carry
                r0 = pl.multiple_of(t * bt, bt)
                nre = are * hre - aim * him + hs_ref[pl.ds(r0, bt), re_sl]
                nim = are * him + aim * hre + hs_ref[pl.ds(r0, bt), im_sl]
                hs_ref[pl.ds(r0, bt), re_sl] = nre
                hs_ref[pl.ds(r0, bt), im_sl] = nim
                return nre, nim

            hre, him = lax.fori_loop(0, steps, step, (hst_ref[:, re_sl], hst_ref[:, im_sl]), unroll=True)
            hst_ref[:, re_sl] = hre
            hst_ref[:, im_sl] = him

    ys = [_dot(hs_ref[:, re_off(s):re_off(s + 1)].astype(BF16), cmat_ref[s]) for s in range(S5_SLABS)]
    y = jnp.concatenate(ys, axis=1) + d_ref[...] * u
    g = jax.nn.gelu(y)
    gl = _dot(g.astype(BF16), wglu_ref[...]) + bglu_ref[...]
    o = gl[:, :S5_WIDTH] * jax.nn.sigmoid(gl[:, S5_WIDTH:])
    o_ref[...] = pltpu.einshape("tbd->btd", o.reshape(tc, bt, S5_WIDTH))

    @pl.when(c == pl.num_programs(1) - 1)
    def _():
        for s in range(S5_SLABS):
            dst = slice(s * S5_SLAB_H, (s + 1) * S5_SLAB_H)
            hre_ref[:, dst] = hst_ref[:, re_off(s):re_off(s) + S5_SLAB_H]
            him_ref[:, dst] = hst_ref[:, im_off(s):im_off(s) + S5_SLAB_H]


def _s5(z3, bmat, cmat, a_vec, d_vec, wglu, bglu, hre0, him0, state_layer, tc, bt, t_valid):
    bsz, t, _ = z3.shape
    nc = t // tc
    if t_valid < t:
        assert nc == 1
    steps = tc if t_valid == t else t_valid
    lane_chunk = min(S5_SLAB_H, max(LANE, (8 * SUBLANE * LANE) // bt))
    kern = functools.partial(_s5_kernel, tc=tc, bt=bt, steps=steps, lane_chunk=lane_chunk)
    full = lambda shape: pl.BlockSpec(shape, lambda b, c: (0,) * len(shape))
    return pl.pallas_call(
        kern,
        grid=(bsz // bt, nc),
        in_specs=[
            pl.BlockSpec((bt, tc, S5_WIDTH), lambda b, c: (b, c, COL_U // S5_WIDTH)),
            full((S5_SLABS, S5_SLAB_U, 2 * S5_SLAB_H)),
            full((S5_SLABS, 2 * S5_SLAB_H, S5_SLAB_U)),
            full((1, 2 * S5_HW)),
            full((1, S5_WIDTH)),
            full((S5_WIDTH, 2 * S5_WIDTH)),
            full((1, 2 * S5_WIDTH)),
            pl.BlockSpec((None, bt, S5_HW), lambda b, c: (state_layer, b, 0)),
            pl.BlockSpec((None, bt, S5_HW), lambda b, c: (state_layer, b, 0)),
        ],
        out_specs=[
            pl.BlockSpec((bt, tc, S5_WIDTH), lambda b, c: (b, c, 0)),
            pl.BlockSpec((bt, S5_HW), lambda b, c: (b, 0)),
            pl.BlockSpec((bt, S5_HW), lambda b, c: (b, 0)),
        ],
        out_shape=[
            jax.ShapeDtypeStruct((bsz, t, S5_WIDTH), F32),
            jax.ShapeDtypeStruct((bsz, S5_HW), F32),
            jax.ShapeDtypeStruct((bsz, S5_HW), F32),
        ],
        scratch_shapes=[
            pltpu.VMEM((tc * bt, 2 * S5_HW), F32),
            pltpu.VMEM((bt, 2 * S5_HW), F32),
        ],
        compiler_params=_cparams(("parallel", "arbitrary")),
    )(z3, bmat, cmat, a_vec, d_vec, wglu, bglu, hre0, him0)


def _s5_params(a_re, a_im, b_re, b_im, c_re, c_im, log_dt):
    g = S5_GROUPS
    dt = jnp.exp(log_dt)[:, None]
    er = jnp.exp(a_re * dt)
    abr = er * jnp.cos(a_im * dt)
    abi = er * jnp.sin(a_im * dt)
    den = a_re * a_re + a_im * a_im
    xr = abr - 1.0
    cr = (xr * a_re + abi * a_im) / den
    ci = (abi * a_re - xr * a_im) / den
    bbr = cr[..., None] * b_re - ci[..., None] * b_im
    bbi = cr[..., None] * b_im + ci[..., None] * b_re
    sg = g // S5_SLABS
    eye = jnp.eye(sg, dtype=F32)
    slab = lambda x: x.reshape((S5_SLABS, sg) + x.shape[1:])
    bd_in = lambda x: jnp.einsum('sgnc,gh->sgchn', slab(x), eye).reshape(S5_SLABS, S5_SLAB_U, S5_SLAB_H)
    bd_out = lambda x: jnp.einsum('sgcn,gh->sgnhc', slab(x), eye).reshape(S5_SLABS, S5_SLAB_H, S5_SLAB_U)
    bmat = jnp.concatenate([bd_in(bbr), bd_in(bbi)], axis=2).astype(BF16)
    cmat = jnp.concatenate([bd_out(c_re), -bd_out(c_im)], axis=1).astype(BF16)
    lanes = lambda x: x.reshape(S5_SLABS, 1, S5_SLAB_H)
    a_vec = jnp.concatenate([lanes(abr), lanes(abi)], axis=2).reshape(1, 2 * S5_HW)
    return bmat, cmat, a_vec


def _ssd_kernel(xbc_ref, zz_ref, zdt_ref, cw_ref, cb_ref, dtb_ref, alog_ref, dsk_ref, nw_ref,
                conv0_ref, s0_ref, *rest, chunk, bt, t_valid, t_total):
    y_ref, s_ref, cn_ref, st_ref, tail_ref = rest[-5:]
    c = pl.program_id(1)
    nc = t_total // chunk
    L = chunk
    TAIL = SUBLANE
    last_valid = t_valid - (nc - 1) * L

    @pl.when(c == 0)
    def _():
        st_ref[...] = s0_ref[...]
        tail_ref[...] = jnp.zeros((bt, TAIL, M2_CONV_DIM), F32)
        tail_ref[:, TAIL - (M2_CONV - 1):TAIL, :] = conv0_ref[...]

    causal = lax.broadcasted_iota(jnp.int32, (L, L), 0) >= lax.broadcasted_iota(jnp.int32, (L, L), 1)
    lo_lane = lax.broadcasted_iota(jnp.int32, (L, LANE), 1) < M2_HEADDIM
    lo_row = lax.broadcasted_iota(jnp.int32, (LANE, LANE), 0) < M2_HEADDIM
    neg_a = -jnp.exp(alog_ref[...])
    nw = nw_ref[...]
    if t_valid < t_total:
        valid = (c * L + lax.broadcasted_iota(jnp.int32, (L, LANE), 0)) < t_valid

    def pair_cols(a, h0):
        return jnp.where(lo_lane, a[:, h0:h0 + 1], a[:, h0 + 1:h0 + 2])

    for bi in range(bt):
        xbc = xbc_ref[bi]
        ext = jnp.concatenate([tail_ref[bi], xbc], axis=0)
        conv = cb_ref[...]
        for j in range(M2_CONV):
            o0 = TAIL - (M2_CONV - 1) + j
            conv = conv + cw_ref[j:j + 1, :] * ext[o0:o0 + L, :]
        tail_ref[bi] = xbc[L - TAIL:L, :]

        @pl.when(c == nc - 1)
        def _(ext=ext, bi=bi):
            cn_ref[bi] = ext[TAIL + last_valid - (M2_CONV - 1):TAIL + last_valid, :]

        act = _silu(conv)
        xs = act[:, :M2_INNER]
        bm = act[:, M2_INNER:M2_INNER + M2_BC].astype(BF16)
        cm = act[:, M2_INNER + M2_BC:].astype(BF16)

        dtr = zdt_ref[bi] + dtb_ref[...]
        dt = jnp.maximum(dtr, 0.0) + jnp.log1p(jnp.exp(-jnp.abs(dtr)))
        if t_valid < t_total:
            dt = jnp.where(valid, dt, 0.0)
        acs = _cumsum_rows(dt * neg_a)
        acs_t = acs.T
        acs_end = acs[L - 1:L, :]
        e_acs = jnp.exp(acs)
        e_end = jnp.exp(acs_end - acs)
        e_tot = jnp.exp(acs_end)

        ys = []
        for g in range(M2_NGROUPS):
            bg = bm[:, g * M2_STATE:(g + 1) * M2_STATE]
            cg = cm[:, g * M2_STATE:(g + 1) * M2_STATE]
            cb = _dot_nt(cg, bg)
            for pr in range(2):
                p = g * 2 + pr
                h0 = 2 * p
                xp = xs[:, p * LANE:(p + 1) * LANE]
                xdt = xp * pair_cols(dt, h0)
                ydiag = None
                for hh in range(2):
                    h = h0 + hh
                    lm = jnp.where(causal, jnp.exp(jnp.minimum(acs[:, h:h + 1] - acs_t[h:h + 1, :], 0.0)), 0.0)
                    mk = lo_lane if hh == 0 else jnp.logical_not(lo_lane)
                    part = _dot((cb * lm).astype(BF16), jnp.where(mk, xdt, 0.0).astype(BF16))
                    ydiag = part if ydiag is None else ydiag + part
                sp = st_ref[bi, p]
                yoff = pair_cols(e_acs, h0) * _dot_nt(cg, sp.astype(BF16))
                ys.append(ydiag + yoff + dsk_ref[:, p * LANE:(p + 1) * LANE] * xp)
                xe = (xdt * pair_cols(e_end, h0)).astype(BF16)
                scale = jnp.where(lo_row, e_tot[:, h0:h0 + 1], e_tot[:, h0 + 1:h0 + 2])
                st_ref[bi, p] = scale * sp + _dot_tn(xe, bg)

        y = jnp.concatenate(ys, axis=1) * _silu(zz_ref[bi])
        outs = []
        for g in range(M2_NGROUPS):
            sl = slice(g * M2_NORM_W, (g + 1) * M2_NORM_W)
            yg = y[:, sl]
            outs.append(yg * _rms_scale(yg) * nw[:, sl])
        y_ref[bi] = jnp.concatenate(outs, axis=1)

    @pl.when(c == nc - 1)
    def _():
        s_ref[...] = st_ref[...]


def _ssd(z3, cw, cb, dtb, alog, dsk, nw, conv0, s0, prev, layer, state_layer, chunk, bt, t_valid):
    bsz, t, _ = z3.shape
    nc = t // chunk
    assert chunk >= SUBLANE and t_valid - (nc - 1) * chunk >= 1
    kern = functools.partial(_ssd_kernel, chunk=chunk, bt=bt, t_valid=t_valid, t_total=t)
    full = lambda shape: pl.BlockSpec(shape, lambda b, c: (0,) * len(shape))
    extra_specs, extra_args, aliases = [], [], {}
    if prev is not None:
        extra_specs = [pl.BlockSpec(memory_space=pl.ANY)] * 2
        extra_args = list(prev)
        aliases = {11: 1, 12: 2}
    return pl.pallas_call(
        kern,
        grid=(bsz // bt, nc),
        input_output_aliases=aliases,
        in_specs=[
            pl.BlockSpec((bt, chunk, M2_CONV_DIM), lambda b, c: (b, c, COL_XBC // M2_CONV_DIM)),
            pl.BlockSpec((bt, chunk, M2_INNER), lambda b, c: (b, c, COL_Z // M2_INNER)),
            pl.BlockSpec((bt, chunk, LANE), lambda b, c: (b, c, COL_DT // LANE)),
            full((M2_CONV, M2_CONV_DIM)),
            full((1, M2_CONV_DIM)),
            full((1, LANE)),
            full((1, LANE)),
            full((1, M2_INNER)),
            full((1, M2_INNER)),
            pl.BlockSpec((None, bt, M2_CONV - 1, M2_CONV_DIM), lambda b, c: (state_layer, b, 0, 0)),
            pl.BlockSpec((None, bt, M2_PAIRS, LANE, M2_STATE), lambda b, c: (state_layer, b, 0, 0, 0)),
        ] + extra_specs,
        out_specs=[
            pl.BlockSpec((bt, chunk, M2_INNER), lambda b, c: (b, c, 0)),
            pl.BlockSpec((None, bt, M2_PAIRS, LANE, M2_STATE), lambda b, c: (layer, b, 0, 0, 0)),
            pl.BlockSpec((None, bt, M2_CONV - 1, M2_CONV_DIM), lambda b, c: (layer, b, 0, 0)),
        ],
        out_shape=[
            jax.ShapeDtypeStruct((bsz, t, M2_INNER), F32),
            jax.ShapeDtypeStruct((DEPTH, bsz, M2_PAIRS, LANE, M2_STATE), F32),
            jax.ShapeDtypeStruct((DEPTH, bsz, M2_CONV - 1, M2_CONV_DIM), F32),
        ],
        scratch_shapes=[
            pltpu.VMEM((bt, M2_PAIRS, LANE, M2_STATE), F32),
            pltpu.VMEM((bt, SUBLANE, M2_CONV_DIM), F32),
        ],
        compiler_params=_cparams(("parallel", "arbitrary")),
    )(z3, z3, z3, cw, cb, dtb, alog, dsk, nw, conv0, s0, *extra_args)


def _merge_kernel(oa_ref, ob_ref, oc_ref, ga_ref, gb_ref, gc_ref, h_ref, wa_ref, wb_ref, wc_ref, wo_ref, o_ref):
    m = jax.nn.sigmoid(ga_ref[...]) * _dot(oa_ref[...].astype(BF16), wa_ref[...])
    m = m + jax.nn.sigmoid(gb_ref[...]) * _dot(ob_ref[...].astype(BF16), wb_ref[...])
    m = m + jax.nn.sigmoid(gc_ref[...]) * _dot(oc_ref[...].astype(BF16), wc_ref[...])
    o_ref[...] = h_ref[...] + _dot(m.astype(BF16), wo_ref[...])


def _merge(oa, ob, oc, z, h, wa, wb, wc, wo, layer):
    n = h.shape[0]
    tm = _row_tile(n, 512)
    gblk = COL_GATE // D_MODEL
    row = lambda w, j=0: pl.BlockSpec((tm, w), lambda i: (i, j))
    full = lambda shape: pl.BlockSpec((None,) + shape, lambda i: (layer,) + (0,) * len(shape))
    return pl.pallas_call(
        _merge_kernel,
        grid=(n // tm,),
        in_specs=[
            row(HG_W), row(S5_WIDTH), row(M2_INNER),
            row(D_MODEL, gblk), row(D_MODEL, gblk + 1), row(D_MODEL, gblk + 2),
            row(D_MODEL),
            full((HG_W, D_MODEL)), full((S5_WIDTH, D_MODEL)), full((M2_INNER, D_MODEL)),
            full((D_MODEL, D_MODEL)),
        ],
        out_specs=row(D_MODEL),
        out_shape=jax.ShapeDtypeStruct((n, D_MODEL), F32),
        compiler_params=_cparams(("parallel",)),
    )(oa, ob, oc, z, z, z, h, wa, wb, wc, wo)


R_E0 = E_GROUPS


def _moe_kernel(h_ref, nw_ref, wr_ref, br_ref, wg_ref, wu_ref, wd_ref, o_ref, xn_ref, comb_ref, acc_ref):
    e = pl.program_id(1)
    tm = h_ref.shape[0]
    lane = lax.broadcasted_iota(jnp.int32, (tm, LANE), 1)
    neg = -jnp.inf

    @pl.when(e == 0)
    def _():
        x = h_ref[...]
        xn = x * _rms_scale(x) * nw_ref[...]
        xn_ref[...] = xn.astype(BF16)
        logits = jnp.dot(xn, wr_ref[...], preferred_element_type=F32,
                         precision=lax.Precision.HIGHEST) + br_ref[...]
        gl = jnp.where(lane < E_GROUPS, logits, neg)
        gmax = jnp.max(gl, axis=1, keepdims=True)
        gidx = jnp.min(jnp.where(gl == gmax, lane, LANE), axis=1, keepdims=True)
        p_group = 1.0 / jnp.sum(jnp.exp(gl - gmax), axis=1, keepdims=True)
        in_grp = (lane >= R_E0) & (lane < R_E0 + N_EXPERTS) & (((lane - R_E0) >> 2) == gidx)
        ev = jnp.where(in_grp, logits, neg)
        v1 = jnp.max(ev, axis=1, keepdims=True)
        i1 = jnp.min(jnp.where(ev == v1, lane, LANE), axis=1, keepdims=True)
        ev2 = jnp.where(lane == i1, neg, ev)
        v2 = jnp.max(ev2, axis=1, keepdims=True)
        i2 = jnp.min(jnp.where(ev2 == v2, lane, LANE), axis=1, keepdims=True)
        e2 = jnp.exp(v2 - v1)
        w1 = 1.0 / (1.0 + e2)
        comb_ref[...] = jnp.where(lane == i1, w1 * p_group, jnp.where(lane == i2, e2 * w1 * p_group, 0.0))
        acc_ref[...] = jnp.zeros_like(acc_ref)

    xn = xn_ref[...]
    hid = _silu(_dot(xn, wg_ref[...].astype(BF16))) * _dot(xn, wu_ref[...].astype(BF16))
    ce = jnp.sum(jnp.where(lane == R_E0 + e, comb_ref[...], 0.0), axis=1, keepdims=True)
    acc_ref[...] += ce * _dot(hid.astype(BF16), wd_ref[...].astype(BF16))

    @pl.when(e == N_EXPERTS - 1)
    def _():
        o_ref[...] = h_ref[...] + acc_ref[...]


def _moe(h, nw, wr, br, wg, wu, wd, layer):
    n = h.shape[0]
    tm = _row_tile(n, 1024)
    return pl.pallas_call(
        _moe_kernel,
        grid=(n // tm, N_EXPERTS),
        in_specs=[
            pl.BlockSpec((tm, D_MODEL), lambda i, e: (i, 0)),
            pl.BlockSpec((1, D_MODEL), lambda i, e: (0, 0)),
            pl.BlockSpec((D_MODEL, LANE), lambda i, e: (0, 0)),
            pl.BlockSpec((1, LANE), lambda i, e: (0, 0)),
            pl.BlockSpec((None, None, D_MODEL, E_FF), lambda i, e: (layer, e, 0, 0)),
            pl.BlockSpec((None, None, D_MODEL, E_FF), lambda i, e: (layer, e, 0, 0)),
            pl.BlockSpec((None, None, E_FF, D_MODEL), lambda i, e: (layer, e, 0, 0)),
        ],
        out_specs=pl.BlockSpec((tm, D_MODEL), lambda i, e: (i, 0)),
        out_shape=jax.ShapeDtypeStruct((n, D_MODEL), F32),
        scratch_shapes=[
            pltpu.VMEM((tm, D_MODEL), BF16),
            pltpu.VMEM((tm, LANE), F32),
            pltpu.VMEM((tm, D_MODEL), F32),
        ],
        compiler_params=_cparams(("parallel", "arbitrary")),
    )(h, nw, wr, br, wg, wu, wd)


def _ple_kernel(h_ref, p_ref, nw_ref, wg_ref, wp_ref, nf_ref, o_ref, y_ref):
    x = h_ref[...]
    xn = (x * _rms_scale(x) * nw_ref[...]).astype(BF16)
    out = x + jax.nn.sigmoid(_dot(xn, wg_ref[...])) * _dot(p_ref[...].astype(BF16), wp_ref[...])
    o_ref[...] = out
    y_ref[...] = out * _rms_scale(out) * nf_ref[...]


def _ple(h, p, nw, wg, wp, nf, layer):
    n = h.shape[0]
    tm = _row_tile(n, 1024)
    row = lambda w: pl.BlockSpec((tm, w), lambda i: (i, 0))
    full = lambda shape: pl.BlockSpec(shape, lambda i: (0,) * len(shape))
    lfull = lambda shape: pl.BlockSpec((None,) + shape, lambda i: (layer,) + (0,) * len(shape))
    return pl.pallas_call(
        _ple_kernel,
        grid=(n // tm,),
        in_specs=[row(D_MODEL), pl.BlockSpec((None, tm, PLE_DIM), lambda i: (layer, i, 0)),
                  full((1, D_MODEL)), lfull((D_MODEL, D_MODEL)),
                  lfull((PLE_DIM, D_MODEL)), full((1, D_MODEL))],
        out_specs=[row(D_MODEL), row(D_MODEL)],
        out_shape=[jax.ShapeDtypeStruct((n, D_MODEL), F32), jax.ShapeDtypeStruct((n, D_MODEL), F32)],
        compiler_params=_cparams(("parallel",)),
    )(h, p, nw, wg, wp, nf)


def _prep_weights(w):
    win_r = _win_prep(w['w_in'])
    lane_pad = lambda a: jnp.pad(a, ((0, 0), (0, LANE - a.shape[1])))
    s5 = [_s5_params(w['s5_a_re'][i], w['s5_a_im'][i], w['s5_b_re'][i], w['s5_b_im'][i],
                     w['s5_c_re'][i], w['s5_c_im'][i], w['s5_log_dt'][i]) for i in range(DEPTH)]
    wr = jnp.concatenate([w['w_rg'], w['w_re'],
                          jnp.zeros((DEPTH, D_MODEL, LANE - E_GROUPS - N_EXPERTS), F32)], axis=2)
    br = jnp.concatenate([w['b_rg'], w['b_re'],
                          jnp.zeros((DEPTH, LANE - E_GROUPS - N_EXPERTS), F32)], axis=1)
    return dict(
        win=win_r,
        s5=s5,
        s5_d=w['s5_d'].reshape(DEPTH, 1, S5_WIDTH),
        wglu=w['s5_w_glu'].astype(BF16),
        bglu=w['s5_b_glu'].reshape(DEPTH, 1, 2 * S5_WIDTH),
        dtb=lane_pad(w['m2_dt_bias']).reshape(DEPTH, 1, LANE),
        alog=lane_pad(w['m2_a_log']).reshape(DEPTH, 1, LANE),
        dsk=jnp.repeat(w['m2_d'], M2_HEADDIM, axis=1).reshape(DEPTH, 1, M2_INNER),
        wa=w['w_br_hg'].astype(BF16), wb=w['w_br_s5'].astype(BF16), wc=w['w_br_m2'].astype(BF16),
        wo=w['w_out'].astype(BF16),
        wr=wr, br=br.reshape(DEPTH, 1, LANE),
        wpg=w['w_ple_gate'].astype(BF16), wpp=w['w_ple_proj'].astype(BF16),
    )


def _trunk(x, p, states, w, pw, t_valid, cfg):
    bsz, t, _ = x.shape
    n = bsz * t
    st_hg, st_re, st_im, st_ssm, st_conv = states
    per_layer = st_hg.shape[0] == DEPTH
    st_re = st_re.reshape(st_re.shape[0], bsz, S5_HW)
    st_im = st_im.reshape(st_im.shape[0], bsz, S5_HW)
    st_ssm = st_ssm.reshape(st_ssm.shape[0], bsz, M2_PAIRS, LANE, M2_STATE)
    h = x.reshape(n, D_MODEL)
    new = []
    y = None
    s_hg = None
    ssd_prev = None
    for i in range(DEPTH):
        sl = i if per_layer else 0
        z = _norm_matmul(h, w['norm_mix'][i].reshape(1, D_MODEL), pw['win'], i)
        z3 = z.reshape(bsz, t, Z_COLS)
        oa, s_hg = _hgrn(z3, w['hg_lb_logits'], w['hg_gnorm'][i].reshape(1, HG_DIM), st_hg, s_hg, i, sl,
                         cfg['hg_chunk'], cfg['hg_bt'], t_valid)
        bmat, cmat, a_vec = pw['s5'][i]
        ob, s_re, s_im = _s5(z3, bmat, cmat, a_vec, pw['s5_d'][i], pw['wglu'][i], pw['bglu'][i],
                             st_re, st_im, sl, cfg['s5_tc'], cfg['s5_bt'], t_valid)
        oc, s_ssm, conv_new = _ssd(z3, w['m2_conv_w'][i], w['m2_conv_b'][i].reshape(1, M2_CONV_DIM),
                                   pw['dtb'][i], pw['alog'][i], pw['dsk'][i],
                                   w['m2_norm'][i].reshape(1, M2_INNER), st_conv, st_ssm, ssd_prev, i, sl,
                                   cfg['m2_chunk'], cfg['m2_bt'], t_valid)
        ssd_prev = (s_ssm, conv_new)
        h = _merge(oa.reshape(n, HG_W), ob.reshape(n, S5_WIDTH), oc.reshape(n, M2_INNER), z, h,
                   pw['wa'], pw['wb'], pw['wc'], pw['wo'], i)
        h = _moe(h, w['norm_ffn'][i].reshape(1, D_MODEL), pw['wr'][i], pw['br'][i],
                 w['w_e_gate'], w['w_e_up'], w['w_e_down'], i)
        h, y = _ple(h, p.reshape(DEPTH, n, PLE_DIM), w['norm_ple'][i].reshape(1, D_MODEL),
                    pw['wpg'], pw['wpp'], w['norm_final'].reshape(1, D_MODEL), i)
        new.append((s_re.reshape(bsz, S5_GROUPS, S5_STATE), s_im.reshape(bsz, S5_GROUPS, S5_STATE)))
    s5_re, s5_im = (jnp.stack([nl[j] for nl in new]) for j in range(2))
    stacked = (s_hg, s5_re, s5_im, s_ssm.reshape(DEPTH, bsz, M2_HEADS, M2_HEADDIM, M2_STATE), conv_new)
    return y.reshape(bsz, t, D_MODEL), stacked


def _zero_states(bsz):
    return (jnp.zeros((1, bsz, HG_HEADS, HG_DIM, HG_DIM), F32),
            jnp.zeros((1, bsz, S5_GROUPS, S5_STATE), F32),
            jnp.zeros((1, bsz, S5_GROUPS, S5_STATE), F32),
            jnp.zeros((1, bsz, M2_HEADS, M2_HEADDIM, M2_STATE), F32),
            jnp.zeros((1, bsz, M2_CONV - 1, M2_CONV_DIM), F32))


def _pad_time(a, axis, t_pad):
    t = a.shape[axis]
    if t == t_pad:
        return a
    widths = [(0, 0)] * a.ndim
    widths[axis] = (0, t_pad - t)
    return jnp.pad(a, widths)


def kernel(x_prompt, x_sample, state_hgrn, state_s5_re, state_s5_im, state_ssm, state_conv,
           p_prompt, p_sample,
           norm_mix, w_in, hg_lb_logits, hg_gnorm, w_br_hg,
           s5_a_re, s5_a_im, s5_b_re, s5_b_im, s5_c_re, s5_c_im, s5_d, s5_log_dt, s5_w_glu, s5_b_glu, w_br_s5,
           m2_conv_w, m2_conv_b, m2_dt_bias, m2_a_log, m2_d, m2_norm, w_br_m2,
           w_out,
           norm_ffn, w_rg, b_rg, w_re, b_re, w_e_gate, w_e_up, w_e_down,
           norm_ple, w_ple_gate, w_ple_proj,
           norm_final):
    w = dict(norm_mix=norm_mix, w_in=w_in, hg_lb_logits=hg_lb_logits, hg_gnorm=hg_gnorm, w_br_hg=w_br_hg,
             s5_a_re=s5_a_re, s5_a_im=s5_a_im, s5_b_re=s5_b_re, s5_b_im=s5_b_im, s5_c_re=s5_c_re,
             s5_c_im=s5_c_im, s5_d=s5_d, s5_log_dt=s5_log_dt, s5_w_glu=s5_w_glu, s5_b_glu=s5_b_glu,
             w_br_s5=w_br_s5, m2_conv_w=m2_conv_w, m2_conv_b=m2_conv_b, m2_dt_bias=m2_dt_bias,
             m2_a_log=m2_a_log, m2_d=m2_d, m2_norm=m2_norm, w_br_m2=w_br_m2, w_out=w_out,
             norm_ffn=norm_ffn, w_rg=w_rg, b_rg=b_rg, w_re=w_re, b_re=b_re, w_e_gate=w_e_gate,
             w_e_up=w_e_up, w_e_down=w_e_down, norm_ple=norm_ple, w_ple_gate=w_ple_gate,
             w_ple_proj=w_ple_proj, norm_final=norm_final)
    pw = _prep_weights(w)

    bp, tp, _ = x_prompt.shape
    cfg_p = dict(hg_chunk=min(128, tp), hg_bt=min(2, bp), s5_tc=min(64, tp), s5_bt=SUBLANE,
                 m2_chunk=min(128, tp), m2_bt=1)
    y_p, st_p = _trunk(x_prompt, p_prompt, _zero_states(bp), w, pw, tp, cfg_p)

    bs, ts, _ = x_sample.shape
    ts_pad = -(-ts // SUBLANE) * SUBLANE
    cfg_s = dict(hg_chunk=ts_pad, hg_bt=min(8, bs), s5_tc=ts_pad, s5_bt=min(32, bs),
                 m2_chunk=ts_pad, m2_bt=min(4, bs))
    y_s, st_s = _trunk(_pad_time(x_sample, 1, ts_pad), _pad_time(p_sample, 2, ts_pad),
                       (state_hgrn, state_s5_re, state_s5_im, state_ssm, state_conv), w, pw, ts, cfg_s)
    return (y_p, y_s[:, :ts]) + st_p + st_s
```

```python
import functools

import jax
import jax.numpy as jnp
from jax import lax
from jax.experimental import pallas as pl
from jax.experimental.pallas import tpu as pltpu

F32 = jnp.float32
BF16 = jnp.bfloat16

D_MODEL = 1024
DEPTH = 2
PLE_DIM = 256
NORM_EPS = 1e-6

HG_HEADS = 4
HG_DIM = 128
HG_W = HG_HEADS * HG_DIM

S5_WIDTH = 512
S5_GROUP = 16
S5_GROUPS = S5_WIDTH // S5_GROUP
S5_STATE = 64
S5_HW = S5_GROUPS * S5_STATE
S5_SLABS = 2
S5_SLAB_U = S5_WIDTH // S5_SLABS
S5_SLAB_H = S5_HW // S5_SLABS

M2_INNER = 1024
M2_HEADDIM = 64
M2_HEADS = M2_INNER // M2_HEADDIM
M2_NGROUPS = 4
M2_STATE = 128
M2_CONV = 4
M2_BC = M2_NGROUPS * M2_STATE
M2_CONV_DIM = M2_INNER + 2 * M2_BC
M2_PAIRS = M2_HEADS // 2
M2_NORM_W = M2_INNER // M2_NGROUPS

N_BRANCH = 3
E_GROUPS = 4
E_PER_GROUP = 4
N_EXPERTS = E_GROUPS * E_PER_GROUP
E_FF = 256

LANE = 128
SUBLANE = 8

COL_XBC = 0
COL_Z = COL_XBC + M2_CONV_DIM
COL_GATE = COL_Z + M2_INNER
COL_Q = COL_GATE + N_BRANCH * D_MODEL
COL_F = COL_Q + HG_W
COL_I = COL_F + HG_W
COL_G = COL_I + HG_W
COL_U = COL_G + HG_W
COL_DT = COL_U + S5_WIDTH
Z_TN = 1280
Z_COLS = 7 * Z_TN
assert COL_DT + LANE <= Z_COLS and COL_DT % LANE == 0

_O_Q, _O_F, _O_I, _O_G, _O_U = 0, 512, 1024, 1536, 2048
_O_Z = 2560
_O_XBC = _O_Z + M2_INNER
_O_DT = _O_XBC + M2_CONV_DIM
_O_GATE = _O_DT + M2_HEADS
IN_COLS = _O_GATE + N_BRANCH * D_MODEL

VMEM_LIMIT = 56 * 1024 * 1024


def _cparams(sem):
    return pltpu.CompilerParams(dimension_semantics=sem, vmem_limit_bytes=VMEM_LIMIT)


def _rms_scale(x):
    return lax.rsqrt(jnp.mean(x * x, axis=-1, keepdims=True) + NORM_EPS)


def _silu(x):
    return x * jax.nn.sigmoid(x)


def _dot(a, b):
    return jnp.dot(a, b, preferred_element_type=F32)


def _dot_nt(a, b):
    return lax.dot_general(a, b, (((1,), (1,)), ((), ())), preferred_element_type=F32)


def _dot_tn(a, b):
    return lax.dot_general(a, b, (((0,), (0,)), ((), ())), preferred_element_type=F32)


def _cumsum_rows(x):
    n = x.shape[0]
    row = lax.broadcasted_iota(jnp.int32, x.shape, 0)
    s = 1
    while s < n:
        x = x + jnp.where(row >= s, pltpu.roll(x, s, 0), 0.0)
        s *= 2
    return x


def _row_tile(n, pref):
    t = min(pref, n)
    assert n % t == 0
    return t


def _win_prep_kernel(w_ref, o_ref):
    w = w_ref[...]
    rows = w.shape[0]
    pieces = [
        w[:, _O_XBC:_O_XBC + M2_CONV_DIM], w[:, _O_Z:_O_Z + M2_INNER], w[:, _O_GATE:IN_COLS],
        w[:, :_O_Z], w[:, _O_DT:_O_DT + M2_HEADS], jnp.zeros((rows, Z_COLS - IN_COLS), F32)]
    o_ref[...] = jnp.concatenate(pieces, axis=1).astype(BF16)


def _win_prep(w_in):
    tr = 256
    return pl.pallas_call(
        _win_prep_kernel,
        grid=(DEPTH, D_MODEL // tr),
        in_specs=[pl.BlockSpec((None, tr, IN_COLS), lambda l, i: (l, i, 0))],
        out_specs=pl.BlockSpec((None, tr, Z_COLS), lambda l, i: (l, i, 0)),
        out_shape=jax.ShapeDtypeStruct((DEPTH, D_MODEL, Z_COLS), BF16),
        compiler_params=_cparams(("parallel", "parallel")),
    )(w_in)


def _norm_matmul_kernel(x_ref, nw_ref, w_ref, o_ref, xn_ref):
    @pl.when(pl.program_id(1) == 0)
    def _():
        x = x_ref[...]
        xn_ref[...] = (x * _rms_scale(x) * nw_ref[...]).astype(BF16)

    o_ref[...] = _dot(xn_ref[...], w_ref[...])


def _norm_matmul(h, nw, w, layer):
    n = h.shape[0]
    cols = w.shape[2]
    tm = _row_tile(n, 1024)
    tn = Z_TN
    return pl.pallas_call(
        _norm_matmul_kernel,
        grid=(n // tm, cols // tn),
        in_specs=[
            pl.BlockSpec((tm, D_MODEL), lambda i, j: (i, 0)),
            pl.BlockSpec((1, D_MODEL), lambda i, j: (0, 0)),
            pl.BlockSpec((None, D_MODEL, tn), lambda i, j: (layer, 0, j)),
        ],
        out_specs=pl.BlockSpec((tm, tn), lambda i, j: (i, j)),
        out_shape=jax.ShapeDtypeStruct((n, cols), F32),
        scratch_shapes=[pltpu.VMEM((tm, D_MODEL), BF16)],
        compiler_params=_cparams(("parallel", "arbitrary")),
    )(h, nw, w)


def _hgrn_chunk(q, k, v, logf, st, lev, dcode, rows_used):
    C = q.shape[0]
    b = _cumsum_rows(logf)
    b_end = b[C - 1:C, :]
    o = _dot((q * jnp.exp(b)).astype(BF16), st.astype(BF16))
    v16 = v.astype(BF16)

    def shifted_score(j):
        if j == 0:
            return jnp.sum(q * k, axis=1, keepdims=True)
        kr = pltpu.roll(k, j, 0)
        br = pltpu.roll(b, j, 0)
        return jnp.sum(q * kr * jnp.exp(jnp.minimum(b - br, 0.0)), axis=1, keepdims=True)

    if C == SUBLANE:
        row = lax.broadcasted_iota(jnp.int32, (C, HG_DIM), 0)
        o = o + shifted_score(0) * v
        for j in range(1, min(SUBLANE, rows_used)):
            o = o + jnp.where(row >= j, shifted_score(j), 0.0) * pltpu.roll(v, j, 0)
        ke = k * jnp.exp(b_end - b)
        decay = jnp.transpose(jnp.broadcast_to(jnp.exp(b_end), (SUBLANE, HG_DIM)))[:, 0:1]
        return o, st * decay + _dot_tn(ke.astype(BF16), v16)

    sc = jnp.where(dcode == 0, shifted_score(0), 0.0)
    for j in range(1, SUBLANE):
        sc = jnp.where(dcode == j, shifted_score(j), sc)

    m = C // 2
    while m >= SUBLANE:
        parts = []
        for p0 in range(0, C, 2 * m):
            parts.append(jnp.broadcast_to(b[p0 + m - 1:p0 + m, :], (2 * m, HG_DIM)))
        r = parts[0] if len(parts) == 1 else jnp.concatenate(parts, axis=0)
        qe = q * jnp.exp(jnp.minimum(b - r, 0.0))
        ke = k * jnp.exp(jnp.minimum(r - b, 0.0))
        pm = _dot_nt(qe.astype(BF16), ke.astype(BF16))
        sc = jnp.where((lev >> (m.bit_length() - 1)) == 1, pm, sc)
        m //= 2

    o = o + _dot(sc.astype(BF16), v16)
    ke = k * jnp.exp(b_end - b)
    decay = jnp.transpose(jnp.broadcast_to(jnp.exp(b_end), (SUBLANE, HG_DIM)))[:, 0:1]
    st_new = st * decay + _dot_tn(ke.astype(BF16), v16)
    return o, st_new


def _hgrn_kernel(zq_ref, zf_ref, zi_ref, zg_ref, lbl_ref, gw_ref, s0_ref, *rest,
                 layer, chunk, bt, t_valid, t_total):
    o_ref, s_ref, st_ref = rest[-3:]
    c = pl.program_id(1)
    C = chunk

    @pl.when(c == 0)
    def _():
        st_ref[...] = s0_ref[...]

    lg = lbl_ref[...]
    e = jnp.exp(lg - jnp.max(lg, axis=0, keepdims=True))
    prob = e / jnp.sum(e, axis=0, keepdims=True)
    lb_all = jnp.zeros((1, HG_W), F32)
    for j in range(1, layer + 1):
        lb_all = lb_all + prob[j:j + 1]

    rr = lax.broadcasted_iota(jnp.int32, (C, C), 0)
    cc = lax.broadcasted_iota(jnp.int32, (C, C), 1)
    xr = rr ^ cc
    lev = jnp.where(rr > cc, xr, -1)
    dcode = jnp.where(xr < SUBLANE, rr - cc, -1)
    if t_valid < t_total:
        valid = (c * C + lax.broadcasted_iota(jnp.int32, (C, HG_DIM), 0)) < t_valid
    gw = gw_ref[...]

    for bi in range(bt):
        for h in range(HG_HEADS):
            sl = slice(h * HG_DIM, (h + 1) * HG_DIM)
            lb = lb_all[:, sl]
            zf = zf_ref[bi, :, sl]
            q = _silu(zq_ref[bi, :, sl])
            logf = jnp.log(lb + (1.0 - lb) * jax.nn.sigmoid(zf))
            k = (1.0 - lb) * jax.nn.sigmoid(-zf)
            if t_valid < t_total:
                logf = jnp.where(valid, logf, 0.0)
                k = jnp.where(valid, k, 0.0)
            o, st_new = _hgrn_chunk(q, k, zi_ref[bi, :, sl], logf, st_ref[bi, h], lev, dcode,
                                    min(C, t_valid))
            o_ref[bi, :, sl] = o * _rms_scale(o) * gw * _silu(zg_ref[bi, :, sl])
            st_ref[bi, h] = st_new

    @pl.when(c == pl.num_programs(1) - 1)
    def _():
        s_ref[...] = st_ref[...]


def _hgrn(z3, lb_logits, gw, s0, prev, layer, state_layer, chunk, bt, t_valid):
    bsz, t, _ = z3.shape
    nc = t // chunk
    kern = functools.partial(_hgrn_kernel, layer=layer, chunk=chunk, bt=bt, t_valid=t_valid, t_total=t)

    def zspec(col):
        return pl.BlockSpec((bt, chunk, HG_W), lambda b, c: (b, c, col // HG_W))

    in_specs = [
        zspec(COL_Q), zspec(COL_F), zspec(COL_I), zspec(COL_G),
        pl.BlockSpec((DEPTH, HG_W), lambda b, c: (0, 0)),
        pl.BlockSpec((1, HG_DIM), lambda b, c: (0, 0)),
        pl.BlockSpec((None, bt, HG_HEADS, HG_DIM, HG_DIM), lambda b, c: (state_layer, b, 0, 0, 0)),
    ]
    args = [z3, z3, z3, z3, lb_logits, gw, s0]
    aliases = {}
    if prev is not None:
        in_specs.append(pl.BlockSpec(memory_space=pl.ANY))
        args.append(prev)
        aliases = {len(args) - 1: 1}
    return pl.pallas_call(
        kern,
        grid=(bsz // bt, nc),
        in_specs=in_specs,
        out_specs=[
            pl.BlockSpec((bt, chunk, HG_W), lambda b, c: (b, c, 0)),
            pl.BlockSpec((None, bt, HG_HEADS, HG_DIM, HG_DIM), lambda b, c: (layer, b, 0, 0, 0)),
        ],
        out_shape=[
            jax.ShapeDtypeStruct((bsz, t, HG_W), F32),
            jax.ShapeDtypeStruct((DEPTH, bsz, HG_HEADS, HG_DIM, HG_DIM), F32),
        ],
        scratch_shapes=[pltpu.VMEM((bt, HG_HEADS, HG_DIM, HG_DIM), F32)],
        input_output_aliases=aliases,
        compiler_params=_cparams(("parallel", "arbitrary")),
    )(*args)


def _s5_kernel(u_ref, bmat_ref, cmat_ref, a_ref, d_ref, wglu_ref, bglu_ref, hre0_ref, him0_ref,
               o_ref, hre_ref, him_ref, hs_ref, hst_ref, *, tc, bt, steps, lane_chunk):
    c = pl.program_id(1)

    def re_off(s):
        return 2 * s * S5_SLAB_H

    def im_off(s):
        return (2 * s + 1) * S5_SLAB_H

    @pl.when(c == 0)
    def _():
        for s in range(S5_SLABS):
            src = slice(s * S5_SLAB_H, (s + 1) * S5_SLAB_H)
            hst_ref[:, re_off(s):re_off(s) + S5_SLAB_H] = hre0_ref[:, src]
            hst_ref[:, im_off(s):im_off(s) + S5_SLAB_H] = him0_ref[:, src]

    u = pltpu.einshape("btd->tbd", u_ref[...]).reshape(tc * bt, S5_WIDTH)
    u16 = u.astype(BF16)
    for s in range(S5_SLABS):
        hs_ref[:, re_off(s):re_off(s + 1)] = _dot(u16[:, s * S5_SLAB_U:(s + 1) * S5_SLAB_U], bmat_ref[s])

    for s in range(S5_SLABS):
        for l0 in range(0, S5_SLAB_H, lane_chunk):
            re_sl = slice(re_off(s) + l0, re_off(s) + l0 + lane_chunk)
            im_sl = slice(im_off(s) + l0, im_off(s) + l0 + lane_chunk)
            are = a_ref[:, re_sl]
            aim = a_ref[:, im_sl]

            def step(t, carry, re_sl=re_sl, im_sl=im_sl, are=are, aim=aim):
                hre, him = carry
                r0 = pl.multiple_of(t * bt, bt)
                nre = are * hre - aim * him + hs_ref[pl.ds(r0, bt), re_sl]
                nim = are * him + aim * hre + hs_ref[pl.ds(r0, bt), im_sl]
                hs_ref[pl.ds(r0, bt), re_sl] = nre
                hs_ref[pl.ds(r0, bt), im_sl] = nim
                return nre, nim

            hre, him = lax.fori_loop(0, steps, step, (hst_ref[:, re_sl], hst_ref[:, im_sl]), unroll=True)
            hst_ref[:, re_sl] = hre
            hst_ref[:, im_sl] = him

    ys = [_dot(hs_ref[:, re_off(s):re_off(s + 1)].astype(BF16), cmat_ref[s]) for s in range(S5_SLABS)]
    y = jnp.concatenate(ys, axis=1) + d_ref[...] * u
    g = jax.nn.gelu(y)
    gl = _dot(g.astype(BF16), wglu_ref[...]) + bglu_ref[...]
    o = gl[:, :S5_WIDTH] * jax.nn.sigmoid(gl[:, S5_WIDTH:])
    o_ref[...] = pltpu.einshape("tbd->btd", o.reshape(tc, bt, S5_WIDTH))

    @pl.when(c == pl.num_programs(1) - 1)
    def _():
        for s in range(S5_SLABS):
            dst = slice(s * S5_SLAB_H, (s + 1) * S5_SLAB_H)
            hre_ref[:, dst] = hst_ref[:, re_off(s):re_off(s) + S5_SLAB_H]
            him_ref[:, dst] = hst_ref[:, im_off(s):im_off(s) + S5_SLAB_H]


def _s5(z3, bmat, cmat, a_vec, d_vec, wglu, bglu, hre0, him0, state_layer, tc, bt, t_valid):
    bsz, t, _ = z3.shape
    nc = t // tc
    if t_valid < t:
        assert nc == 1
    steps = tc if t_valid == t else t_valid
    lane_chunk = min(S5_SLAB_H, max(LANE, (8 * SUBLANE * LANE) // bt))
    kern = functools.partial(_s5_kernel, tc=tc, bt=bt, steps=steps, lane_chunk=lane_chunk)
    full = lambda shape: pl.BlockSpec(shape, lambda b, c: (0,) * len(shape))
    return pl.pallas_call(
        kern,
        grid=(bsz // bt, nc),
        in_specs=[
            pl.BlockSpec((bt, tc, S5_WIDTH), lambda b, c: (b, c, COL_U // S5_WIDTH)),
            full((S5_SLABS, S5_SLAB_U, 2 * S5_SLAB_H)),
            full((S5_SLABS, 2 * S5_SLAB_H, S5_SLAB_U)),
            full((1, 2 * S5_HW)),
            full((1, S5_WIDTH)),
            full((S5_WIDTH, 2 * S5_WIDTH)),
            full((1, 2 * S5_WIDTH)),
            pl.BlockSpec((None, bt, S5_HW), lambda b, c: (state_layer, b, 0)),
            pl.BlockSpec((None, bt, S5_HW), lambda b, c: (state_layer, b, 0)),
        ],
        out_specs=[
            pl.BlockSpec((bt, tc, S5_WIDTH), lambda b, c: (b, c, 0)),
            pl.BlockSpec((bt, S5_HW), lambda b, c: (b, 0)),
            pl.BlockSpec((bt, S5_HW), lambda b, c: (b, 0)),
        ],
        out_shape=[
            jax.ShapeDtypeStruct((bsz, t, S5_WIDTH), F32),
            jax.ShapeDtypeStruct((bsz, S5_HW), F32),
            jax.ShapeDtypeStruct((bsz, S5_HW), F32),
        ],
        scratch_shapes=[
            pltpu.VMEM((tc * bt, 2 * S5_HW), F32),
            pltpu.VMEM((bt, 2 * S5_HW), F32),
        ],
        compiler_params=_cparams(("parallel", "arbitrary")),
    )(z3, bmat, cmat, a_vec, d_vec, wglu, bglu, hre0, him0)


def _s5_params(a_re, a_im, b_re, b_im, c_re, c_im, log_dt):
    g = S5_GROUPS
    dt = jnp.exp(log_dt)[:, None]
    er = jnp.exp(a_re * dt)
    abr = er * jnp.cos(a_im * dt)
    abi = er * jnp.sin(a_im * dt)
    den = a_re * a_re + a_im * a_im
    xr = abr - 1.0
    cr = (xr * a_re + abi * a_im) / den
    ci = (abi * a_re - xr * a_im) / den
    bbr = cr[..., None] * b_re - ci[..., None] * b_im
    bbi = cr[..., None] * b_im + ci[..., None] * b_re
    sg = g // S5_SLABS
    eye = jnp.eye(sg, dtype=F32)
    slab = lambda x: x.reshape((S5_SLABS, sg) + x.shape[1:])
    bd_in = lambda x: jnp.einsum('sgnc,gh->sgchn', slab(x), eye).reshape(S5_SLABS, S5_SLAB_U, S5_SLAB_H)
    bd_out = lambda x: jnp.einsum('sgcn,gh->sgnhc', slab(x), eye).reshape(S5_SLABS, S5_SLAB_H, S5_SLAB_U)
    bmat = jnp.concatenate([bd_in(bbr), bd_in(bbi)], axis=2).astype(BF16)
    cmat = jnp.concatenate([bd_out(c_re), -bd_out(c_im)], axis=1).astype(BF16)
    lanes = lambda x: x.reshape(S5_SLABS, 1, S5_SLAB_H)
    a_vec = jnp.concatenate([lanes(abr), lanes(abi)], axis=2).reshape(1, 2 * S5_HW)
    return bmat, cmat, a_vec


def _ssd_kernel(xbc_ref, zz_ref, zdt_ref, cw_ref, cb_ref, dtb_ref, alog_ref, dsk_ref, nw_ref,
                conv0_ref, s0_ref, *rest, chunk, bt, t_valid, t_total):
    y_ref, s_ref, cn_ref, st_ref, tail_ref = rest[-5:]
    c = pl.program_id(1)
    nc = t_total // chunk
    L = chunk
    TAIL = SUBLANE
    last_valid = t_valid - (nc - 1) * L

    @pl.when(c == 0)
    def _():
        st_ref[...] = s0_ref[...]
        tail_ref[...] = jnp.zeros((bt, TAIL, M2_CONV_DIM), F32)
        tail_ref[:, TAIL - (M2_CONV - 1):TAIL, :] = conv0_ref[...]

    causal = lax.broadcasted_iota(jnp.int32, (L, L), 0) >= lax.broadcasted_iota(jnp.int32, (L, L), 1)
    lo_lane = lax.broadcasted_iota(jnp.int32, (L, LANE), 1) < M2_HEADDIM
    lo_row = lax.broadcasted_iota(jnp.int32, (LANE, LANE), 0) < M2_HEADDIM
    neg_a = -jnp.exp(alog_ref[...])
    nw = nw_ref[...]
    if t_valid < t_total:
        valid = (c * L + lax.broadcasted_iota(jnp.int32, (L, LANE), 0)) < t_valid

    def pair_cols(a, h0):
        return jnp.where(lo_lane, a[:, h0:h0 + 1], a[:, h0 + 1:h0 + 2])

    for bi in range(bt):
        xbc = xbc_ref[bi]
        ext = jnp.concatenate([tail_ref[bi], xbc], axis=0)
        conv = cb_ref[...]
        for j in range(M2_CONV):
            o0 = TAIL - (M2_CONV - 1) + j
            conv = conv + cw_ref[j:j + 1, :] * ext[o0:o0 + L, :]
        tail_ref[bi] = xbc[L - TAIL:L, :]

        @pl.when(c == nc - 1)
        def _(ext=ext, bi=bi):
            cn_ref[bi] = ext[TAIL + last_valid - (M2_CONV - 1):TAIL + last_valid, :]

        act = _silu(conv)
        xs = act[:, :M2_INNER]
        bm = act[:, M2_INNER:M2_INNER + M2_BC].astype(BF16)
        cm = act[:, M2_INNER + M2_BC:].astype(BF16)

        dtr = zdt_ref[bi] + dtb_ref[...]
        dt = jnp.maximum(dtr, 0.0) + jnp.log1p(jnp.exp(-jnp.abs(dtr)))
        if t_valid < t_total:
            dt = jnp.where(valid, dt, 0.0)
        acs = _cumsum_rows(dt * neg_a)
        acs_t = acs.T
        acs_end = acs[L - 1:L, :]
        e_acs = jnp.exp(acs)
        e_end = jnp.exp(acs_end - acs)
        e_tot = jnp.exp(acs_end)

        ys = []
        for g in range(M2_NGROUPS):
            bg = bm[:, g * M2_STATE:(g + 1) * M2_STATE]
            cg = cm[:, g * M2_STATE:(g + 1) * M2_STATE]
            cb = _dot_nt(cg, bg)
            for pr in range(2):
                p = g * 2 + pr
                h0 = 2 * p
                xp = xs[:, p * LANE:(p + 1) * LANE]
                xdt = xp * pair_cols(dt, h0)
                ydiag = None
                for hh in range(2):
                    h = h0 + hh
                    lm = jnp.where(causal, jnp.exp(jnp.minimum(acs[:, h:h + 1] - acs_t[h:h + 1, :], 0.0)), 0.0)
                    mk = lo_lane if hh == 0 else jnp.logical_not(lo_lane)
                    part = _dot((cb * lm).astype(BF16), jnp.where(mk, xdt, 0.0).astype(BF16))
                    ydiag = part if ydiag is None else ydiag + part
                sp = st_ref[bi, p]
                yoff = pair_cols(e_acs, h0) * _dot_nt(cg, sp.astype(BF16))
                ys.append(ydiag + yoff + dsk_ref[:, p * LANE:(p + 1) * LANE] * xp)
                xe = (xdt * pair_cols(e_end, h0)).astype(BF16)
                scale = jnp.where(lo_row, e_tot[:, h0:h0 + 1], e_tot[:, h0 + 1:h0 + 2])
                st_ref[bi, p] = scale * sp + _dot_tn(xe, bg)

        y = jnp.concatenate(ys, axis=1) * _silu(zz_ref[bi])
        outs = []
        for g in range(M2_NGROUPS):
            sl = slice(g * M2_NORM_W, (g + 1) * M2_NORM_W)
            yg = y[:, sl]
            outs.append(yg * _rms_scale(yg) * nw[:, sl])
        y_ref[bi] = jnp.concatenate(outs, axis=1)

    @pl.when(c == nc - 1)
    def _():
        s_ref[...] = st_ref[...]


def _ssd(z3, cw, cb, dtb, alog, dsk, nw, conv0, s0, prev, layer, state_layer, chunk, bt, t_valid):
    bsz, t, _ = z3.shape
    nc = t // chunk
    assert chunk >= SUBLANE and t_valid - (nc - 1) * chunk >= 1
    kern = functools.partial(_ssd_kernel, chunk=chunk, bt=bt, t_valid=t_valid, t_total=t)
    full = lambda shape: pl.BlockSpec(shape, lambda b, c: (0,) * len(shape))
    extra_specs, extra_args, aliases = [], [], {}
    if prev is not None:
        extra_specs = [pl.BlockSpec(memory_space=pl.ANY)] * 2
        extra_args = list(prev)
        aliases = {11: 1, 12: 2}
    return pl.pallas_call(
        kern,
        grid=(bsz // bt, nc),
        input_output_aliases=aliases,
        in_specs=[
            pl.BlockSpec((bt, chunk, M2_CONV_DIM), lambda b, c: (b, c, COL_XBC // M2_CONV_DIM)),
            pl.BlockSpec((bt, chunk, M2_INNER), lambda b, c: (b, c, COL_Z // M2_INNER)),
            pl.BlockSpec((bt, chunk, LANE), lambda b, c: (b, c, COL_DT // LANE)),
            full((M2_CONV, M2_CONV_DIM)),
            full((1, M2_CONV_DIM)),
            full((1, LANE)),
            full((1, LANE)),
            full((1, M2_INNER)),
            full((1, M2_INNER)),
            pl.BlockSpec((None, bt, M2_CONV - 1, M2_CONV_DIM), lambda b, c: (state_layer, b, 0, 0)),
            pl.BlockSpec((None, bt, M2_PAIRS, LANE, M2_STATE), lambda b, c: (state_layer, b, 0, 0, 0)),
        ] + extra_specs,
        out_specs=[
            pl.BlockSpec((bt, chunk, M2_INNER), lambda b, c: (b, c, 0)),
            pl.BlockSpec((None, bt, M2_PAIRS, LANE, M2_STATE), lambda b, c: (layer, b, 0, 0, 0)),
            pl.BlockSpec((None, bt, M2_CONV - 1, M2_CONV_DIM), lambda b, c: (layer, b, 0, 0)),
        ],
        out_shape=[
            jax.ShapeDtypeStruct((bsz, t, M2_INNER), F32),
            jax.ShapeDtypeStruct((DEPTH, bsz, M2_PAIRS, LANE, M2_STATE), F32),
            jax.ShapeDtypeStruct((DEPTH, bsz, M2_CONV - 1, M2_CONV_DIM), F32),
        ],
        scratch_shapes=[
            pltpu.VMEM((bt, M2_PAIRS, LANE, M2_STATE), F32),
            pltpu.VMEM((bt, SUBLANE, M2_CONV_DIM), F32),
        ],
        compiler_params=_cparams(("parallel", "arbitrary")),
    )(z3, z3, z3, cw, cb, dtb, alog, dsk, nw, conv0, s0, *extra_args)


def _merge_kernel(oa_ref, ob_ref, oc_ref, ga_ref, gb_ref, gc_ref, h_ref, wa_ref, wb_ref, wc_ref, wo_ref, o_ref):
    m = jax.nn.sigmoid(ga_ref[...]) * _dot(oa_ref[...].astype(BF16), wa_ref[...])
    m = m + jax.nn.sigmoid(gb_ref[...]) * _dot(ob_ref[...].astype(BF16), wb_ref[...])
    m = m + jax.nn.sigmoid(gc_ref[...]) * _dot(oc_ref[...].astype(BF16), wc_ref[...])
    o_ref[...] = h_ref[...] + _dot(m.astype(BF16), wo_ref[...])


def _merge(oa, ob, oc, z, h, wa, wb, wc, wo, layer):
    n = h.shape[0]
    tm = _row_tile(n, 512)
    gblk = COL_GATE // D_MODEL
    row = lambda w, j=0: pl.BlockSpec((tm, w), lambda i: (i, j))
    full = lambda shape: pl.BlockSpec((None,) + shape, lambda i: (layer,) + (0,) * len(shape))
    return pl.pallas_call(
        _merge_kernel,
        grid=(n // tm,),
        in_specs=[
            row(HG_W), row(S5_WIDTH), row(M2_INNER),
            row(D_MODEL, gblk), row(D_MODEL, gblk + 1), row(D_MODEL, gblk + 2),
            row(D_MODEL),
            full((HG_W, D_MODEL)), full((S5_WIDTH, D_MODEL)), full((M2_INNER, D_MODEL)),
            full((D_MODEL, D_MODEL)),
        ],
        out_specs=row(D_MODEL),
        out_shape=jax.ShapeDtypeStruct((n, D_MODEL), F32),
        compiler_params=_cparams(("parallel",)),
    )(oa, ob, oc, z, z, z, h, wa, wb, wc, wo)


R_E0 = E_GROUPS
MOE_EPS = 2


def _moe_kernel(h_ref, nw_ref, wr_ref, br_ref, wg_ref, wu_ref, wd_ref, o_ref, xn_ref, comb_ref, acc_ref):
    e = pl.program_id(1)
    tm = h_ref.shape[0]
    lane = lax.broadcasted_iota(jnp.int32, (tm, LANE), 1)
    neg = -jnp.inf

    @pl.when(e == 0)
    def _():
        x = h_ref[...]
        xn = x * _rms_scale(x) * nw_ref[...]
        xn_ref[...] = xn.astype(BF16)
        logits = jnp.dot(xn, wr_ref[...], preferred_element_type=F32,
                         precision=lax.Precision.HIGHEST) + br_ref[...]
        gl = jnp.where(lane < E_GROUPS, logits, neg)
        gmax = jnp.max(gl, axis=1, keepdims=True)
        gidx = jnp.min(jnp.where(gl == gmax, lane, LANE), axis=1, keepdims=True)
        p_group = 1.0 / jnp.sum(jnp.exp(gl - gmax), axis=1, keepdims=True)
        in_grp = (lane >= R_E0) & (lane < R_E0 + N_EXPERTS) & (((lane - R_E0) >> 2) == gidx)
        ev = jnp.where(in_grp, logits, neg)
        v1 = jnp.max(ev, axis=1, keepdims=True)
        i1 = jnp.min(jnp.where(ev == v1, lane, LANE), axis=1, keepdims=True)
        ev2 = jnp.where(lane == i1, neg, ev)
        v2 = jnp.max(ev2, axis=1, keepdims=True)
        i2 = jnp.min(jnp.where(ev2 == v2, lane, LANE), axis=1, keepdims=True)
        e2 = jnp.exp(v2 - v1)
        w1 = 1.0 / (1.0 + e2)
        comb_ref[...] = jnp.where(lane == i1, w1 * p_group, jnp.where(lane == i2, e2 * w1 * p_group, 0.0))
        acc_ref[...] = jnp.zeros_like(acc_ref)

    xn = xn_ref[...]
    comb = comb_ref[...]
    out = acc_ref[...]
    for j in range(MOE_EPS):
        hid = _silu(_dot(xn, wg_ref[j].astype(BF16))) * _dot(xn, wu_ref[j].astype(BF16))
        ce = jnp.sum(jnp.where(lane == R_E0 + e * MOE_EPS + j, comb, 0.0), axis=1, keepdims=True)
        out = out + ce * _dot(hid.astype(BF16), wd_ref[j].astype(BF16))
    acc_ref[...] = out

    @pl.when(e == N_EXPERTS // MOE_EPS - 1)
    def _():
        o_ref[...] = h_ref[...] + out


def _moe(h, nw, wr, br, wg, wu, wd, layer):
    n = h.shape[0]
    tm = _row_tile(n, 1024)
    return pl.pallas_call(
        _moe_kernel,
        grid=(n // tm, N_EXPERTS // MOE_EPS),
        in_specs=[
            pl.BlockSpec((tm, D_MODEL), lambda i, e: (i, 0)),
            pl.BlockSpec((1, D_MODEL), lambda i, e: (0, 0)),
            pl.BlockSpec((D_MODEL, LANE), lambda i, e: (0, 0)),
            pl.BlockSpec((1, LANE), lambda i, e: (0, 0)),
            pl.BlockSpec((None, MOE_EPS, D_MODEL, E_FF), lambda i, e: (layer, e, 0, 0)),
            pl.BlockSpec((None, MOE_EPS, D_MODEL, E_FF), lambda i, e: (layer, e, 0, 0)),
            pl.BlockSpec((None, MOE_EPS, E_FF, D_MODEL), lambda i, e: (layer, e, 0, 0)),
        ],
        out_specs=pl.BlockSpec((tm, D_MODEL), lambda i, e: (i, 0)),
        out_shape=jax.ShapeDtypeStruct((n, D_MODEL), F32),
        scratch_shapes=[
            pltpu.VMEM((tm, D_MODEL), BF16),
            pltpu.VMEM((tm, LANE), F32),
            pltpu.VMEM((tm, D_MODEL), F32),
        ],
        compiler_params=_cparams(("parallel", "arbitrary")),
    )(h, nw, wr, br, wg, wu, wd)


def _ple_kernel(h_ref, p_ref, nw_ref, wg_ref, wp_ref, nf_ref, o_ref, y_ref):
    x = h_ref[...]
    xn = (x * _rms_scale(x) * nw_ref[...]).astype(BF16)
    out = x + jax.nn.sigmoid(_dot(xn, wg_ref[...])) * _dot(p_ref[...].astype(BF16), wp_ref[...])
    o_ref[...] = out
    y_ref[...] = out * _rms_scale(out) * nf_ref[...]


def _ple(h, p, nw, wg, wp, nf, layer):
    n = h.shape[0]
    tm = _row_tile(n, 1024)
    row = lambda w: pl.BlockSpec((tm, w), lambda i: (i, 0))
    full = lambda shape: pl.BlockSpec(shape, lambda i: (0,) * len(shape))
    lfull = lambda shape: pl.BlockSpec((None,) + shape, lambda i: (layer,) + (0,) * len(shape))
    return pl.pallas_call(
        _ple_kernel,
        grid=(n // tm,),
        in_specs=[row(D_MODEL), pl.BlockSpec((None, tm, PLE_DIM), lambda i: (layer, i, 0)),
                  full((1, D_MODEL)), lfull((D_MODEL, D_MODEL)),
                  lfull((PLE_DIM, D_MODEL)), full((1, D_MODEL))],
        out_specs=[row(D_MODEL), row(D_MODEL)],
        out_shape=[jax.ShapeDtypeStruct((n, D_MODEL), F32), jax.ShapeDtypeStruct((n, D_MODEL), F32)],
        compiler_params=_cparams(("parallel",)),
    )(h, p, nw, wg, wp, nf)


def _prep_weights(w):
    win_r = _win_prep(w['w_in'])
    lane_pad = lambda a: jnp.pad(a, ((0, 0), (0, LANE - a.shape[1])))
    s5 = [_s5_params(w['s5_a_re'][i], w['s5_a_im'][i], w['s5_b_re'][i], w['s5_b_im'][i],
                     w['s5_c_re'][i], w['s5_c_im'][i], w['s5_log_dt'][i]) for i in range(DEPTH)]
    wr = jnp.concatenate([w['w_rg'], w['w_re'],
                          jnp.zeros((DEPTH, D_MODEL, LANE - E_GROUPS - N_EXPERTS), F32)], axis=2)
    br = jnp.concatenate([w['b_rg'], w['b_re'],
                          jnp.zeros((DEPTH, LANE - E_GROUPS - N_EXPERTS), F32)], axis=1)
    return dict(
        win=win_r,
        s5=s5,
        s5_d=w['s5_d'].reshape(DEPTH, 1, S5_WIDTH),
        wglu=w['s5_w_glu'].astype(BF16),
        bglu=w['s5_b_glu'].reshape(DEPTH, 1, 2 * S5_WIDTH),
        dtb=lane_pad(w['m2_dt_bias']).reshape(DEPTH, 1, LANE),
        alog=lane_pad(w['m2_a_log']).reshape(DEPTH, 1, LANE),
        dsk=jnp.repeat(w['m2_d'], M2_HEADDIM, axis=1).reshape(DEPTH, 1, M2_INNER),
        wa=w['w_br_hg'].astype(BF16), wb=w['w_br_s5'].astype(BF16), wc=w['w_br_m2'].astype(BF16),
        wo=w['w_out'].astype(BF16),
        wr=wr, br=br.reshape(DEPTH, 1, LANE),
        wpg=w['w_ple_gate'].astype(BF16), wpp=w['w_ple_proj'].astype(BF16),
    )


def _trunk(x, p, states, w, pw, t_valid, cfg):
    bsz, t, _ = x.shape
    n = bsz * t
    st_hg, st_re, st_im, st_ssm, st_conv = states
    per_layer = st_hg.shape[0] == DEPTH
    st_re = st_re.reshape(st_re.shape[0], bsz, S5_HW)
    st_im = st_im.reshape(st_im.shape[0], bsz, S5_HW)
    st_ssm = st_ssm.reshape(st_ssm.shape[0], bsz, M2_PAIRS, LANE, M2_STATE)
    h = x.reshape(n, D_MODEL)
    new = []
    y = None
    s_hg = None
    ssd_prev = None
    for i in range(DEPTH):
        sl = i if per_layer else 0
        z = _norm_matmul(h, w['norm_mix'][i].reshape(1, D_MODEL), pw['win'], i)
        z3 = z.reshape(bsz, t, Z_COLS)
        oa, s_hg = _hgrn(z3, w['hg_lb_logits'], w['hg_gnorm'][i].reshape(1, HG_DIM), st_hg, s_hg, i, sl,
                         cfg['hg_chunk'], cfg['hg_bt'], t_valid)
        bmat, cmat, a_vec = pw['s5'][i]
        ob, s_re, s_im = _s5(z3, bmat, cmat, a_vec, pw['s5_d'][i], pw['wglu'][i], pw['bglu'][i],
                             st_re, st_im, sl, cfg['s5_tc'], cfg['s5_bt'], t_valid)
        oc, s_ssm, conv_new = _ssd(z3, w['m2_conv_w'][i], w['m2_conv_b'][i].reshape(1, M2_CONV_DIM),
                                   pw['dtb'][i], pw['alog'][i], pw['dsk'][i],
                                   w['m2_norm'][i].reshape(1, M2_INNER), st_conv, st_ssm, ssd_prev, i, sl,
                                   cfg['m2_chunk'], cfg['m2_bt'], t_valid)
        ssd_prev = (s_ssm, conv_new)
        h = _merge(oa.reshape(n, HG_W), ob.reshape(n, S5_WIDTH), oc.reshape(n, M2_INNER), z, h,
                   pw['wa'], pw['wb'], pw['wc'], pw['wo'], i)
        h = _moe(h, w['norm_ffn'][i].reshape(1, D_MODEL), pw['wr'][i], pw['br'][i],
                 w['w_e_gate'], w['w_e_up'], w['w_e_down'], i)
        h, y = _ple(h, p.reshape(DEPTH, n, PLE_DIM), w['norm_ple'][i].reshape(1, D_MODEL),
                    pw['wpg'], pw['wpp'], w['norm_final'].reshape(1, D_MODEL), i)
        new.append((s_re.reshape(bsz, S5_GROUPS, S5_STATE), s_im.reshape(bsz, S5_GROUPS, S5_STATE)))
    s5_re, s5_im = (jnp.stack([nl[j] for nl in new]) for j in range(2))
    stacked = (s_hg, s5_re, s5_im, s_ssm.reshape(DEPTH, bsz, M2_HEADS, M2_HEADDIM, M2_STATE), conv_new)
    return y.reshape(bsz, t, D_MODEL), stacked


def _zero_states(bsz):
    return (jnp.zeros((1, bsz, HG_HEADS, HG_DIM, HG_DIM), F32),
            jnp.zeros((1, bsz, S5_GROUPS, S5_STATE), F32),
            jnp.zeros((1, bsz, S5_GROUPS, S5_STATE), F32),
            jnp.zeros((1, bsz, M2_HEADS, M2_HEADDIM, M2_STATE), F32),
            jnp.zeros((1, bsz, M2_CONV - 1, M2_CONV_DIM), F32))


def _pad_time(a, axis, t_pad):
    t = a.shape[axis]
    if t == t_pad:
        return a
    widths = [(0, 0)] * a.ndim
    widths[axis] = (0, t_pad - t)
    return jnp.pad(a, widths)


def kernel(x_prompt, x_sample, state_hgrn, state_s5_re, state_s5_im, state_ssm, state_conv,
           p_prompt, p_sample,
           norm_mix, w_in, hg_lb_logits, hg_gnorm, w_br_hg,
           s5_a_re, s5_a_im, s5_b_re, s5_b_im, s5_c_re, s5_c_im, s5_d, s5_log_dt, s5_w_glu, s5_b_glu, w_br_s5,
           m2_conv_w, m2_conv_b, m2_dt_bias, m2_a_log, m2_d, m2_norm, w_br_m2,
           w_out,
           norm_ffn, w_rg, b_rg, w_re, b_re, w_e_gate, w_e_up, w_e_down,
           norm_ple, w_ple_gate, w_ple_proj,
           norm_final):
    w = dict(norm_mix=norm_mix, w_in=w_in, hg_lb_logits=hg_lb_logits, hg_gnorm=hg_gnorm, w_br_hg=w_br_hg,
             s5_a_re=s5_a_re, s5_a_im=s5_a_im, s5_b_re=s5_b_re, s5_b_im=s5_b_im, s5_c_re=s5_c_re,
             s5_c_im=s5_c_im, s5_d=s5_d, s5_log_dt=s5_log_dt, s5_w_glu=s5_w_glu, s5_b_glu=s5_b_glu,
             w_br_s5=w_br_s5, m2_conv_w=m2_conv_w, m2_conv_b=m2_conv_b, m2_dt_bias=m2_dt_bias,
             m2_a_log=m2_a_log, m2_d=m2_d, m2_norm=m2_norm, w_br_m2=w_br_m2, w_out=w_out,
             norm_ffn=norm_ffn, w_rg=w_rg, b_rg=b_rg, w_re=w_re, b_re=b_re, w_e_gate=w_e_gate,
             w_e_up=w_e_up, w_e_down=w_e_down, norm_ple=norm_ple, w_ple_gate=w_ple_gate,
             w_ple_proj=w_ple_proj, norm_final=norm_final)
    pw = _prep_weights(w)

    bp, tp, _ = x_prompt.shape
    cfg_p = dict(hg_chunk=min(128, tp), hg_bt=min(2, bp), s5_tc=min(64, tp), s5_bt=SUBLANE,
                 m2_chunk=min(128, tp), m2_bt=1)
    y_p, st_p = _trunk(x_prompt, p_prompt, _zero_states(bp), w, pw, tp, cfg_p)

    bs, ts, _ = x_sample.shape
    ts_pad = -(-ts // SUBLANE) * SUBLANE
    cfg_s = dict(hg_chunk=ts_pad, hg_bt=min(8, bs), s5_tc=ts_pad, s5_bt=min(32, bs),
                 m2_chunk=ts_pad, m2_bt=min(4, bs))
    y_s, st_s = _trunk(_pad_time(x_sample, 1, ts_pad), _pad_time(p_sample, 2, ts_pad),
                       (state_hgrn, state_s5_re, state_s5_im, state_ssm, state_conv), w, pw, ts, cfg_s)
    return (y_p, y_s[:, :ts]) + st_p + st_s
```

```python
import functools

import jax
import jax.numpy as jnp
from jax import lax
from jax.experimental import pallas as pl
from jax.experimental.pallas import tpu as pltpu

F32 = jnp.float32
BF16 = jnp.bfloat16

D_MODEL = 1024
DEPTH = 2
PLE_DIM = 256
NORM_EPS = 1e-6

HG_HEADS = 4
HG_DIM = 128
HG_W = HG_HEADS * HG_DIM

S5_WIDTH = 512
S5_GROUP = 16
S5_GROUPS = S5_WIDTH // S5_GROUP
S5_STATE = 64
S5_HW = S5_GROUPS * S5_STATE
S5_SLABS = 2
S5_SLAB_U = S5_WIDTH // S5_SLABS
S5_SLAB_H = S5_HW // S5_SLABS

M2_INNER = 1024
M2_HEADDIM = 64
M2_HEADS = M2_INNER // M2_HEADDIM
M2_NGROUPS = 4
M2_STATE = 128
M2_CONV = 4
M2_BC = M2_NGROUPS * M2_STATE
M2_CONV_DIM = M2_INNER + 2 * M2_BC
M2_PAIRS = M2_HEADS // 2
M2_NORM_W = M2_INNER // M2_NGROUPS

N_BRANCH = 3
E_GROUPS = 4
E_PER_GROUP = 4
N_EXPERTS = E_GROUPS * E_PER_GROUP
E_FF = 256

LANE = 128
SUBLANE = 8

COL_XBC = 0
COL_Z = COL_XBC + M2_CONV_DIM
COL_GATE = COL_Z + M2_INNER
COL_Q = COL_GATE + N_BRANCH * D_MODEL
COL_F = COL_Q + HG_W
COL_I = COL_F + HG_W
COL_G = COL_I + HG_W
COL_U = COL_G + HG_W
COL_DT = COL_U + S5_WIDTH
Z_TN = 1280
Z_COLS = 7 * Z_TN
assert COL_DT + LANE <= Z_COLS and COL_DT % LANE == 0

_O_Q, _O_F, _O_I, _O_G, _O_U = 0, 512, 1024, 1536, 2048
_O_Z = 2560
_O_XBC = _O_Z + M2_INNER
_O_DT = _O_XBC + M2_CONV_DIM
_O_GATE = _O_DT + M2_HEADS
IN_COLS = _O_GATE + N_BRANCH * D_MODEL

VMEM_LIMIT = 56 * 1024 * 1024


def _cparams(sem):
    return pltpu.CompilerParams(dimension_semantics=sem, vmem_limit_bytes=VMEM_LIMIT)


def _rms_scale(x):
    return lax.rsqrt(jnp.mean(x * x, axis=-1, keepdims=True) + NORM_EPS)


def _silu(x):
    return x * jax.nn.sigmoid(x)


def _dot(a, b):
    return jnp.dot(a, b, preferred_element_type=F32)


def _dot_nt(a, b):
    return lax.dot_general(a, b, (((1,), (1,)), ((), ())), preferred_element_type=F32)


def _dot_tn(a, b):
    return lax.dot_general(a, b, (((0,), (0,)), ((), ())), preferred_element_type=F32)


def _cumsum_rows(x):
    n = x.shape[0]
    row = lax.broadcasted_iota(jnp.int32, x.shape, 0)
    s = 1
    while s < n:
        x = x + jnp.where(row >= s, pltpu.roll(x, s, 0), 0.0)
        s *= 2
    return x


def _row_tile(n, pref):
    t = min(pref, n)
    assert n % t == 0
    return t


def _win_prep_kernel(wt_ref, o_ref):
    w = wt_ref[...]
    lanes = w.shape[1]
    pieces = [
        w[_O_XBC:_O_XBC + M2_CONV_DIM], w[_O_Z:_O_Z + M2_INNER], w[_O_GATE:IN_COLS],
        w[:_O_Z], w[_O_DT:_O_DT + M2_HEADS], jnp.zeros((Z_COLS - IN_COLS, lanes), F32)]
    o_ref[...] = jnp.concatenate(pieces, axis=0).T.astype(BF16)


def _win_prep(w_in):
    tk = 256
    return pl.pallas_call(
        _win_prep_kernel,
        grid=(DEPTH, D_MODEL // tk),
        in_specs=[pl.BlockSpec((None, IN_COLS, tk), lambda l, i: (l, 0, i))],
        out_specs=pl.BlockSpec((None, tk, Z_COLS), lambda l, i: (l, i, 0)),
        out_shape=jax.ShapeDtypeStruct((DEPTH, D_MODEL, Z_COLS), BF16),
        compiler_params=_cparams(("parallel", "parallel")),
    )(jnp.swapaxes(w_in, 1, 2))


def _norm_matmul_kernel(x_ref, nw_ref, w_ref, o_ref, xn_ref):
    @pl.when(pl.program_id(1) == 0)
    def _():
        x = x_ref[...]
        xn_ref[...] = (x * _rms_scale(x) * nw_ref[...]).astype(BF16)

    o_ref[...] = _dot(xn_ref[...], w_ref[...])


def _norm_matmul(h, nw, w, layer):
    n = h.shape[0]
    cols = w.shape[2]
    tm = _row_tile(n, 1024)
    tn = Z_TN
    return pl.pallas_call(
        _norm_matmul_kernel,
        grid=(n // tm, cols // tn),
        in_specs=[
            pl.BlockSpec((tm, D_MODEL), lambda i, j: (i, 0)),
            pl.BlockSpec((1, D_MODEL), lambda i, j: (0, 0)),
            pl.BlockSpec((None, D_MODEL, tn), lambda i, j: (layer, 0, j)),
        ],
        out_specs=pl.BlockSpec((tm, tn), lambda i, j: (i, j)),
        out_shape=jax.ShapeDtypeStruct((n, cols), F32),
        scratch_shapes=[pltpu.VMEM((tm, D_MODEL), BF16)],
        compiler_params=_cparams(("parallel", "arbitrary")),
    )(h, nw, w)


def _hgrn_chunk(q, k, v, logf, st, lev, dcode, rows_used):
    C = q.shape[0]
    b = _cumsum_rows(logf)
    b_end = b[C - 1:C, :]
    o = _dot((q * jnp.exp(b)).astype(BF16), st.astype(BF16))
    v16 = v.astype(BF16)

    def shifted_score(j):
        if j == 0:
            return jnp.sum(q * k, axis=1, keepdims=True)
        kr = pltpu.roll(k, j, 0)
        br = pltpu.roll(b, j, 0)
        return jnp.sum(q * kr * jnp.exp(jnp.minimum(b - br, 0.0)), axis=1, keepdims=True)

    if C == SUBLANE:
        row = lax.broadcasted_iota(jnp.int32, (C, HG_DIM), 0)
        o = o + shifted_score(0) * v
        for j in range(1, min(SUBLANE, rows_used)):
            o = o + jnp.where(row >= j, shifted_score(j), 0.0) * pltpu.roll(v, j, 0)
        ke = k * jnp.exp(b_end - b)
        decay = jnp.transpose(jnp.broadcast_to(jnp.exp(b_end), (SUBLANE, HG_DIM)))[:, 0:1]
        return o, st * decay + _dot_tn(ke.astype(BF16), v16)

    sc = jnp.where(dcode == 0, shifted_score(0), 0.0)
    for j in range(1, SUBLANE):
        sc = jnp.where(dcode == j, shifted_score(j), sc)

    m = C // 2
    while m >= SUBLANE:
        parts = []
        for p0 in range(0, C, 2 * m):
            parts.append(jnp.broadcast_to(b[p0 + m - 1:p0 + m, :], (2 * m, HG_DIM)))
        r = parts[0] if len(parts) == 1 else jnp.concatenate(parts, axis=0)
        qe = q * jnp.exp(jnp.minimum(b - r, 0.0))
        ke = k * jnp.exp(jnp.minimum(r - b, 0.0))
        pm = _dot_nt(qe.astype(BF16), ke.astype(BF16))
        sc = jnp.where((lev >> (m.bit_length() - 1)) == 1, pm, sc)
        m //= 2

    o = o + _dot(sc.astype(BF16), v16)
    ke = k * jnp.exp(b_end - b)
    decay = jnp.transpose(jnp.broadcast_to(jnp.exp(b_end), (SUBLANE, HG_DIM)))[:, 0:1]
    st_new = st * decay + _dot_tn(ke.astype(BF16), v16)
    return o, st_new


def _hgrn_kernel(zq_ref, zf_ref, zi_ref, zg_ref, lbl_ref, gw_ref, s0_ref, *rest,
                 layer, chunk, bt, t_valid, t_total):
    o_ref, s_ref, st_ref = rest[-3:]
    c = pl.program_id(1)
    C = chunk

    @pl.when(c == 0)
    def _():
        st_ref[...] = s0_ref[...]

    lg = lbl_ref[...]
    e = jnp.exp(lg - jnp.max(lg, axis=0, keepdims=True))
    prob = e / jnp.sum(e, axis=0, keepdims=True)
    lb_all = jnp.zeros((1, HG_W), F32)
    for j in range(1, layer + 1):
        lb_all = lb_all + prob[j:j + 1]

    rr = lax.broadcasted_iota(jnp.int32, (C, C), 0)
    cc = lax.broadcasted_iota(jnp.int32, (C, C), 1)
    xr = rr ^ cc
    lev = jnp.where(rr > cc, xr, -1)
    dcode = jnp.where(xr < SUBLANE, rr - cc, -1)
    if t_valid < t_total:
        valid = (c * C + lax.broadcasted_iota(jnp.int32, (C, HG_DIM), 0)) < t_valid
    gw = gw_ref[...]

    for bi in range(bt):
        for h in range(HG_HEADS):
            sl = slice(h * HG_DIM, (h + 1) * HG_DIM)
            lb = lb_all[:, sl]
            zf = zf_ref[bi, :, sl]
            q = _silu(zq_ref[bi, :, sl])
            logf = jnp.log(lb + (1.0 - lb) * jax.nn.sigmoid(zf))
            k = (1.0 - lb) * jax.nn.sigmoid(-zf)
            if t_valid < t_total:
                logf = jnp.where(valid, logf, 0.0)
                k = jnp.where(valid, k, 0.0)
            o, st_new = _hgrn_chunk(q, k, zi_ref[bi, :, sl], logf, st_ref[bi, h], lev, dcode,
                                    min(C, t_valid))
            o_ref[bi, :, sl] = o * _rms_scale(o) * gw * _silu(zg_ref[bi, :, sl])
            st_ref[bi, h] = st_new

    @pl.when(c == pl.num_programs(1) - 1)
    def _():
        s_ref[...] = st_ref[...]


def _hgrn(z3, lb_logits, gw, s0, prev, layer, state_layer, chunk, bt, t_valid):
    bsz, t, _ = z3.shape
    nc = t // chunk
    kern = functools.partial(_hgrn_kernel, layer=layer, chunk=chunk, bt=bt, t_valid=t_valid, t_total=t)

    def zspec(col):
        return pl.BlockSpec((bt, chunk, HG_W), lambda b, c: (b, c, col // HG_W))

    in_specs = [
        zspec(COL_Q), zspec(COL_F), zspec(COL_I), zspec(COL_G),
        pl.BlockSpec((DEPTH, HG_W), lambda b, c: (0, 0)),
        pl.BlockSpec((1, HG_DIM), lambda b, c: (0, 0)),
        pl.BlockSpec((None, bt, HG_HEADS, HG_DIM, HG_DIM), lambda b, c: (state_layer, b, 0, 0, 0)),
    ]
    args = [z3, z3, z3, z3, lb_logits, gw, s0]
    aliases = {}
    if prev is not None:
        in_specs.append(pl.BlockSpec(memory_space=pl.ANY))
        args.append(prev)
        aliases = {len(args) - 1: 1}
    return pl.pallas_call(
        kern,
        grid=(bsz // bt, nc),
        in_specs=in_specs,
        out_specs=[
            pl.BlockSpec((bt, chunk, HG_W), lambda b, c: (b, c, 0)),
            pl.BlockSpec((None, bt, HG_HEADS, HG_DIM, HG_DIM), lambda b, c: (layer, b, 0, 0, 0)),
        ],
        out_shape=[
            jax.ShapeDtypeStruct((bsz, t, HG_W), F32),
            jax.ShapeDtypeStruct((DEPTH, bsz, HG_HEADS, HG_DIM, HG_DIM), F32),
        ],
        scratch_shapes=[pltpu.VMEM((bt, HG_HEADS, HG_DIM, HG_DIM), F32)],
        input_output_aliases=aliases,
        compiler_params=_cparams(("parallel", "arbitrary")),
    )(*args)


def _s5_kernel(u_ref, bmat_ref, cmat_ref, a_ref, d_ref, wglu_ref, bglu_ref, hre0_ref, him0_ref,
               o_ref, hre_ref, him_ref, hs_ref, hst_ref, *, tc, bt, steps, lane_chunk):
    c = pl.program_id(1)

    def re_off(s):
        return 2 * s * S5_SLAB_H

    def im_off(s):
        return (2 * s + 1) * S5_SLAB_H

    @pl.when(c == 0)
    def _():
        for s in range(S5_SLABS):
            src = slice(s * S5_SLAB_H, (s + 1) * S5_SLAB_H)
            hst_ref[:, re_off(s):re_off(s) + S5_SLAB_H] = hre0_ref[:, src]
            hst_ref[:, im_off(s):im_off(s) + S5_SLAB_H] = him0_ref[:, src]

    u = pltpu.einshape("btd->tbd", u_ref[...]).reshape(tc * bt, S5_WIDTH)
    u16 = u.astype(BF16)
    for s in range(S5_SLABS):
        hs_ref[:, re_off(s):re_off(s + 1)] = _dot(u16[:, s * S5_SLAB_U:(s + 1) * S5_SLAB_U], bmat_ref[s])

    for s in range(S5_SLABS):
        for l0 in range(0, S5_SLAB_H, lane_chunk):
            re_sl = slice(re_off(s) + l0, re_off(s) + l0 + lane_chunk)
            im_sl = slice(im_off(s) + l0, im_off(s) + l0 + lane_chunk)
            are = a_ref[:, re_sl]
            aim = a_ref[:, im_sl]

            def step(t, carry, re_sl=re_sl, im_sl=im_sl, are=are, aim=aim):
                hre, him = carry
                r0 = pl.multiple_of(t * bt, bt)
                nre = are * hre - aim * him + hs_ref[pl.ds(r0, bt), re_sl]
                nim = are * him + aim * hre + hs_ref[pl.ds(r0, bt), im_sl]
                hs_ref[pl.ds(r0, bt), re_sl] = nre
                hs_ref[pl.ds(r0, bt), im_sl] = nim
                return nre, nim

            hre, him = lax.fori_loop(0, steps, step, (hst_ref[:, re_sl], hst_ref[:, im_sl]), unroll=True)
            hst_ref[:, re_sl] = hre
            hst_ref[:, im_sl] = him

    ys = [_dot(hs_ref[:, re_off(s):re_off(s + 1)].astype(BF16), cmat_ref[s]) for s in range(S5_SLABS)]
    y = jnp.concatenate(ys, axis=1) + d_ref[...] * u
    g = jax.nn.gelu(y)
    gl = _dot(g.astype(BF16), wglu_ref[...]) + bglu_ref[...]
    o = gl[:, :S5_WIDTH] * jax.nn.sigmoid(gl[:, S5_WIDTH:])
    o_ref[...] = pltpu.einshape("tbd->btd", o.reshape(tc, bt, S5_WIDTH))

    @pl.when(c == pl.num_programs(1) - 1)
    def _():
        for s in range(S5_SLABS):
            dst = slice(s * S5_SLAB_H, (s + 1) * S5_SLAB_H)
            hre_ref[:, dst] = hst_ref[:, re_off(s):re_off(s) + S5_SLAB_H]
            him_ref[:, dst] = hst_ref[:, im_off(s):im_off(s) + S5_SLAB_H]


def _s5(z3, bmat, cmat, a_vec, d_vec, wglu, bglu, hre0, him0, state_layer, tc, bt, t_valid):
    bsz, t, _ = z3.shape
    nc = t // tc
    if t_valid < t:
        assert nc == 1
    steps = tc if t_valid == t else t_valid
    lane_chunk = min(S5_SLAB_H, max(LANE, (8 * SUBLANE * LANE) // bt))
    kern = functools.partial(_s5_kernel, tc=tc, bt=bt, steps=steps, lane_chunk=lane_chunk)
    full = lambda shape: pl.BlockSpec(shape, lambda b, c: (0,) * len(shape))
    return pl.pallas_call(
        kern,
        grid=(bsz // bt, nc),
        in_specs=[
            pl.BlockSpec((bt, tc, S5_WIDTH), lambda b, c: (b, c, COL_U // S5_WIDTH)),
            full((S5_SLABS, S5_SLAB_U, 2 * S5_SLAB_H)),
            full((S5_SLABS, 2 * S5_SLAB_H, S5_SLAB_U)),
            full((1, 2 * S5_HW)),
            full((1, S5_WIDTH)),
            full((S5_WIDTH, 2 * S5_WIDTH)),
            full((1, 2 * S5_WIDTH)),
            pl.BlockSpec((None, bt, S5_HW), lambda b, c: (state_layer, b, 0)),
            pl.BlockSpec((None, bt, S5_HW), lambda b, c: (state_layer, b, 0)),
        ],
        out_specs=[
            pl.BlockSpec((bt, tc, S5_WIDTH), lambda b, c: (b, c, 0)),
            pl.BlockSpec((bt, S5_HW), lambda b, c: (b, 0)),
            pl.BlockSpec((bt, S5_HW), lambda b, c: (b, 0)),
        ],
        out_shape=[
            jax.ShapeDtypeStruct((bsz, t, S5_WIDTH), F32),
            jax.ShapeDtypeStruct((bsz, S5_HW), F32),
            jax.ShapeDtypeStruct((bsz, S5_HW), F32),
        ],
        scratch_shapes=[
            pltpu.VMEM((tc * bt, 2 * S5_HW), F32),
            pltpu.VMEM((bt, 2 * S5_HW), F32),
        ],
        compiler_params=_cparams(("parallel", "arbitrary")),
    )(z3, bmat, cmat, a_vec, d_vec, wglu, bglu, hre0, him0)


def _s5_params(a_re, a_im, b_re, b_im, c_re, c_im, log_dt):
    g = S5_GROUPS
    dt = jnp.exp(log_dt)[:, None]
    er = jnp.exp(a_re * dt)
    abr = er * jnp.cos(a_im * dt)
    abi = er * jnp.sin(a_im * dt)
    den = a_re * a_re + a_im * a_im
    xr = abr - 1.0
    cr = (xr * a_re + abi * a_im) / den
    ci = (abi * a_re - xr * a_im) / den
    bbr = cr[..., None] * b_re - ci[..., None] * b_im
    bbi = cr[..., None] * b_im + ci[..., None] * b_re
    sg = g // S5_SLABS
    eye = jnp.eye(sg, dtype=F32)
    slab = lambda x: x.reshape((S5_SLABS, sg) + x.shape[1:])
    bd_in = lambda x: jnp.einsum('sgnc,gh->sgchn', slab(x), eye).reshape(S5_SLABS, S5_SLAB_U, S5_SLAB_H)
    bd_out = lambda x: jnp.einsum('sgcn,gh->sgnhc', slab(x), eye).reshape(S5_SLABS, S5_SLAB_H, S5_SLAB_U)
    bmat = jnp.concatenate([bd_in(bbr), bd_in(bbi)], axis=2).astype(BF16)
    cmat = jnp.concatenate([bd_out(c_re), -bd_out(c_im)], axis=1).astype(BF16)
    lanes = lambda x: x.reshape(S5_SLABS, 1, S5_SLAB_H)
    a_vec = jnp.concatenate([lanes(abr), lanes(abi)], axis=2).reshape(1, 2 * S5_HW)
    return bmat, cmat, a_vec


def _ssd_kernel(xbc_ref, zz_ref, zdt_ref, cw_ref, cb_ref, dtb_ref, alog_ref, dsk_ref, nw_ref,
                conv0_ref, s0_ref, *rest, chunk, bt, t_valid, t_total):
    y_ref, s_ref, cn_ref, st_ref, tail_ref = rest[-5:]
    c = pl.program_id(1)
    nc = t_total // chunk
    L = chunk
    TAIL = SUBLANE
    last_valid = t_valid - (nc - 1) * L

    @pl.when(c == 0)
    def _():
        st_ref[...] = s0_ref[...]
        tail_ref[...] = jnp.zeros((bt, TAIL, M2_CONV_DIM), F32)
        tail_ref[:, TAIL - (M2_CONV - 1):TAIL, :] = conv0_ref[...]

    causal = lax.broadcasted_iota(jnp.int32, (L, L), 0) >= lax.broadcasted_iota(jnp.int32, (L, L), 1)
    lo_lane = lax.broadcasted_iota(jnp.int32, (L, LANE), 1) < M2_HEADDIM
    lo_row = lax.broadcasted_iota(jnp.int32, (LANE, LANE), 0) < M2_HEADDIM
    neg_a = -jnp.exp(alog_ref[...])
    nw = nw_ref[...]
    if t_valid < t_total:
        valid = (c * L + lax.broadcasted_iota(jnp.int32, (L, LANE), 0)) < t_valid

    def pair_cols(a, h0):
        return jnp.where(lo_lane, a[:, h0:h0 + 1], a[:, h0 + 1:h0 + 2])

    for bi in range(bt):
        xbc = xbc_ref[bi]
        ext = jnp.concatenate([tail_ref[bi], xbc], axis=0)
        conv = cb_ref[...]
        for j in range(M2_CONV):
            o0 = TAIL - (M2_CONV - 1) + j
            conv = conv + cw_ref[j:j + 1, :] * ext[o0:o0 + L, :]
        tail_ref[bi] = xbc[L - TAIL:L, :]

        @pl.when(c == nc - 1)
        def _(ext=ext, bi=bi):
            cn_ref[bi] = ext[TAIL + last_valid - (M2_CONV - 1):TAIL + last_valid, :]

        act = _silu(conv)
        xs = act[:, :M2_INNER]
        bm = act[:, M2_INNER:M2_INNER + M2_BC].astype(BF16)
        cm = act[:, M2_INNER + M2_BC:].astype(BF16)

        dtr = zdt_ref[bi] + dtb_ref[...]
        dt = jnp.maximum(dtr, 0.0) + jnp.log1p(jnp.exp(-jnp.abs(dtr)))
        if t_valid < t_total:
            dt = jnp.where(valid, dt, 0.0)
        acs = _cumsum_rows(dt * neg_a)
        acs_t = acs.T
        acs_end = acs[L - 1:L, :]
        e_acs = jnp.exp(acs)
        e_end = jnp.exp(acs_end - acs)
        e_tot = jnp.exp(acs_end)

        ys = []
        for g in range(M2_NGROUPS):
            bg = bm[:, g * M2_STATE:(g + 1) * M2_STATE]
            cg = cm[:, g * M2_STATE:(g + 1) * M2_STATE]
            cb = _dot_nt(cg, bg)
            for pr in range(2):
                p = g * 2 + pr
                h0 = 2 * p
                xp = xs[:, p * LANE:(p + 1) * LANE]
                xdt = xp * pair_cols(dt, h0)
                ydiag = None
                for hh in range(2):
                    h = h0 + hh
                    lm = jnp.where(causal, jnp.exp(jnp.minimum(acs[:, h:h + 1] - acs_t[h:h + 1, :], 0.0)), 0.0)
                    mk = lo_lane if hh == 0 else jnp.logical_not(lo_lane)
                    part = _dot((cb * lm).astype(BF16), jnp.where(mk, xdt, 0.0).astype(BF16))
                    ydiag = part if ydiag is None else ydiag + part
                sp = st_ref[bi, p]
                yoff = pair_cols(e_acs, h0) * _dot_nt(cg, sp.astype(BF16))
                ys.append(ydiag + yoff + dsk_ref[:, p * LANE:(p + 1) * LANE] * xp)
                xe = (xdt * pair_cols(e_end, h0)).astype(BF16)
                scale = jnp.where(lo_row, e_tot[:, h0:h0 + 1], e_tot[:, h0 + 1:h0 + 2])
                st_ref[bi, p] = scale * sp + _dot_tn(xe, bg)

        y = jnp.concatenate(ys, axis=1) * _silu(zz_ref[bi])
        outs = []
        for g in range(M2_NGROUPS):
            sl = slice(g * M2_NORM_W, (g + 1) * M2_NORM_W)
            yg = y[:, sl]
            outs.append(yg * _rms_scale(yg) * nw[:, sl])
        y_ref[bi] = jnp.concatenate(outs, axis=1)

    @pl.when(c == nc - 1)
    def _():
        s_ref[...] = st_ref[...]


def _ssd(z3, cw, cb, dtb, alog, dsk, nw, conv0, s0, prev, layer, state_layer, chunk, bt, t_valid):
    bsz, t, _ = z3.shape
    nc = t // chunk
    assert chunk >= SUBLANE and t_valid - (nc - 1) * chunk >= 1
    kern = functools.partial(_ssd_kernel, chunk=chunk, bt=bt, t_valid=t_valid, t_total=t)
    full = lambda shape: pl.BlockSpec(shape, lambda b, c: (0,) * len(shape))
    extra_specs, extra_args, aliases = [], [], {}
    if prev is not None:
        extra_specs = [pl.BlockSpec(memory_space=pl.ANY)] * 2
        extra_args = list(prev)
        aliases = {11: 1, 12: 2}
    return pl.pallas_call(
        kern,
        grid=(bsz // bt, nc),
        input_output_aliases=aliases,
        in_specs=[
            pl.BlockSpec((bt, chunk, M2_CONV_DIM), lambda b, c: (b, c, COL_XBC // M2_CONV_DIM)),
            pl.BlockSpec((bt, chunk, M2_INNER), lambda b, c: (b, c, COL_Z // M2_INNER)),
            pl.BlockSpec((bt, chunk, LANE), lambda b, c: (b, c, COL_DT // LANE)),
            full((M2_CONV, M2_CONV_DIM)),
            full((1, M2_CONV_DIM)),
            full((1, LANE)),
            full((1, LANE)),
            full((1, M2_INNER)),
            full((1, M2_INNER)),
            pl.BlockSpec((None, bt, M2_CONV - 1, M2_CONV_DIM), lambda b, c: (state_layer, b, 0, 0)),
            pl.BlockSpec((None, bt, M2_PAIRS, LANE, M2_STATE), lambda b, c: (state_layer, b, 0, 0, 0)),
        ] + extra_specs,
        out_specs=[
            pl.BlockSpec((bt, chunk, M2_INNER), lambda b, c: (b, c, 0)),
            pl.BlockSpec((None, bt, M2_PAIRS, LANE, M2_STATE), lambda b, c: (layer, b, 0, 0, 0)),
            pl.BlockSpec((None, bt, M2_CONV - 1, M2_CONV_DIM), lambda b, c: (layer, b, 0, 0)),
        ],
        out_shape=[
            jax.ShapeDtypeStruct((bsz, t, M2_INNER), F32),
            jax.ShapeDtypeStruct((DEPTH, bsz, M2_PAIRS, LANE, M2_STATE), F32),
            jax.ShapeDtypeStruct((DEPTH, bsz, M2_CONV - 1, M2_CONV_DIM), F32),
        ],
        scratch_shapes=[
            pltpu.VMEM((bt, M2_PAIRS, LANE, M2_STATE), F32),
            pltpu.VMEM((bt, SUBLANE, M2_CONV_DIM), F32),
        ],
        compiler_params=_cparams(("parallel", "arbitrary")),
    )(z3, z3, z3, cw, cb, dtb, alog, dsk, nw, conv0, s0, *extra_args)


def _merge_kernel(oa_ref, ob_ref, oc_ref, ga_ref, gb_ref, gc_ref, h_ref, wa_ref, wb_ref, wc_ref, wo_ref, o_ref):
    m = jax.nn.sigmoid(ga_ref[...]) * _dot(oa_ref[...].astype(BF16), wa_ref[...])
    m = m + jax.nn.sigmoid(gb_ref[...]) * _dot(ob_ref[...].astype(BF16), wb_ref[...])
    m = m + jax.nn.sigmoid(gc_ref[...]) * _dot(oc_ref[...].astype(BF16), wc_ref[...])
    o_ref[...] = h_ref[...] + _dot(m.astype(BF16), wo_ref[...])


def _merge(oa, ob, oc, z, h, wa, wb, wc, wo, layer):
    n = h.shape[0]
    tm = _row_tile(n, 512)
    gblk = COL_GATE // D_MODEL
    row = lambda w, j=0: pl.BlockSpec((tm, w), lambda i: (i, j))
    full = lambda shape: pl.BlockSpec((None,) + shape, lambda i: (layer,) + (0,) * len(shape))
    return pl.pallas_call(
        _merge_kernel,
        grid=(n // tm,),
        in_specs=[
            row(HG_W), row(S5_WIDTH), row(M2_INNER),
            row(D_MODEL, gblk), row(D_MODEL, gblk + 1), row(D_MODEL, gblk + 2),
            row(D_MODEL),
            full((HG_W, D_MODEL)), full((S5_WIDTH, D_MODEL)), full((M2_INNER, D_MODEL)),
            full((D_MODEL, D_MODEL)),
        ],
        out_specs=row(D_MODEL),
        out_shape=jax.ShapeDtypeStruct((n, D_MODEL), F32),
        compiler_params=_cparams(("parallel",)),
    )(oa, ob, oc, z, z, z, h, wa, wb, wc, wo)


R_E0 = E_GROUPS
MOE_EPS = 2


def _moe_kernel(h_ref, nw_ref, wr_ref, br_ref, wg_ref, wu_ref, wd_ref, o_ref, xn_ref, comb_ref, acc_ref):
    e = pl.program_id(1)
    tm = h_ref.shape[0]
    lane = lax.broadcasted_iota(jnp.int32, (tm, LANE), 1)
    neg = -jnp.inf

    @pl.when(e == 0)
    def _():
        x = h_ref[...]
        xn = x * _rms_scale(x) * nw_ref[...]
        xn_ref[...] = xn.astype(BF16)
        logits = jnp.dot(xn, wr_ref[...], preferred_element_type=F32,
                         precision=lax.Precision.HIGHEST) + br_ref[...]
        gl = jnp.where(lane < E_GROUPS, logits, neg)
        gmax = jnp.max(gl, axis=1, keepdims=True)
        gidx = jnp.min(jnp.where(gl == gmax, lane, LANE), axis=1, keepdims=True)
        p_group = 1.0 / jnp.sum(jnp.exp(gl - gmax), axis=1, keepdims=True)
        in_grp = (lane >= R_E0) & (lane < R_E0 + N_EXPERTS) & (((lane - R_E0) >> 2) == gidx)
        ev = jnp.where(in_grp, logits, neg)
        v1 = jnp.max(ev, axis=1, keepdims=True)
        i1 = jnp.min(jnp.where(ev == v1, lane, LANE), axis=1, keepdims=True)
        ev2 = jnp.where(lane == i1, neg, ev)
        v2 = jnp.max(ev2, axis=1, keepdims=True)
        i2 = jnp.min(jnp.where(ev2 == v2, lane, LANE), axis=1, keepdims=True)
        e2 = jnp.exp(v2 - v1)
        w1 = 1.0 / (1.0 + e2)
        comb_ref[...] = jnp.where(lane == i1, w1 * p_group, jnp.where(lane == i2, e2 * w1 * p_group, 0.0))
        acc_ref[...] = jnp.zeros_like(acc_ref)

    xn = xn_ref[...]
    comb = comb_ref[...]
    out = acc_ref[...]
    for j in range(MOE_EPS):
        hid = _silu(_dot(xn, wg_ref[j].astype(BF16))) * _dot(xn, wu_ref[j].astype(BF16))
        ce = jnp.sum(jnp.where(lane == R_E0 + e * MOE_EPS + j, comb, 0.0), axis=1, keepdims=True)
        out = out + ce * _dot(hid.astype(BF16), wd_ref[j].astype(BF16))
    acc_ref[...] = out

    @pl.when(e == N_EXPERTS // MOE_EPS - 1)
    def _():
        o_ref[...] = h_ref[...] + out


def _moe(h, nw, wr, br, wg, wu, wd, layer):
    n = h.shape[0]
    tm = _row_tile(n, 1024)
    return pl.pallas_call(
        _moe_kernel,
        grid=(n // tm, N_EXPERTS // MOE_EPS),
        in_specs=[
            pl.BlockSpec((tm, D_MODEL), lambda i, e: (i, 0)),
            pl.BlockSpec((1, D_MODEL), lambda i, e: (0, 0)),
            pl.BlockSpec((D_MODEL, LANE), lambda i, e: (0, 0)),
            pl.BlockSpec((1, LANE), lambda i, e: (0, 0)),
            pl.BlockSpec((None, MOE_EPS, D_MODEL, E_FF), lambda i, e: (layer, e, 0, 0)),
            pl.BlockSpec((None, MOE_EPS, D_MODEL, E_FF), lambda i, e: (layer, e, 0, 0)),
            pl.BlockSpec((None, MOE_EPS, E_FF, D_MODEL), lambda i, e: (layer, e, 0, 0)),
        ],
        out_specs=pl.BlockSpec((tm, D_MODEL), lambda i, e: (i, 0)),
        out_shape=jax.ShapeDtypeStruct((n, D_MODEL), F32),
        scratch_shapes=[
            pltpu.VMEM((tm, D_MODEL), BF16),
            pltpu.VMEM((tm, LANE), F32),
            pltpu.VMEM((tm, D_MODEL), F32),
        ],
        compiler_params=_cparams(("parallel", "arbitrary")),
    )(h, nw, wr, br, wg, wu, wd)


def _ple_kernel(h_ref, p_ref, nw_ref, wg_ref, wp_ref, nf_ref, o_ref, *, final):
    x = h_ref[...]
    xn = (x * _rms_scale(x) * nw_ref[...]).astype(BF16)
    out = x + jax.nn.sigmoid(_dot(xn, wg_ref[...])) * _dot(p_ref[...].astype(BF16), wp_ref[...])
    o_ref[...] = out * _rms_scale(out) * nf_ref[...] if final else out


def _ple(h, p, nw, wg, wp, nf, layer):
    n = h.shape[0]
    tm = _row_tile(n, 1024)
    row = lambda w: pl.BlockSpec((tm, w), lambda i: (i, 0))
    full = lambda shape: pl.BlockSpec(shape, lambda i: (0,) * len(shape))
    lfull = lambda shape: pl.BlockSpec((None,) + shape, lambda i: (layer,) + (0,) * len(shape))
    return pl.pallas_call(
        functools.partial(_ple_kernel, final=layer == DEPTH - 1),
        grid=(n // tm,),
        in_specs=[row(D_MODEL), pl.BlockSpec((None, tm, PLE_DIM), lambda i: (layer, i, 0)),
                  full((1, D_MODEL)), lfull((D_MODEL, D_MODEL)),
                  lfull((PLE_DIM, D_MODEL)), full((1, D_MODEL))],
        out_specs=row(D_MODEL),
        out_shape=jax.ShapeDtypeStruct((n, D_MODEL), F32),
        compiler_params=_cparams(("parallel",)),
    )(h, p, nw, wg, wp, nf)


def _prep_weights(w):
    win_r = _win_prep(w['w_in'])
    lane_pad = lambda a: jnp.pad(a, ((0, 0), (0, LANE - a.shape[1])))
    s5 = [_s5_params(w['s5_a_re'][i], w['s5_a_im'][i], w['s5_b_re'][i], w['s5_b_im'][i],
                     w['s5_c_re'][i], w['s5_c_im'][i], w['s5_log_dt'][i]) for i in range(DEPTH)]
    wr = jnp.concatenate([w['w_rg'], w['w_re'],
                          jnp.zeros((DEPTH, D_MODEL, LANE - E_GROUPS - N_EXPERTS), F32)], axis=2)
    br = jnp.concatenate([w['b_rg'], w['b_re'],
                          jnp.zeros((DEPTH, LANE - E_GROUPS - N_EXPERTS), F32)], axis=1)
    return dict(
        win=win_r,
        s5=s5,
        s5_d=w['s5_d'].reshape(DEPTH, 1, S5_WIDTH),
        wglu=w['s5_w_glu'].astype(BF16),
        bglu=w['s5_b_glu'].reshape(DEPTH, 1, 2 * S5_WIDTH),
        dtb=lane_pad(w['m2_dt_bias']).reshape(DEPTH, 1, LANE),
        alog=lane_pad(w['m2_a_log']).reshape(DEPTH, 1, LANE),
        dsk=jnp.repeat(w['m2_d'], M2_HEADDIM, axis=1).reshape(DEPTH, 1, M2_INNER),
        wa=w['w_br_hg'].astype(BF16), wb=w['w_br_s5'].astype(BF16), wc=w['w_br_m2'].astype(BF16),
        wo=w['w_out'].astype(BF16),
        wr=wr, br=br.reshape(DEPTH, 1, LANE),
        wpg=w['w_ple_gate'].astype(BF16), wpp=w['w_ple_proj'].astype(BF16),
    )


def _trunk(x, p, states, w, pw, t_valid, cfg):
    bsz, t, _ = x.shape
    n = bsz * t
    st_hg, st_re, st_im, st_ssm, st_conv = states
    per_layer = st_hg.shape[0] == DEPTH
    st_re = st_re.reshape(st_re.shape[0], bsz, S5_HW)
    st_im = st_im.reshape(st_im.shape[0], bsz, S5_HW)
    st_ssm = st_ssm.reshape(st_ssm.shape[0], bsz, M2_PAIRS, LANE, M2_STATE)
    h = x.reshape(n, D_MODEL)
    new = []
    s_hg = None
    ssd_prev = None
    for i in range(DEPTH):
        sl = i if per_layer else 0
        z = _norm_matmul(h, w['norm_mix'][i].reshape(1, D_MODEL), pw['win'], i)
        z3 = z.reshape(bsz, t, Z_COLS)
        oa, s_hg = _hgrn(z3, w['hg_lb_logits'], w['hg_gnorm'][i].reshape(1, HG_DIM), st_hg, s_hg, i, sl,
                         cfg['hg_chunk'], cfg['hg_bt'], t_valid)
        bmat, cmat, a_vec = pw['s5'][i]
        ob, s_re, s_im = _s5(z3, bmat, cmat, a_vec, pw['s5_d'][i], pw['wglu'][i], pw['bglu'][i],
                             st_re, st_im, sl, cfg['s5_tc'], cfg['s5_bt'], t_valid)
        oc, s_ssm, conv_new = _ssd(z3, w['m2_conv_w'][i], w['m2_conv_b'][i].reshape(1, M2_CONV_DIM),
                                   pw['dtb'][i], pw['alog'][i], pw['dsk'][i],
                                   w['m2_norm'][i].reshape(1, M2_INNER), st_conv, st_ssm, ssd_prev, i, sl,
                                   cfg['m2_chunk'], cfg['m2_bt'], t_valid)
        ssd_prev = (s_ssm, conv_new)
        h = _merge(oa.reshape(n, HG_W), ob.reshape(n, S5_WIDTH), oc.reshape(n, M2_INNER), z, h,
                   pw['wa'], pw['wb'], pw['wc'], pw['wo'], i)
        h = _moe(h, w['norm_ffn'][i].reshape(1, D_MODEL), pw['wr'][i], pw['br'][i],
                 w['w_e_gate'], w['w_e_up'], w['w_e_down'], i)
        h = _ple(h, p.reshape(DEPTH, n, PLE_DIM), w['norm_ple'][i].reshape(1, D_MODEL),
                 pw['wpg'], pw['wpp'], w['norm_final'].reshape(1, D_MODEL), i)
        new.append((s_re.reshape(bsz, S5_GROUPS, S5_STATE), s_im.reshape(bsz, S5_GROUPS, S5_STATE)))
    s5_re, s5_im = (jnp.stack([nl[j] for nl in new]) for j in range(2))
    stacked = (s_hg, s5_re, s5_im, s_ssm.reshape(DEPTH, bsz, M2_HEADS, M2_HEADDIM, M2_STATE), conv_new)
    return h.reshape(bsz, t, D_MODEL), stacked


def _zero_states(bsz):
    return (jnp.zeros((1, bsz, HG_HEADS, HG_DIM, HG_DIM), F32),
            jnp.zeros((1, bsz, S5_GROUPS, S5_STATE), F32),
            jnp.zeros((1, bsz, S5_GROUPS, S5_STATE), F32),
            jnp.zeros((1, bsz, M2_HEADS, M2_HEADDIM, M2_STATE), F32),
            jnp.zeros((1, bsz, M2_CONV - 1, M2_CONV_DIM), F32))


def _pad_time(a, axis, t_pad):
    t = a.shape[axis]
    if t == t_pad:
        return a
    widths = [(0, 0)] * a.ndim
    widths[axis] = (0, t_pad - t)
    return jnp.pad(a, widths)


def kernel(x_prompt, x_sample, state_hgrn, state_s5_re, state_s5_im, state_ssm, state_conv,
           p_prompt, p_sample,
           norm_mix, w_in, hg_lb_logits, hg_gnorm, w_br_hg,
           s5_a_re, s5_a_im, s5_b_re, s5_b_im, s5_c_re, s5_c_im, s5_d, s5_log_dt, s5_w_glu, s5_b_glu, w_br_s5,
           m2_conv_w, m2_conv_b, m2_dt_bias, m2_a_log, m2_d, m2_norm, w_br_m2,
           w_out,
           norm_ffn, w_rg, b_rg, w_re, b_re, w_e_gate, w_e_up, w_e_down,
           norm_ple, w_ple_gate, w_ple_proj,
           norm_final):
    w = dict(norm_mix=norm_mix, w_in=w_in, hg_lb_logits=hg_lb_logits, hg_gnorm=hg_gnorm, w_br_hg=w_br_hg,
             s5_a_re=s5_a_re, s5_a_im=s5_a_im, s5_b_re=s5_b_re, s5_b_im=s5_b_im, s5_c_re=s5_c_re,
             s5_c_im=s5_c_im, s5_d=s5_d, s5_log_dt=s5_log_dt, s5_w_glu=s5_w_glu, s5_b_glu=s5_b_glu,
             w_br_s5=w_br_s5, m2_conv_w=m2_conv_w, m2_conv_b=m2_conv_b, m2_dt_bias=m2_dt_bias,
             m2_a_log=m2_a_log, m2_d=m2_d, m2_norm=m2_norm, w_br_m2=w_br_m2, w_out=w_out,
             norm_ffn=norm_ffn, w_rg=w_rg, b_rg=b_rg, w_re=w_re, b_re=b_re, w_e_gate=w_e_gate,
             w_e_up=w_e_up, w_e_down=w_e_down, norm_ple=norm_ple, w_ple_gate=w_ple_gate,
             w_ple_proj=w_ple_proj, norm_final=norm_final)
    pw = _prep_weights(w)

    bp, tp, _ = x_prompt.shape
    cfg_p = dict(hg_chunk=min(128, tp), hg_bt=min(2, bp), s5_tc=min(64, tp), s5_bt=SUBLANE,
                 m2_chunk=min(128, tp), m2_bt=1)
    y_p, st_p = _trunk(x_prompt, p_prompt, _zero_states(bp), w, pw, tp, cfg_p)

    bs, ts, _ = x_sample.shape
    ts_pad = -(-ts // SUBLANE) * SUBLANE
    cfg_s = dict(hg_chunk=ts_pad, hg_bt=min(8, bs), s5_tc=ts_pad, s5_bt=min(32, bs),
                 m2_chunk=ts_pad, m2_bt=min(4, bs))
    y_s, st_s = _trunk(_pad_time(x_sample, 1, ts_pad), _pad_time(p_sample, 2, ts_pad),
                       (state_hgrn, state_s5_re, state_s5_im, state_ssm, state_conv), w, pw, ts, cfg_s)
    return (y_p, y_s[:, :ts]) + st_p + st_s
```

```python
import functools

import jax
import jax.numpy as jnp
from jax import lax
from jax.experimental import pallas as pl
from jax.experimental.pallas import tpu as pltpu

F32 = jnp.float32
BF16 = jnp.bfloat16

D_MODEL = 1024
DEPTH = 2
PLE_DIM = 256
NORM_EPS = 1e-6

HG_HEADS = 4
HG_DIM = 128
HG_W = HG_HEADS * HG_DIM

S5_WIDTH = 512
S5_GROUP = 16
S5_GROUPS = S5_WIDTH // S5_GROUP
S5_STATE = 64
S5_HW = S5_GROUPS * S5_STATE
S5_SLABS = 2
S5_SLAB_U = S5_WIDTH // S5_SLABS
S5_SLAB_H = S5_HW // S5_SLABS

M2_INNER = 1024
M2_HEADDIM = 64
M2_HEADS = M2_INNER // M2_HEADDIM
M2_NGROUPS = 4
M2_STATE = 128
M2_CONV = 4
M2_BC = M2_NGROUPS * M2_STATE
M2_CONV_DIM = M2_INNER + 2 * M2_BC
M2_PAIRS = M2_HEADS // 2
M2_NORM_W = M2_INNER // M2_NGROUPS

N_BRANCH = 3
E_GROUPS = 4
E_PER_GROUP = 4
N_EXPERTS = E_GROUPS * E_PER_GROUP
E_FF = 256

LANE = 128
SUBLANE = 8

COL_XBC = 0
COL_Z = COL_XBC + M2_CONV_DIM
COL_GATE = COL_Z + M2_INNER
COL_Q = COL_GATE + N_BRANCH * D_MODEL
COL_F = COL_Q + HG_W
COL_I = COL_F + HG_W
COL_G = COL_I + HG_W
COL_U = COL_G + HG_W
COL_DT = COL_U + S5_WIDTH
Z_TN = 1280
Z_COLS = 7 * Z_TN
assert COL_DT + LANE <= Z_COLS and COL_DT % LANE == 0

_O_Q, _O_F, _O_I, _O_G, _O_U = 0, 512, 1024, 1536, 2048
_O_Z = 2560
_O_XBC = _O_Z + M2_INNER
_O_DT = _O_XBC + M2_CONV_DIM
_O_GATE = _O_DT + M2_HEADS
IN_COLS = _O_GATE + N_BRANCH * D_MODEL

VMEM_LIMIT = 56 * 1024 * 1024


def _cparams(sem):
    return pltpu.CompilerParams(dimension_semantics=sem, vmem_limit_bytes=VMEM_LIMIT)


def _rms_scale(x):
    return lax.rsqrt(jnp.mean(x * x, axis=-1, keepdims=True) + NORM_EPS)


def _silu(x):
    return x * jax.nn.sigmoid(x)


def _dot(a, b):
    return jnp.dot(a, b, preferred_element_type=F32)


def _dot_nt(a, b):
    return lax.dot_general(a, b, (((1,), (1,)), ((), ())), preferred_element_type=F32)


def _dot_tn(a, b):
    return lax.dot_general(a, b, (((0,), (0,)), ((), ())), preferred_element_type=F32)


def _cumsum_rows(x):
    n = x.shape[0]
    row = lax.broadcasted_iota(jnp.int32, x.shape, 0)
    s = 1
    while s < n:
        x = x + jnp.where(row >= s, pltpu.roll(x, s, 0), 0.0)
        s *= 2
    return x


def _write_layer(ref, layer, first, value, idx=()):
    if not first:
        ref[idx if idx else ...] = value
        return
    for l in range(DEPTH):
        ref[(l,) + idx] = value if l == layer else jnp.zeros_like(value)


def _row_tile(n, pref):
    t = min(pref, n)
    assert n % t == 0
    return t


def _win_prep_kernel(wt_ref, o_ref):
    w = wt_ref[...]
    lanes = w.shape[1]
    pieces = [
        w[_O_XBC:_O_XBC + M2_CONV_DIM], w[_O_Z:_O_Z + M2_INNER], w[_O_GATE:IN_COLS],
        w[:_O_Z], w[_O_DT:_O_DT + M2_HEADS], jnp.zeros((Z_COLS - IN_COLS, lanes), F32)]
    o_ref[...] = jnp.concatenate(pieces, axis=0).T.astype(BF16)


def _win_prep(w_in):
    tk = 256
    return pl.pallas_call(
        _win_prep_kernel,
        grid=(DEPTH, D_MODEL // tk),
        in_specs=[pl.BlockSpec((None, IN_COLS, tk), lambda l, i: (l, 0, i))],
        out_specs=pl.BlockSpec((None, tk, Z_COLS), lambda l, i: (l, i, 0)),
        out_shape=jax.ShapeDtypeStruct((DEPTH, D_MODEL, Z_COLS), BF16),
        compiler_params=_cparams(("parallel", "parallel")),
    )(jnp.swapaxes(w_in, 1, 2))


def _norm_matmul_kernel(x_ref, nw_ref, w_ref, o_ref, xn_ref):
    @pl.when(pl.program_id(1) == 0)
    def _():
        x = x_ref[...]
        xn_ref[...] = (x * _rms_scale(x) * nw_ref[...]).astype(BF16)

    o_ref[...] = _dot(xn_ref[...], w_ref[...])


def _norm_matmul(h, nw, w, layer):
    n = h.shape[0]
    cols = w.shape[2]
    tm = _row_tile(n, 1024)
    tn = Z_TN
    return pl.pallas_call(
        _norm_matmul_kernel,
        grid=(n // tm, cols // tn),
        in_specs=[
            pl.BlockSpec((tm, D_MODEL), lambda i, j: (i, 0)),
            pl.BlockSpec((1, D_MODEL), lambda i, j: (0, 0)),
            pl.BlockSpec((None, D_MODEL, tn), lambda i, j: (layer, 0, j)),
        ],
        out_specs=pl.BlockSpec((tm, tn), lambda i, j: (i, j)),
        out_shape=jax.ShapeDtypeStruct((n, cols), F32),
        scratch_shapes=[pltpu.VMEM((tm, D_MODEL), BF16)],
        compiler_params=_cparams(("parallel", "arbitrary")),
    )(h, nw, w)


def _hgrn_chunk(q, k, v, logf, st, lev, dcode, rows_used):
    C = q.shape[0]
    b = _cumsum_rows(logf)
    b_end = b[C - 1:C, :]
    o = _dot((q * jnp.exp(b)).astype(BF16), st.astype(BF16))
    v16 = v.astype(BF16)

    def shifted_score(j):
        if j == 0:
            return jnp.sum(q * k, axis=1, keepdims=True)
        kr = pltpu.roll(k, j, 0)
        br = pltpu.roll(b, j, 0)
        return jnp.sum(q * kr * jnp.exp(jnp.minimum(b - br, 0.0)), axis=1, keepdims=True)

    if C == SUBLANE:
        row = lax.broadcasted_iota(jnp.int32, (C, HG_DIM), 0)
        o = o + shifted_score(0) * v
        for j in range(1, min(SUBLANE, rows_used)):
            o = o + jnp.where(row >= j, shifted_score(j), 0.0) * pltpu.roll(v, j, 0)
        ke = k * jnp.exp(b_end - b)
        decay = jnp.transpose(jnp.broadcast_to(jnp.exp(b_end), (SUBLANE, HG_DIM)))[:, 0:1]
        return o, st * decay + _dot_tn(ke.astype(BF16), v16)

    sc = jnp.where(dcode == 0, shifted_score(0), 0.0)
    for j in range(1, SUBLANE):
        sc = jnp.where(dcode == j, shifted_score(j), sc)

    m = C // 2
    while m >= SUBLANE:
        parts = []
        for p0 in range(0, C, 2 * m):
            parts.append(jnp.broadcast_to(b[p0 + m - 1:p0 + m, :], (2 * m, HG_DIM)))
        r = parts[0] if len(parts) == 1 else jnp.concatenate(parts, axis=0)
        qe = q * jnp.exp(jnp.minimum(b - r, 0.0))
        ke = k * jnp.exp(jnp.minimum(r - b, 0.0))
        pm = _dot_nt(qe.astype(BF16), ke.astype(BF16))
        sc = jnp.where((lev >> (m.bit_length() - 1)) == 1, pm, sc)
        m //= 2

    o = o + _dot(sc.astype(BF16), v16)
    ke = k * jnp.exp(b_end - b)
    decay = jnp.transpose(jnp.broadcast_to(jnp.exp(b_end), (SUBLANE, HG_DIM)))[:, 0:1]
    st_new = st * decay + _dot_tn(ke.astype(BF16), v16)
    return o, st_new


def _hgrn_kernel(zq_ref, zf_ref, zi_ref, zg_ref, lbl_ref, gw_ref, s0_ref, *rest,
                 layer, first, chunk, bt, t_valid, t_total):
    o_ref, s_ref, st_ref = rest[-3:]
    c = pl.program_id(1)
    C = chunk

    @pl.when(c == 0)
    def _():
        st_ref[...] = s0_ref[...]

    lg = lbl_ref[...]
    e = jnp.exp(lg - jnp.max(lg, axis=0, keepdims=True))
    prob = e / jnp.sum(e, axis=0, keepdims=True)
    lb_all = jnp.zeros((1, HG_W), F32)
    for j in range(1, layer + 1):
        lb_all = lb_all + prob[j:j + 1]

    rr = lax.broadcasted_iota(jnp.int32, (C, C), 0)
    cc = lax.broadcasted_iota(jnp.int32, (C, C), 1)
    xr = rr ^ cc
    lev = jnp.where(rr > cc, xr, -1)
    dcode = jnp.where(xr < SUBLANE, rr - cc, -1)
    if t_valid < t_total:
        valid = (c * C + lax.broadcasted_iota(jnp.int32, (C, HG_DIM), 0)) < t_valid
    gw = gw_ref[...]

    for bi in range(bt):
        for h in range(HG_HEADS):
            sl = slice(h * HG_DIM, (h + 1) * HG_DIM)
            lb = lb_all[:, sl]
            zf = zf_ref[bi, :, sl]
            q = _silu(zq_ref[bi, :, sl])
            logf = jnp.log(lb + (1.0 - lb) * jax.nn.sigmoid(zf))
            k = (1.0 - lb) * jax.nn.sigmoid(-zf)
            if t_valid < t_total:
                logf = jnp.where(valid, logf, 0.0)
                k = jnp.where(valid, k, 0.0)
            o, st_new = _hgrn_chunk(q, k, zi_ref[bi, :, sl], logf, st_ref[bi, h], lev, dcode,
                                    min(C, t_valid))
            o_ref[bi, :, sl] = o * _rms_scale(o) * gw * _silu(zg_ref[bi, :, sl])
            st_ref[bi, h] = st_new

    @pl.when(c == pl.num_programs(1) - 1)
    def _():
        _write_layer(s_ref, layer, first, st_ref[...])


def _hgrn(z3, lb_logits, gw, s0, prev, layer, state_layer, chunk, bt, t_valid):
    bsz, t, _ = z3.shape
    nc = t // chunk
    first = prev is None
    kern = functools.partial(_hgrn_kernel, layer=layer, first=first, chunk=chunk, bt=bt, t_valid=t_valid,
                             t_total=t)
    st_shape = (bt, HG_HEADS, HG_DIM, HG_DIM)
    st_spec = (pl.BlockSpec((DEPTH,) + st_shape, lambda b, c: (0, b, 0, 0, 0)) if first else
               pl.BlockSpec((None,) + st_shape, lambda b, c: (layer, b, 0, 0, 0)))

    def zspec(col):
        return pl.BlockSpec((bt, chunk, HG_W), lambda b, c: (b, c, col // HG_W))

    in_specs = [
        zspec(COL_Q), zspec(COL_F), zspec(COL_I), zspec(COL_G),
        pl.BlockSpec((DEPTH, HG_W), lambda b, c: (0, 0)),
        pl.BlockSpec((1, HG_DIM), lambda b, c: (0, 0)),
        pl.BlockSpec((None, bt, HG_HEADS, HG_DIM, HG_DIM), lambda b, c: (state_layer, b, 0, 0, 0)),
    ]
    args = [z3, z3, z3, z3, lb_logits, gw, s0]
    aliases = {}
    if prev is not None:
        in_specs.append(pl.BlockSpec(memory_space=pl.ANY))
        args.append(prev)
        aliases = {len(args) - 1: 1}
    return pl.pallas_call(
        kern,
        grid=(bsz // bt, nc),
        in_specs=in_specs,
        out_specs=[
            pl.BlockSpec((bt, chunk, HG_W), lambda b, c: (b, c, 0)),
            st_spec,
        ],
        out_shape=[
            jax.ShapeDtypeStruct((bsz, t, HG_W), F32),
            jax.ShapeDtypeStruct((DEPTH, bsz, HG_HEADS, HG_DIM, HG_DIM), F32),
        ],
        scratch_shapes=[pltpu.VMEM((bt, HG_HEADS, HG_DIM, HG_DIM), F32)],
        input_output_aliases=aliases,
        compiler_params=_cparams(("parallel", "arbitrary")),
    )(*args)


def _s5_kernel(u_ref, bmat_ref, cmat_ref, a_ref, d_ref, wglu_ref, bglu_ref, hre0_ref, him0_ref,
               o_ref, hre_ref, him_ref, hs_ref, hst_ref, *, tc, bt, steps, lane_chunk):
    c = pl.program_id(1)

    def re_off(s):
        return 2 * s * S5_SLAB_H

    def im_off(s):
        return (2 * s + 1) * S5_SLAB_H

    @pl.when(c == 0)
    def _():
        for s in range(S5_SLABS):
            src = slice(s * S5_SLAB_H, (s + 1) * S5_SLAB_H)
            hst_ref[:, re_off(s):re_off(s) + S5_SLAB_H] = hre0_ref[:, src]
            hst_ref[:, im_off(s):im_off(s) + S5_SLAB_H] = him0_ref[:, src]

    u = pltpu.einshape("btd->tbd", u_ref[...]).reshape(tc * bt, S5_WIDTH)
    u16 = u.astype(BF16)
    for s in range(S5_SLABS):
        hs_ref[:, re_off(s):re_off(s + 1)] = _dot(u16[:, s * S5_SLAB_U:(s + 1) * S5_SLAB_U], bmat_ref[s])

    for s in range(S5_SLABS):
        for l0 in range(0, S5_SLAB_H, lane_chunk):
            re_sl = slice(re_off(s) + l0, re_off(s) + l0 + lane_chunk)
            im_sl = slice(im_off(s) + l0, im_off(s) + l0 + lane_chunk)
            are = a_ref[:, re_sl]
            aim = a_ref[:, im_sl]

            def step(t, carry, re_sl=re_sl, im_sl=im_sl, are=are, aim=aim):
                hre, him = carry
                r0 = pl.multiple_of(t * bt, bt)
                nre = are * hre - aim * him + hs_ref[pl.ds(r0, bt), re_sl]
                nim = are * him + aim * hre + hs_ref[pl.ds(r0, bt), im_sl]
                hs_ref[pl.ds(r0, bt), re_sl] = nre
                hs_ref[pl.ds(r0, bt), im_sl] = nim
                return nre, nim

            hre, him = lax.fori_loop(0, steps, step, (hst_ref[:, re_sl], hst_ref[:, im_sl]), unroll=True)
            hst_ref[:, re_sl] = hre
            hst_ref[:, im_sl] = him

    ys = [_dot(hs_ref[:, re_off(s):re_off(s + 1)].astype(BF16), cmat_ref[s]) for s in range(S5_SLABS)]
    y = jnp.concatenate(ys, axis=1) + d_ref[...] * u
    g = jax.nn.gelu(y)
    gl = _dot(g.astype(BF16), wglu_ref[...]) + bglu_ref[...]
    o = gl[:, :S5_WIDTH] * jax.nn.sigmoid(gl[:, S5_WIDTH:])
    o_ref[...] = pltpu.einshape("tbd->btd", o.reshape(tc, bt, S5_WIDTH))

    @pl.when(c == pl.num_programs(1) - 1)
    def _():
        for s in range(S5_SLABS):
            dst = slice(s * S5_SLAB_H, (s + 1) * S5_SLAB_H)
            hre_ref[:, dst] = hst_ref[:, re_off(s):re_off(s) + S5_SLAB_H]
            him_ref[:, dst] = hst_ref[:, im_off(s):im_off(s) + S5_SLAB_H]


def _s5(z3, bmat, cmat, a_vec, d_vec, wglu, bglu, hre0, him0, state_layer, tc, bt, t_valid):
    bsz, t, _ = z3.shape
    nc = t // tc
    if t_valid < t:
        assert nc == 1
    steps = tc if t_valid == t else t_valid
    lane_chunk = min(S5_SLAB_H, max(LANE, (8 * SUBLANE * LANE) // bt))
    kern = functools.partial(_s5_kernel, tc=tc, bt=bt, steps=steps, lane_chunk=lane_chunk)
    full = lambda shape: pl.BlockSpec(shape, lambda b, c: (0,) * len(shape))
    return pl.pallas_call(
        kern,
        grid=(bsz // bt, nc),
        in_specs=[
            pl.BlockSpec((bt, tc, S5_WIDTH), lambda b, c: (b, c, COL_U // S5_WIDTH)),
            full((S5_SLABS, S5_SLAB_U, 2 * S5_SLAB_H)),
            full((S5_SLABS, 2 * S5_SLAB_H, S5_SLAB_U)),
            full((1, 2 * S5_HW)),
            full((1, S5_WIDTH)),
            full((S5_WIDTH, 2 * S5_WIDTH)),
            full((1, 2 * S5_WIDTH)),
            pl.BlockSpec((None, bt, S5_HW), lambda b, c: (state_layer, b, 0)),
            pl.BlockSpec((None, bt, S5_HW), lambda b, c: (state_layer, b, 0)),
        ],
        out_specs=[
            pl.BlockSpec((bt, tc, S5_WIDTH), lambda b, c: (b, c, 0)),
            pl.BlockSpec((bt, S5_HW), lambda b, c: (b, 0)),
            pl.BlockSpec((bt, S5_HW), lambda b, c: (b, 0)),
        ],
        out_shape=[
            jax.ShapeDtypeStruct((bsz, t, S5_WIDTH), F32),
            jax.ShapeDtypeStruct((bsz, S5_HW), F32),
            jax.ShapeDtypeStruct((bsz, S5_HW), F32),
        ],
        scratch_shapes=[
            pltpu.VMEM((tc * bt, 2 * S5_HW), F32),
            pltpu.VMEM((bt, 2 * S5_HW), F32),
        ],
        compiler_params=_cparams(("parallel", "arbitrary")),
    )(z3, bmat, cmat, a_vec, d_vec, wglu, bglu, hre0, him0)


def _s5_params(a_re, a_im, b_re, b_im, c_re, c_im, log_dt):
    g = S5_GROUPS
    dt = jnp.exp(log_dt)[:, None]
    er = jnp.exp(a_re * dt)
    abr = er * jnp.cos(a_im * dt)
    abi = er * jnp.sin(a_im * dt)
    den = a_re * a_re + a_im * a_im
    xr = abr - 1.0
    cr = (xr * a_re + abi * a_im) / den
    ci = (abi * a_re - xr * a_im) / den
    bbr = cr[..., None] * b_re - ci[..., None] * b_im
    bbi = cr[..., None] * b_im + ci[..., None] * b_re
    sg = g // S5_SLABS
    eye = jnp.eye(sg, dtype=F32)
    slab = lambda x: x.reshape((S5_SLABS, sg) + x.shape[1:])
    bd_in = lambda x: jnp.einsum('sgnc,gh->sgchn', slab(x), eye).reshape(S5_SLABS, S5_SLAB_U, S5_SLAB_H)
    bd_out = lambda x: jnp.einsum('sgcn,gh->sgnhc', slab(x), eye).reshape(S5_SLABS, S5_SLAB_H, S5_SLAB_U)
    bmat = jnp.concatenate([bd_in(bbr), bd_in(bbi)], axis=2).astype(BF16)
    cmat = jnp.concatenate([bd_out(c_re), -bd_out(c_im)], axis=1).astype(BF16)
    lanes = lambda x: x.reshape(S5_SLABS, 1, S5_SLAB_H)
    a_vec = jnp.concatenate([lanes(abr), lanes(abi)], axis=2).reshape(1, 2 * S5_HW)
    return bmat, cmat, a_vec


def _ssd_kernel(xbc_ref, zz_ref, zdt_ref, cw_ref, cb_ref, dtb_ref, alog_ref, dsk_ref, nw_ref,
                conv0_ref, s0_ref, *rest, layer, first, chunk, bt, t_valid, t_total):
    y_ref, s_ref, cn_ref, st_ref, tail_ref = rest[-5:]
    c = pl.program_id(1)
    nc = t_total // chunk
    L = chunk
    TAIL = SUBLANE
    last_valid = t_valid - (nc - 1) * L

    @pl.when(c == 0)
    def _():
        st_ref[...] = s0_ref[...]
        tail_ref[...] = jnp.zeros((bt, TAIL, M2_CONV_DIM), F32)
        tail_ref[:, TAIL - (M2_CONV - 1):TAIL, :] = conv0_ref[...]

    causal = lax.broadcasted_iota(jnp.int32, (L, L), 0) >= lax.broadcasted_iota(jnp.int32, (L, L), 1)
    lo_lane = lax.broadcasted_iota(jnp.int32, (L, LANE), 1) < M2_HEADDIM
    lo_row = lax.broadcasted_iota(jnp.int32, (LANE, LANE), 0) < M2_HEADDIM
    neg_a = -jnp.exp(alog_ref[...])
    nw = nw_ref[...]
    if t_valid < t_total:
        valid = (c * L + lax.broadcasted_iota(jnp.int32, (L, LANE), 0)) < t_valid

    def pair_cols(a, h0):
        return jnp.where(lo_lane, a[:, h0:h0 + 1], a[:, h0 + 1:h0 + 2])

    for bi in range(bt):
        xbc = xbc_ref[bi]
        ext = jnp.concatenate([tail_ref[bi], xbc], axis=0)
        conv = cb_ref[...]
        for j in range(M2_CONV):
            o0 = TAIL - (M2_CONV - 1) + j
            conv = conv + cw_ref[j:j + 1, :] * ext[o0:o0 + L, :]
        tail_ref[bi] = xbc[L - TAIL:L, :]

        @pl.when(c == nc - 1)
        def _(ext=ext, bi=bi):
            _write_layer(cn_ref, layer, first, ext[TAIL + last_valid - (M2_CONV - 1):TAIL + last_valid, :], (bi,))

        act = _silu(conv)
        xs = act[:, :M2_INNER]
        bm = act[:, M2_INNER:M2_INNER + M2_BC].astype(BF16)
        cm = act[:, M2_INNER + M2_BC:].astype(BF16)

        dtr = zdt_ref[bi] + dtb_ref[...]
        dt = jnp.maximum(dtr, 0.0) + jnp.log1p(jnp.exp(-jnp.abs(dtr)))
        if t_valid < t_total:
            dt = jnp.where(valid, dt, 0.0)
        acs = _cumsum_rows(dt * neg_a)
        acs_t = acs.T
        acs_end = acs[L - 1:L, :]
        e_acs = jnp.exp(acs)
        e_end = jnp.exp(acs_end - acs)
        e_tot = jnp.exp(acs_end)

        ys = []
        for g in range(M2_NGROUPS):
            bg = bm[:, g * M2_STATE:(g + 1) * M2_STATE]
            cg = cm[:, g * M2_STATE:(g + 1) * M2_STATE]
            cb = _dot_nt(cg, bg)
            for pr in range(2):
                p = g * 2 + pr
                h0 = 2 * p
                xp = xs[:, p * LANE:(p + 1) * LANE]
                xdt = xp * pair_cols(dt, h0)
                ydiag = None
                for hh in range(2):
                    h = h0 + hh
                    lm = jnp.where(causal, jnp.exp(jnp.minimum(acs[:, h:h + 1] - acs_t[h:h + 1, :], 0.0)), 0.0)
                    mk = lo_lane if hh == 0 else jnp.logical_not(lo_lane)
                    part = _dot((cb * lm).astype(BF16), jnp.where(mk, xdt, 0.0).astype(BF16))
                    ydiag = part if ydiag is None else ydiag + part
                sp = st_ref[bi, p]
                yoff = pair_cols(e_acs, h0) * _dot_nt(cg, sp.astype(BF16))
                ys.append(ydiag + yoff + dsk_ref[:, p * LANE:(p + 1) * LANE] * xp)
                xe = (xdt * pair_cols(e_end, h0)).astype(BF16)
                scale = jnp.where(lo_row, e_tot[:, h0:h0 + 1], e_tot[:, h0 + 1:h0 + 2])
                st_ref[bi, p] = scale * sp + _dot_tn(xe, bg)

        y = jnp.concatenate(ys, axis=1) * _silu(zz_ref[bi])
        outs = []
        for g in range(M2_NGROUPS):
            sl = slice(g * M2_NORM_W, (g + 1) * M2_NORM_W)
            yg = y[:, sl]
            outs.append(yg * _rms_scale(yg) * nw[:, sl])
        y_ref[bi] = jnp.concatenate(outs, axis=1)

    @pl.when(c == nc - 1)
    def _():
        _write_layer(s_ref, layer, first, st_ref[...])


def _ssd(z3, cw, cb, dtb, alog, dsk, nw, conv0, s0, prev, layer, state_layer, chunk, bt, t_valid):
    bsz, t, _ = z3.shape
    nc = t // chunk
    assert chunk >= SUBLANE and t_valid - (nc - 1) * chunk >= 1
    first = prev is None
    kern = functools.partial(_ssd_kernel, layer=layer, first=first, chunk=chunk, bt=bt, t_valid=t_valid,
                             t_total=t)

    def stacked(shape):
        if first:
            return pl.BlockSpec((DEPTH,) + shape, lambda b, c: (0, b) + (0,) * (len(shape) - 1))
        return pl.BlockSpec((None,) + shape, lambda b, c: (layer, b) + (0,) * (len(shape) - 1))
    full = lambda shape: pl.BlockSpec(shape, lambda b, c: (0,) * len(shape))
    extra_specs, extra_args, aliases = [], [], {}
    if prev is not None:
        extra_specs = [pl.BlockSpec(memory_space=pl.ANY)] * 2
        extra_args = list(prev)
        aliases = {11: 1, 12: 2}
    return pl.pallas_call(
        kern,
        grid=(bsz // bt, nc),
        input_output_aliases=aliases,
        in_specs=[
            pl.BlockSpec((bt, chunk, M2_CONV_DIM), lambda b, c: (b, c, COL_XBC // M2_CONV_DIM)),
            pl.BlockSpec((bt, chunk, M2_INNER), lambda b, c: (b, c, COL_Z // M2_INNER)),
            pl.BlockSpec((bt, chunk, LANE), lambda b, c: (b, c, COL_DT // LANE)),
            full((M2_CONV, M2_CONV_DIM)),
            full((1, M2_CONV_DIM)),
            full((1, LANE)),
            full((1, LANE)),
            full((1, M2_INNER)),
            full((1, M2_INNER)),
            pl.BlockSpec((None, bt, M2_CONV - 1, M2_CONV_DIM), lambda b, c: (state_layer, b, 0, 0)),
            pl.BlockSpec((None, bt, M2_PAIRS, LANE, M2_STATE), lambda b, c: (state_layer, b, 0, 0, 0)),
        ] + extra_specs,
        out_specs=[
            pl.BlockSpec((bt, chunk, M2_INNER), lambda b, c: (b, c, 0)),
            stacked((bt, M2_PAIRS, LANE, M2_STATE)),
            stacked((bt, M2_CONV - 1, M2_CONV_DIM)),
        ],
        out_shape=[
            jax.ShapeDtypeStruct((bsz, t, M2_INNER), F32),
            jax.ShapeDtypeStruct((DEPTH, bsz, M2_PAIRS, LANE, M2_STATE), F32),
            jax.ShapeDtypeStruct((DEPTH, bsz, M2_CONV - 1, M2_CONV_DIM), F32),
        ],
        scratch_shapes=[
            pltpu.VMEM((bt, M2_PAIRS, LANE, M2_STATE), F32),
            pltpu.VMEM((bt, SUBLANE, M2_CONV_DIM), F32),
        ],
        compiler_params=_cparams(("parallel", "arbitrary")),
    )(z3, z3, z3, cw, cb, dtb, alog, dsk, nw, conv0, s0, *extra_args)


def _merge_kernel(oa_ref, ob_ref, oc_ref, ga_ref, gb_ref, gc_ref, h_ref, wa_ref, wb_ref, wc_ref, wo_ref, o_ref):
    m = jax.nn.sigmoid(ga_ref[...]) * _dot(oa_ref[...].astype(BF16), wa_ref[...])
    m = m + jax.nn.sigmoid(gb_ref[...]) * _dot(ob_ref[...].astype(BF16), wb_ref[...])
    m = m + jax.nn.sigmoid(gc_ref[...]) * _dot(oc_ref[...].astype(BF16), wc_ref[...])
    o_ref[...] = h_ref[...] + _dot(m.astype(BF16), wo_ref[...])


def _merge(oa, ob, oc, z, h, wa, wb, wc, wo, layer):
    n = h.shape[0]
    tm = _row_tile(n, 512)
    gblk = COL_GATE // D_MODEL
    row = lambda w, j=0: pl.BlockSpec((tm, w), lambda i: (i, j))
    full = lambda shape: pl.BlockSpec((None,) + shape, lambda i: (layer,) + (0,) * len(shape))
    return pl.pallas_call(
        _merge_kernel,
        grid=(n // tm,),
        in_specs=[
            row(HG_W), row(S5_WIDTH), row(M2_INNER),
            row(D_MODEL, gblk), row(D_MODEL, gblk + 1), row(D_MODEL, gblk + 2),
            row(D_MODEL),
            full((HG_W, D_MODEL)), full((S5_WIDTH, D_MODEL)), full((M2_INNER, D_MODEL)),
            full((D_MODEL, D_MODEL)),
        ],
        out_specs=row(D_MODEL),
        out_shape=jax.ShapeDtypeStruct((n, D_MODEL), F32),
        compiler_params=_cparams(("parallel",)),
    )(oa, ob, oc, z, z, z, h, wa, wb, wc, wo)


R_E0 = E_GROUPS
MOE_EPS = 2


def _moe_kernel(h_ref, nw_ref, wr_ref, br_ref, wg_ref, wu_ref, wd_ref, o_ref, xn_ref, comb_ref, acc_ref):
    e = pl.program_id(1)
    tm = h_ref.shape[0]
    lane = lax.broadcasted_iota(jnp.int32, (tm, LANE), 1)
    neg = -jnp.inf

    @pl.when(e == 0)
    def _():
        x = h_ref[...]
        xn = x * _rms_scale(x) * nw_ref[...]
        xn_ref[...] = xn.astype(BF16)
        logits = jnp.dot(xn, wr_ref[...], preferred_element_type=F32,
                         precision=lax.Precision.HIGHEST) + br_ref[...]
        gl = jnp.where(lane < E_GROUPS, logits, neg)
        gmax = jnp.max(gl, axis=1, keepdims=True)
        gidx = jnp.min(jnp.where(gl == gmax, lane, LANE), axis=1, keepdims=True)
        p_group = 1.0 / jnp.sum(jnp.exp(gl - gmax), axis=1, keepdims=True)
        in_grp = (lane >= R_E0) & (lane < R_E0 + N_EXPERTS) & (((lane - R_E0) >> 2) == gidx)
        ev = jnp.where(in_grp, logits, neg)
        v1 = jnp.max(ev, axis=1, keepdims=True)
        i1 = jnp.min(jnp.where(ev == v1, lane, LANE), axis=1, keepdims=True)
        ev2 = jnp.where(lane == i1, neg, ev)
        v2 = jnp.max(ev2, axis=1, keepdims=True)
        i2 = jnp.min(jnp.where(ev2 == v2, lane, LANE), axis=1, keepdims=True)
        e2 = jnp.exp(v2 - v1)
        w1 = 1.0 / (1.0 + e2)
        comb_ref[...] = jnp.where(lane == i1, w1 * p_group, jnp.where(lane == i2, e2 * w1 * p_group, 0.0))
        acc_ref[...] = jnp.zeros_like(acc_ref)

    xn = xn_ref[...]
    comb = comb_ref[...]
    out = acc_ref[...]
    for j in range(MOE_EPS):
        hid = _silu(_dot(xn, wg_ref[j].astype(BF16))) * _dot(xn, wu_ref[j].astype(BF16))
        ce = jnp.sum(jnp.where(lane == R_E0 + e * MOE_EPS + j, comb, 0.0), axis=1, keepdims=True)
        out = out + ce * _dot(hid.astype(BF16), wd_ref[j].astype(BF16))
    acc_ref[...] = out

    @pl.when(e == N_EXPERTS // MOE_EPS - 1)
    def _():
        o_ref[...] = h_ref[...] + out


def _moe(h, nw, wr, br, wg, wu, wd, layer):
    n = h.shape[0]
    tm = _row_tile(n, 1024)
    return pl.pallas_call(
        _moe_kernel,
        grid=(n // tm, N_EXPERTS // MOE_EPS),
        in_specs=[
            pl.BlockSpec((tm, D_MODEL), lambda i, e: (i, 0)),
            pl.BlockSpec((1, D_MODEL), lambda i, e: (0, 0)),
            pl.BlockSpec((D_MODEL, LANE), lambda i, e: (0, 0)),
            pl.BlockSpec((1, LANE), lambda i, e: (0, 0)),
            pl.BlockSpec((None, MOE_EPS, D_MODEL, E_FF), lambda i, e: (layer, e, 0, 0)),
            pl.BlockSpec((None, MOE_EPS, D_MODEL, E_FF), lambda i, e: (layer, e, 0, 0)),
            pl.BlockSpec((None, MOE_EPS, E_FF, D_MODEL), lambda i, e: (layer, e, 0, 0)),
        ],
        out_specs=pl.BlockSpec((tm, D_MODEL), lambda i, e: (i, 0)),
        out_shape=jax.ShapeDtypeStruct((n, D_MODEL), F32),
        scratch_shapes=[
            pltpu.VMEM((tm, D_MODEL), BF16),
            pltpu.VMEM((tm, LANE), F32),
            pltpu.VMEM((tm, D_MODEL), F32),
        ],
        compiler_params=_cparams(("parallel", "arbitrary")),
    )(h, nw, wr, br, wg, wu, wd)


def _ple_kernel(h_ref, p_ref, nw_ref, wg_ref, wp_ref, nf_ref, o_ref, *, final):
    x = h_ref[...]
    xn = (x * _rms_scale(x) * nw_ref[...]).astype(BF16)
    out = x + jax.nn.sigmoid(_dot(xn, wg_ref[...])) * _dot(p_ref[...].astype(BF16), wp_ref[...])
    o_ref[...] = out * _rms_scale(out) * nf_ref[...] if final else out


def _ple(h, p, nw, wg, wp, nf, layer):
    n = h.shape[0]
    tm = _row_tile(n, 1024)
    row = lambda w: pl.BlockSpec((tm, w), lambda i: (i, 0))
    full = lambda shape: pl.BlockSpec(shape, lambda i: (0,) * len(shape))
    lfull = lambda shape: pl.BlockSpec((None,) + shape, lambda i: (layer,) + (0,) * len(shape))
    return pl.pallas_call(
        functools.partial(_ple_kernel, final=layer == DEPTH - 1),
        grid=(n // tm,),
        in_specs=[row(D_MODEL), pl.BlockSpec((None, tm, PLE_DIM), lambda i: (layer, i, 0)),
                  full((1, D_MODEL)), lfull((D_MODEL, D_MODEL)),
                  lfull((PLE_DIM, D_MODEL)), full((1, D_MODEL))],
        out_specs=row(D_MODEL),
        out_shape=jax.ShapeDtypeStruct((n, D_MODEL), F32),
        compiler_params=_cparams(("parallel",)),
    )(h, p, nw, wg, wp, nf)


def _prep_weights(w):
    win_r = _win_prep(w['w_in'])
    lane_pad = lambda a: jnp.pad(a, ((0, 0), (0, LANE - a.shape[1])))
    s5 = [_s5_params(w['s5_a_re'][i], w['s5_a_im'][i], w['s5_b_re'][i], w['s5_b_im'][i],
                     w['s5_c_re'][i], w['s5_c_im'][i], w['s5_log_dt'][i]) for i in range(DEPTH)]
    wr = jnp.concatenate([w['w_rg'], w['w_re'],
                          jnp.zeros((DEPTH, D_MODEL, LANE - E_GROUPS - N_EXPERTS), F32)], axis=2)
    br = jnp.concatenate([w['b_rg'], w['b_re'],
                          jnp.zeros((DEPTH, LANE - E_GROUPS - N_EXPERTS), F32)], axis=1)
    return dict(
        win=win_r,
        s5=s5,
        s5_d=w['s5_d'].reshape(DEPTH, 1, S5_WIDTH),
        wglu=w['s5_w_glu'].astype(BF16),
        bglu=w['s5_b_glu'].reshape(DEPTH, 1, 2 * S5_WIDTH),
        dtb=lane_pad(w['m2_dt_bias']).reshape(DEPTH, 1, LANE),
        alog=lane_pad(w['m2_a_log']).reshape(DEPTH, 1, LANE),
        dsk=jnp.repeat(w['m2_d'], M2_HEADDIM, axis=1).reshape(DEPTH, 1, M2_INNER),
        wa=w['w_br_hg'].astype(BF16), wb=w['w_br_s5'].astype(BF16), wc=w['w_br_m2'].astype(BF16),
        wo=w['w_out'].astype(BF16),
        wr=wr, br=br.reshape(DEPTH, 1, LANE),
        wpg=w['w_ple_gate'].astype(BF16), wpp=w['w_ple_proj'].astype(BF16),
    )


def _trunk(x, p, states, w, pw, t_valid, cfg):
    bsz, t, _ = x.shape
    n = bsz * t
    st_hg, st_re, st_im, st_ssm, st_conv = states
    per_layer = st_hg.shape[0] == DEPTH
    st_re = st_re.reshape(st_re.shape[0], bsz, S5_HW)
    st_im = st_im.reshape(st_im.shape[0], bsz, S5_HW)
    st_ssm = st_ssm.reshape(st_ssm.shape[0], bsz, M2_PAIRS, LANE, M2_STATE)
    h = x.reshape(n, D_MODEL)
    new = []
    s_hg = None
    ssd_prev = None
    for i in range(DEPTH):
        sl = i if per_layer else 0
        z = _norm_matmul(h, w['norm_mix'][i].reshape(1, D_MODEL), pw['win'], i)
        z3 = z.reshape(bsz, t, Z_COLS)
        oa, s_hg = _hgrn(z3, w['hg_lb_logits'], w['hg_gnorm'][i].reshape(1, HG_DIM), st_hg, s_hg, i, sl,
                         cfg['hg_chunk'], cfg['hg_bt'], t_valid)
        bmat, cmat, a_vec = pw['s5'][i]
        ob, s_re, s_im = _s5(z3, bmat, cmat, a_vec, pw['s5_d'][i], pw['wglu'][i], pw['bglu'][i],
                             st_re, st_im, sl, cfg['s5_tc'], cfg['s5_bt'], t_valid)
        oc, s_ssm, conv_new = _ssd(z3, w['m2_conv_w'][i], w['m2_conv_b'][i].reshape(1, M2_CONV_DIM),
                                   pw['dtb'][i], pw['alog'][i], pw['dsk'][i],
                                   w['m2_norm'][i].reshape(1, M2_INNER), st_conv, st_ssm, ssd_prev, i, sl,
                                   cfg['m2_chunk'], cfg['m2_bt'], t_valid)
        ssd_prev = (s_ssm, conv_new)
        h = _merge(oa.reshape(n, HG_W), ob.reshape(n, S5_WIDTH), oc.reshape(n, M2_INNER), z, h,
                   pw['wa'], pw['wb'], pw['wc'], pw['wo'], i)
        h = _moe(h, w['norm_ffn'][i].reshape(1, D_MODEL), pw['wr'][i], pw['br'][i],
                 w['w_e_gate'], w['w_e_up'], w['w_e_down'], i)
        h = _ple(h, p.reshape(DEPTH, n, PLE_DIM), w['norm_ple'][i].reshape(1, D_MODEL),
                 pw['wpg'], pw['wpp'], w['norm_final'].reshape(1, D_MODEL), i)
        new.append((s_re.reshape(bsz, S5_GROUPS, S5_STATE), s_im.reshape(bsz, S5_GROUPS, S5_STATE)))
    s5_re, s5_im = (jnp.stack([nl[j] for nl in new]) for j in range(2))
    stacked = (s_hg, s5_re, s5_im, s_ssm.reshape(DEPTH, bsz, M2_HEADS, M2_HEADDIM, M2_STATE), conv_new)
    return h.reshape(bsz, t, D_MODEL), stacked


def _zero_states(bsz):
    return (jnp.zeros((1, bsz, HG_HEADS, HG_DIM, HG_DIM), F32),
            jnp.zeros((1, bsz, S5_GROUPS, S5_STATE), F32),
            jnp.zeros((1, bsz, S5_GROUPS, S5_STATE), F32),
            jnp.zeros((1, bsz, M2_HEADS, M2_HEADDIM, M2_STATE), F32),
            jnp.zeros((1, bsz, M2_CONV - 1, M2_CONV_DIM), F32))


def _pad_time(a, axis, t_pad):
    t = a.shape[axis]
    if t == t_pad:
        return a
    widths = [(0, 0)] * a.ndim
    widths[axis] = (0, t_pad - t)
    return jnp.pad(a, widths)


def kernel(x_prompt, x_sample, state_hgrn, state_s5_re, state_s5_im, state_ssm, state_conv,
           p_prompt, p_sample,
           norm_mix, w_in, hg_lb_logits, hg_gnorm, w_br_hg,
           s5_a_re, s5_a_im, s5_b_re, s5_b_im, s5_c_re, s5_c_im, s5_d, s5_log_dt, s5_w_glu, s5_b_glu, w_br_s5,
           m2_conv_w, m2_conv_b, m2_dt_bias, m2_a_log, m2_d, m2_norm, w_br_m2,
           w_out,
           norm_ffn, w_rg, b_rg, w_re, b_re, w_e_gate, w_e_up, w_e_down,
           norm_ple, w_ple_gate, w_ple_proj,
           norm_final):
    w = dict(norm_mix=norm_mix, w_in=w_in, hg_lb_logits=hg_lb_logits, hg_gnorm=hg_gnorm, w_br_hg=w_br_hg,
             s5_a_re=s5_a_re, s5_a_im=s5_a_im, s5_b_re=s5_b_re, s5_b_im=s5_b_im, s5_c_re=s5_c_re,
             s5_c_im=s5_c_im, s5_d=s5_d, s5_log_dt=s5_log_dt, s5_w_glu=s5_w_glu, s5_b_glu=s5_b_glu,
             w_br_s5=w_br_s5, m2_conv_w=m2_conv_w, m2_conv_b=m2_conv_b, m2_dt_bias=m2_dt_bias,
             m2_a_log=m2_a_log, m2_d=m2_d, m2_norm=m2_norm, w_br_m2=w_br_m2, w_out=w_out,
             norm_ffn=norm_ffn, w_rg=w_rg, b_rg=b_rg, w_re=w_re, b_re=b_re, w_e_gate=w_e_gate,
             w_e_up=w_e_up, w_e_down=w_e_down, norm_ple=norm_ple, w_ple_gate=w_ple_gate,
             w_ple_proj=w_ple_proj, norm_final=norm_final)
    pw = _prep_weights(w)

    bp, tp, _ = x_prompt.shape
    cfg_p = dict(hg_chunk=min(128, tp), hg_bt=min(2, bp), s5_tc=min(64, tp), s5_bt=SUBLANE,
                 m2_chunk=min(128, tp), m2_bt=1)
    y_p, st_p = _trunk(x_prompt, p_prompt, _zero_states(bp), w, pw, tp, cfg_p)

    bs, ts, _ = x_sample.shape
    ts_pad = -(-ts // SUBLANE) * SUBLANE
    cfg_s = dict(hg_chunk=ts_pad, hg_bt=min(8, bs), s5_tc=ts_pad, s5_bt=min(32, bs),
                 m2_chunk=ts_pad, m2_bt=min(4, bs))
    y_s, st_s = _trunk(_pad_time(x_sample, 1, ts_pad), _pad_time(p_sample, 2, ts_pad),
                       (state_hgrn, state_s5_re, state_s5_im, state_ssm, state_conv), w, pw, ts, cfg_s)
    return (y_p, y_s[:, :ts]) + st_p + st_s
```

```python
import functools

import jax
import jax.numpy as jnp
from jax import lax
from jax.experimental import pallas as pl
from jax.experimental.pallas import tpu as pltpu

F32 = jnp.float32
BF16 = jnp.bfloat16

D_MODEL = 1024
DEPTH = 2
PLE_DIM = 256
NORM_EPS = 1e-6

HG_HEADS = 4
HG_DIM = 128
HG_W = HG_HEADS * HG_DIM

S5_WIDTH = 512
S5_GROUP = 16
S5_GROUPS = S5_WIDTH // S5_GROUP
S5_STATE = 64
S5_HW = S5_GROUPS * S5_STATE
S5_SLABS = 2
S5_SLAB_U = S5_WIDTH // S5_SLABS
S5_SLAB_H = S5_HW // S5_SLABS

M2_INNER = 1024
M2_HEADDIM = 64
M2_HEADS = M2_INNER // M2_HEADDIM
M2_NGROUPS = 4
M2_STATE = 128
M2_CONV = 4
M2_BC = M2_NGROUPS * M2_STATE
M2_CONV_DIM = M2_INNER + 2 * M2_BC
M2_PAIRS = M2_HEADS // 2
M2_NORM_W = M2_INNER // M2_NGROUPS

N_BRANCH = 3
E_GROUPS = 4
E_PER_GROUP = 4
N_EXPERTS = E_GROUPS * E_PER_GROUP
E_FF = 256

LANE = 128
SUBLANE = 8

COL_XBC = 0
COL_Z = COL_XBC + M2_CONV_DIM
COL_GATE = COL_Z + M2_INNER
COL_Q = COL_GATE + N_BRANCH * D_MODEL
COL_F = COL_Q + HG_W
COL_I = COL_F + HG_W
COL_G = COL_I + HG_W
COL_U = COL_G + HG_W
COL_DT = COL_U + S5_WIDTH
Z_TN = 1280
Z_COLS = 7 * Z_TN
assert COL_DT + LANE <= Z_COLS and COL_DT % LANE == 0

_O_Q, _O_F, _O_I, _O_G, _O_U = 0, 512, 1024, 1536, 2048
_O_Z = 2560
_O_XBC = _O_Z + M2_INNER
_O_DT = _O_XBC + M2_CONV_DIM
_O_GATE = _O_DT + M2_HEADS
IN_COLS = _O_GATE + N_BRANCH * D_MODEL

VMEM_LIMIT = 56 * 1024 * 1024


def _cparams(sem):
    return pltpu.CompilerParams(dimension_semantics=sem, vmem_limit_bytes=VMEM_LIMIT)


def _rms_scale(x):
    return lax.rsqrt(jnp.mean(x * x, axis=-1, keepdims=True) + NORM_EPS)


def _silu(x):
    return x * jax.nn.sigmoid(x)


def _dot(a, b):
    return jnp.dot(a, b, preferred_element_type=F32)


def _dot_nt(a, b):
    return lax.dot_general(a, b, (((1,), (1,)), ((), ())), preferred_element_type=F32)


def _dot_tn(a, b):
    return lax.dot_general(a, b, (((0,), (0,)), ((), ())), preferred_element_type=F32)


def _cumsum_rows(x):
    n = x.shape[0]
    row = lax.broadcasted_iota(jnp.int32, x.shape, 0)
    s = 1
    while s < n:
        x = x + jnp.where(row >= s, pltpu.roll(x, s, 0), 0.0)
        s *= 2
    return x


def _write_layer(ref, layer, first, value, idx=()):
    if not first:
        ref[idx if idx else ...] = value
        return
    for l in range(DEPTH):
        ref[(l,) + idx] = value if l == layer else jnp.zeros_like(value)


def _row_tile(n, pref):
    t = min(pref, n)
    assert n % t == 0
    return t


def _win_prep_kernel(wt_ref, o_ref):
    w = wt_ref[...]
    lanes = w.shape[1]
    pieces = [
        w[_O_XBC:_O_XBC + M2_CONV_DIM], w[_O_Z:_O_Z + M2_INNER], w[_O_GATE:IN_COLS],
        w[:_O_Z], w[_O_DT:_O_DT + M2_HEADS], jnp.zeros((Z_COLS - IN_COLS, lanes), F32)]
    o_ref[...] = jnp.concatenate(pieces, axis=0).T.astype(BF16)


def _win_prep(w_in):
    tk = 256
    return pl.pallas_call(
        _win_prep_kernel,
        grid=(DEPTH, D_MODEL // tk),
        in_specs=[pl.BlockSpec((None, IN_COLS, tk), lambda l, i: (l, 0, i))],
        out_specs=pl.BlockSpec((None, tk, Z_COLS), lambda l, i: (l, i, 0)),
        out_shape=jax.ShapeDtypeStruct((DEPTH, D_MODEL, Z_COLS), BF16),
        compiler_params=_cparams(("parallel", "parallel")),
    )(jnp.swapaxes(w_in, 1, 2))


def _norm_matmul_kernel(x_ref, nw_ref, w_ref, o_ref, xn_ref):
    @pl.when(pl.program_id(1) == 0)
    def _():
        x = x_ref[...]
        xn_ref[...] = (x * _rms_scale(x) * nw_ref[...]).astype(BF16)

    o_ref[...] = _dot(xn_ref[...], w_ref[...])


def _norm_matmul(h, nw, w, layer):
    n = h.shape[0]
    cols = w.shape[2]
    tm = _row_tile(n, 1024)
    tn = Z_TN
    return pl.pallas_call(
        _norm_matmul_kernel,
        grid=(n // tm, cols // tn),
        in_specs=[
            pl.BlockSpec((tm, D_MODEL), lambda i, j: (i, 0)),
            pl.BlockSpec((1, D_MODEL), lambda i, j: (0, 0)),
            pl.BlockSpec((None, D_MODEL, tn), lambda i, j: (layer, 0, j)),
        ],
        out_specs=pl.BlockSpec((tm, tn), lambda i, j: (i, j)),
        out_shape=jax.ShapeDtypeStruct((n, cols), F32),
        scratch_shapes=[pltpu.VMEM((tm, D_MODEL), BF16)],
        compiler_params=_cparams(("parallel", "arbitrary")),
    )(h, nw, w)


def _hgrn_chunk(q, k, v, logf, st, lev, dcode, rows_used):
    C = q.shape[0]
    b = _cumsum_rows(logf)
    b_end = b[C - 1:C, :]
    o = _dot((q * jnp.exp(b)).astype(BF16), st.astype(BF16))
    v16 = v.astype(BF16)

    def shifted_score(j):
        if j == 0:
            return jnp.sum(q * k, axis=1, keepdims=True)
        kr = pltpu.roll(k, j, 0)
        br = pltpu.roll(b, j, 0)
        return jnp.sum(q * kr * jnp.exp(jnp.minimum(b - br, 0.0)), axis=1, keepdims=True)

    if C == SUBLANE:
        row = lax.broadcasted_iota(jnp.int32, (C, HG_DIM), 0)
        o = o + shifted_score(0) * v
        for j in range(1, min(SUBLANE, rows_used)):
            o = o + jnp.where(row >= j, shifted_score(j), 0.0) * pltpu.roll(v, j, 0)
        ke = k * jnp.exp(b_end - b)
        decay = jnp.transpose(jnp.broadcast_to(jnp.exp(b_end), (SUBLANE, HG_DIM)))[:, 0:1]
        return o, st * decay + _dot_tn(ke.astype(BF16), v16)

    sc = jnp.where(dcode == 0, shifted_score(0), 0.0)
    for j in range(1, SUBLANE):
        sc = jnp.where(dcode == j, shifted_score(j), sc)

    m = C // 2
    while m >= SUBLANE:
        parts = []
        for p0 in range(0, C, 2 * m):
            parts.append(jnp.broadcast_to(b[p0 + m - 1:p0 + m, :], (2 * m, HG_DIM)))
        r = parts[0] if len(parts) == 1 else jnp.concatenate(parts, axis=0)
        qe = q * jnp.exp(jnp.minimum(b - r, 0.0))
        ke = k * jnp.exp(jnp.minimum(r - b, 0.0))
        pm = _dot_nt(qe.astype(BF16), ke.astype(BF16))
        sc = jnp.where((lev >> (m.bit_length() - 1)) == 1, pm, sc)
        m //= 2

    o = o + _dot(sc.astype(BF16), v16)
    ke = k * jnp.exp(b_end - b)
    decay = jnp.transpose(jnp.broadcast_to(jnp.exp(b_end), (SUBLANE, HG_DIM)))[:, 0:1]
    st_new = st * decay + _dot_tn(ke.astype(BF16), v16)
    return o, st_new


def _hgrn_kernel(zq_ref, zf_ref, zi_ref, zg_ref, lbl_ref, gw_ref, s0_ref, *rest,
                 layer, first, chunk, bt, t_valid, t_total):
    o_ref, s_ref, st_ref = rest[-3:]
    c = pl.program_id(1)
    C = chunk

    @pl.when(c == 0)
    def _():
        st_ref[...] = s0_ref[...]

    lg = lbl_ref[...]
    e = jnp.exp(lg - jnp.max(lg, axis=0, keepdims=True))
    prob = e / jnp.sum(e, axis=0, keepdims=True)
    lb_all = jnp.zeros((1, HG_W), F32)
    for j in range(1, layer + 1):
        lb_all = lb_all + prob[j:j + 1]

    rr = lax.broadcasted_iota(jnp.int32, (C, C), 0)
    cc = lax.broadcasted_iota(jnp.int32, (C, C), 1)
    xr = rr ^ cc
    lev = jnp.where(rr > cc, xr, -1)
    dcode = jnp.where(xr < SUBLANE, rr - cc, -1)
    if t_valid < t_total:
        valid = (c * C + lax.broadcasted_iota(jnp.int32, (C, HG_DIM), 0)) < t_valid
    gw = gw_ref[...]

    for bi in range(bt):
        for h in range(HG_HEADS):
            sl = slice(h * HG_DIM, (h + 1) * HG_DIM)
            lb = lb_all[:, sl]
            zf = zf_ref[bi, :, sl]
            q = _silu(zq_ref[bi, :, sl])
            logf = jnp.log(lb + (1.0 - lb) * jax.nn.sigmoid(zf))
            k = (1.0 - lb) * jax.nn.sigmoid(-zf)
            if t_valid < t_total:
                logf = jnp.where(valid, logf, 0.0)
                k = jnp.where(valid, k, 0.0)
            o, st_new = _hgrn_chunk(q, k, zi_ref[bi, :, sl], logf, st_ref[bi, h], lev, dcode,
                                    min(C, t_valid))
            o_ref[bi, :, sl] = o * _rms_scale(o) * gw * _silu(zg_ref[bi, :, sl])
            st_ref[bi, h] = st_new

    @pl.when(c == pl.num_programs(1) - 1)
    def _():
        _write_layer(s_ref, layer, first, st_ref[...])


def _hgrn(z3, lb_logits, gw, s0, prev, layer, state_layer, chunk, bt, t_valid):
    bsz, t, _ = z3.shape
    nc = t // chunk
    first = prev is None
    kern = functools.partial(_hgrn_kernel, layer=layer, first=first, chunk=chunk, bt=bt, t_valid=t_valid,
                             t_total=t)
    st_shape = (bt, HG_HEADS, HG_DIM, HG_DIM)
    st_spec = (pl.BlockSpec((DEPTH,) + st_shape, lambda b, c: (0, b, 0, 0, 0)) if first else
               pl.BlockSpec((None,) + st_shape, lambda b, c: (layer, b, 0, 0, 0)))

    def zspec(col):
        return pl.BlockSpec((bt, chunk, HG_W), lambda b, c: (b, c, col // HG_W))

    in_specs = [
        zspec(COL_Q), zspec(COL_F), zspec(COL_I), zspec(COL_G),
        pl.BlockSpec((DEPTH, HG_W), lambda b, c: (0, 0)),
        pl.BlockSpec((1, HG_DIM), lambda b, c: (0, 0)),
        pl.BlockSpec((None, bt, HG_HEADS, HG_DIM, HG_DIM), lambda b, c: (state_layer, b, 0, 0, 0)),
    ]
    args = [z3, z3, z3, z3, lb_logits, gw, s0]
    aliases = {}
    if prev is not None:
        in_specs.append(pl.BlockSpec(memory_space=pl.ANY))
        args.append(prev)
        aliases = {len(args) - 1: 1}
    return pl.pallas_call(
        kern,
        grid=(bsz // bt, nc),
        in_specs=in_specs,
        out_specs=[
            pl.BlockSpec((bt, chunk, HG_W), lambda b, c: (b, c, 0)),
            st_spec,
        ],
        out_shape=[
            jax.ShapeDtypeStruct((bsz, t, HG_W), F32),
            jax.ShapeDtypeStruct((DEPTH, bsz, HG_HEADS, HG_DIM, HG_DIM), F32),
        ],
        scratch_shapes=[pltpu.VMEM((bt, HG_HEADS, HG_DIM, HG_DIM), F32)],
        input_output_aliases=aliases,
        compiler_params=_cparams(("parallel", "arbitrary")),
    )(*args)


def _s5_kernel(u_ref, bmat_ref, cmat_ref, a_ref, d_ref, wglu_ref, bglu_ref, hre0_ref, him0_ref,
               o_ref, hre_ref, him_ref, hs_ref, hst_ref, *, tc, bt, steps, lane_chunk):
    c = pl.program_id(1)

    def re_off(s):
        return 2 * s * S5_SLAB_H

    def im_off(s):
        return (2 * s + 1) * S5_SLAB_H

    @pl.when(c == 0)
    def _():
        for s in range(S5_SLABS):
            src = slice(s * S5_SLAB_H, (s + 1) * S5_SLAB_H)
            hst_ref[:, re_off(s):re_off(s) + S5_SLAB_H] = hre0_ref[:, src]
            hst_ref[:, im_off(s):im_off(s) + S5_SLAB_H] = him0_ref[:, src]

    u = pltpu.einshape("btd->tbd", u_ref[...]).reshape(tc * bt, S5_WIDTH)
    u16 = u.astype(BF16)
    for s in range(S5_SLABS):
        hs_ref[:, re_off(s):re_off(s + 1)] = _dot(u16[:, s * S5_SLAB_U:(s + 1) * S5_SLAB_U], bmat_ref[s])

    for s in range(S5_SLABS):
        for l0 in range(0, S5_SLAB_H, lane_chunk):
            re_sl = slice(re_off(s) + l0, re_off(s) + l0 + lane_chunk)
            im_sl = slice(im_off(s) + l0, im_off(s) + l0 + lane_chunk)
            are = a_ref[:, re_sl]
            aim = a_ref[:, im_sl]

            def step(t, carry, re_sl=re_sl, im_sl=im_sl, are=are, aim=aim):
                hre, him = carry
                r0 = pl.multiple_of(t * bt, bt)
                nre = are * hre - aim * him + hs_ref[pl.ds(r0, bt), re_sl]
                nim = are * him + aim * hre + hs_ref[pl.ds(r0, bt), im_sl]
                hs_ref[pl.ds(r0, bt), re_sl] = nre
                hs_ref[pl.ds(r0, bt), im_sl] = nim
                return nre, nim

            hre, him = lax.fori_loop(0, steps, step, (hst_ref[:, re_sl], hst_ref[:, im_sl]), unroll=True)
            hst_ref[:, re_sl] = hre
            hst_ref[:, im_sl] = him

    ys = [_dot(hs_ref[:, re_off(s):re_off(s + 1)].astype(BF16), cmat_ref[s]) for s in range(S5_SLABS)]
    y = jnp.concatenate(ys, axis=1) + d_ref[...] * u
    g = jax.nn.gelu(y)
    gl = _dot(g.astype(BF16), wglu_ref[...]) + bglu_ref[...]
    o = gl[:, :S5_WIDTH] * jax.nn.sigmoid(gl[:, S5_WIDTH:])
    o_ref[...] = pltpu.einshape("tbd->btd", o.reshape(tc, bt, S5_WIDTH))

    @pl.when(c == pl.num_programs(1) - 1)
    def _():
        for s in range(S5_SLABS):
            dst = slice(s * S5_SLAB_H, (s + 1) * S5_SLAB_H)
            hre_ref[:, dst] = hst_ref[:, re_off(s):re_off(s) + S5_SLAB_H]
            him_ref[:, dst] = hst_ref[:, im_off(s):im_off(s) + S5_SLAB_H]


def _s5(z3, bmat, cmat, a_vec, d_vec, wglu, bglu, hre0, him0, state_layer, tc, bt, t_valid):
    bsz, t, _ = z3.shape
    nc = t // tc
    if t_valid < t:
        assert nc == 1
    steps = tc if t_valid == t else t_valid
    lane_chunk = min(S5_SLAB_H, max(LANE, (8 * SUBLANE * LANE) // bt))
    kern = functools.partial(_s5_kernel, tc=tc, bt=bt, steps=steps, lane_chunk=lane_chunk)
    full = lambda shape: pl.BlockSpec(shape, lambda b, c: (0,) * len(shape))
    return pl.pallas_call(
        kern,
        grid=(bsz // bt, nc),
        in_specs=[
            pl.BlockSpec((bt, tc, S5_WIDTH), lambda b, c: (b, c, COL_U // S5_WIDTH)),
            full((S5_SLABS, S5_SLAB_U, 2 * S5_SLAB_H)),
            full((S5_SLABS, 2 * S5_SLAB_H, S5_SLAB_U)),
            full((1, 2 * S5_HW)),
            full((1, S5_WIDTH)),
            full((S5_WIDTH, 2 * S5_WIDTH)),
            full((1, 2 * S5_WIDTH)),
            pl.BlockSpec((None, bt, S5_HW), lambda b, c: (state_layer, b, 0)),
            pl.BlockSpec((None, bt, S5_HW), lambda b, c: (state_layer, b, 0)),
        ],
        out_specs=[
            pl.BlockSpec((bt, tc, S5_WIDTH), lambda b, c: (b, c, 0)),
            pl.BlockSpec((bt, S5_HW), lambda b, c: (b, 0)),
            pl.BlockSpec((bt, S5_HW), lambda b, c: (b, 0)),
        ],
        out_shape=[
            jax.ShapeDtypeStruct((bsz, t, S5_WIDTH), F32),
            jax.ShapeDtypeStruct((bsz, S5_HW), F32),
            jax.ShapeDtypeStruct((bsz, S5_HW), F32),
        ],
        scratch_shapes=[
            pltpu.VMEM((tc * bt, 2 * S5_HW), F32),
            pltpu.VMEM((bt, 2 * S5_HW), F32),
        ],
        compiler_params=_cparams(("parallel", "arbitrary")),
    )(z3, bmat, cmat, a_vec, d_vec, wglu, bglu, hre0, him0)


def _s5_params(a_re, a_im, b_re, b_im, c_re, c_im, log_dt):
    g = S5_GROUPS
    dt = jnp.exp(log_dt)[:, None]
    er = jnp.exp(a_re * dt)
    abr = er * jnp.cos(a_im * dt)
    abi = er * jnp.sin(a_im * dt)
    den = a_re * a_re + a_im * a_im
    xr = abr - 1.0
    cr = (xr * a_re + abi * a_im) / den
    ci = (abi * a_re - xr * a_im) / den
    bbr = cr[..., None] * b_re - ci[..., None] * b_im
    bbi = cr[..., None] * b_im + ci[..., None] * b_re
    sg = g // S5_SLABS
    eye = jnp.eye(sg, dtype=F32)
    slab = lambda x: x.reshape((S5_SLABS, sg) + x.shape[1:])
    bd_in = lambda x: jnp.einsum('sgnc,gh->sgchn', slab(x), eye).reshape(S5_SLABS, S5_SLAB_U, S5_SLAB_H)
    bd_out = lambda x: jnp.einsum('sgcn,gh->sgnhc', slab(x), eye).reshape(S5_SLABS, S5_SLAB_H, S5_SLAB_U)
    bmat = jnp.concatenate([bd_in(bbr), bd_in(bbi)], axis=2).astype(BF16)
    cmat = jnp.concatenate([bd_out(c_re), -bd_out(c_im)], axis=1).astype(BF16)
    lanes = lambda x: x.reshape(S5_SLABS, 1, S5_SLAB_H)
    a_vec = jnp.concatenate([lanes(abr), lanes(abi)], axis=2).reshape(1, 2 * S5_HW)
    return bmat, cmat, a_vec


def _ssd_kernel(xbc_ref, zz_ref, zdt_ref, cw_ref, cb_ref, dtb_ref, alog_ref, dsk_ref, nw_ref,
                conv0_ref, s0_ref, *rest, layer, first, chunk, bt, t_valid, t_total):
    y_ref, s_ref, cn_ref, st_ref, tail_ref = rest[-5:]
    c = pl.program_id(1)
    nc = t_total // chunk
    L = chunk
    TAIL = SUBLANE
    last_valid = t_valid - (nc - 1) * L

    @pl.when(c == 0)
    def _():
        st_ref[...] = s0_ref[...]
        tail_ref[...] = jnp.zeros((bt, TAIL, M2_CONV_DIM), F32)
        tail_ref[:, TAIL - (M2_CONV - 1):TAIL, :] = conv0_ref[...]

    causal = lax.broadcasted_iota(jnp.int32, (L, L), 0) >= lax.broadcasted_iota(jnp.int32, (L, L), 1)
    lo_lane = lax.broadcasted_iota(jnp.int32, (L, LANE), 1) < M2_HEADDIM
    lo_row = lax.broadcasted_iota(jnp.int32, (LANE, LANE), 0) < M2_HEADDIM
    neg_a = -jnp.exp(alog_ref[...])
    nw = nw_ref[...]
    if t_valid < t_total:
        valid = (c * L + lax.broadcasted_iota(jnp.int32, (L, LANE), 0)) < t_valid

    def pair_cols(a, h0):
        return jnp.where(lo_lane, a[:, h0:h0 + 1], a[:, h0 + 1:h0 + 2])

    for bi in range(bt):
        xbc = xbc_ref[bi]
        ext = jnp.concatenate([tail_ref[bi], xbc], axis=0)
        conv = cb_ref[...]
        for j in range(M2_CONV):
            o0 = TAIL - (M2_CONV - 1) + j
            conv = conv + cw_ref[j:j + 1, :] * ext[o0:o0 + L, :]
        tail_ref[bi] = xbc[L - TAIL:L, :]

        @pl.when(c == nc - 1)
        def _(ext=ext, bi=bi):
            _write_layer(cn_ref, layer, first, ext[TAIL + last_valid - (M2_CONV - 1):TAIL + last_valid, :], (bi,))

        act = _silu(conv)
        xs = act[:, :M2_INNER]
        bm = act[:, M2_INNER:M2_INNER + M2_BC].astype(BF16)
        cm = act[:, M2_INNER + M2_BC:].astype(BF16)

        dtr = zdt_ref[bi] + dtb_ref[...]
        dt = jnp.maximum(dtr, 0.0) + jnp.log1p(jnp.exp(-jnp.abs(dtr)))
        if t_valid < t_total:
            dt = jnp.where(valid, dt, 0.0)
        acs = _cumsum_rows(dt * neg_a)
        acs_t = acs.T
        acs_end = acs[L - 1:L, :]
        e_acs = jnp.exp(acs)
        e_end = jnp.exp(acs_end - acs)
        e_tot = jnp.exp(acs_end)

        ys = []
        for g in range(M2_NGROUPS):
            bg = bm[:, g * M2_STATE:(g + 1) * M2_STATE]
            cg = cm[:, g * M2_STATE:(g + 1) * M2_STATE]
            cb = _dot_nt(cg, bg)
            for pr in range(2):
                p = g * 2 + pr
                h0 = 2 * p
                xp = xs[:, p * LANE:(p + 1) * LANE]
                xdt = xp * pair_cols(dt, h0)
                ydiag = None
                for hh in range(2):
                    h = h0 + hh
                    lm = jnp.where(causal, jnp.exp(jnp.minimum(acs[:, h:h + 1] - acs_t[h:h + 1, :], 0.0)), 0.0)
                    mk = lo_lane if hh == 0 else jnp.logical_not(lo_lane)
                    part = _dot((cb * lm).astype(BF16), jnp.where(mk, xdt, 0.0).astype(BF16))
                    ydiag = part if ydiag is None else ydiag + part
                sp = st_ref[bi, p]
                yoff = pair_cols(e_acs, h0) * _dot_nt(cg, sp.astype(BF16))
                ys.append(ydiag + yoff + dsk_ref[:, p * LANE:(p + 1) * LANE] * xp)
                xe = (xdt * pair_cols(e_end, h0)).astype(BF16)
                scale = jnp.where(lo_row, e_tot[:, h0:h0 + 1], e_tot[:, h0 + 1:h0 + 2])
                st_ref[bi, p] = scale * sp + _dot_tn(xe, bg)

        y = jnp.concatenate(ys, axis=1) * _silu(zz_ref[bi])
        outs = []
        for g in range(M2_NGROUPS):
            sl = slice(g * M2_NORM_W, (g + 1) * M2_NORM_W)
            yg = y[:, sl]
            outs.append(yg * _rms_scale(yg) * nw[:, sl])
        y_ref[bi] = jnp.concatenate(outs, axis=1)

    @pl.when(c == nc - 1)
    def _():
        _write_layer(s_ref, layer, first, st_ref[...])


def _ssd(z3, cw, cb, dtb, alog, dsk, nw, conv0, s0, prev, layer, state_layer, chunk, bt, t_valid):
    bsz, t, _ = z3.shape
    nc = t // chunk
    assert chunk >= SUBLANE and t_valid - (nc - 1) * chunk >= 1
    first = prev is None
    kern = functools.partial(_ssd_kernel, layer=layer, first=first, chunk=chunk, bt=bt, t_valid=t_valid,
                             t_total=t)

    def stacked(shape):
        if first:
            return pl.BlockSpec((DEPTH,) + shape, lambda b, c: (0, b) + (0,) * (len(shape) - 1))
        return pl.BlockSpec((None,) + shape, lambda b, c: (layer, b) + (0,) * (len(shape) - 1))
    full = lambda shape: pl.BlockSpec(shape, lambda b, c: (0,) * len(shape))
    extra_specs, extra_args, aliases = [], [], {}
    if prev is not None:
        extra_specs = [pl.BlockSpec(memory_space=pl.ANY)] * 2
        extra_args = list(prev)
        aliases = {11: 1, 12: 2}
    return pl.pallas_call(
        kern,
        grid=(bsz // bt, nc),
        input_output_aliases=aliases,
        in_specs=[
            pl.BlockSpec((bt, chunk, M2_CONV_DIM), lambda b, c: (b, c, COL_XBC // M2_CONV_DIM)),
            pl.BlockSpec((bt, chunk, M2_INNER), lambda b, c: (b, c, COL_Z // M2_INNER)),
            pl.BlockSpec((bt, chunk, LANE), lambda b, c: (b, c, COL_DT // LANE)),
            full((M2_CONV, M2_CONV_DIM)),
            full((1, M2_CONV_DIM)),
            full((1, LANE)),
            full((1, LANE)),
            full((1, M2_INNER)),
            full((1, M2_INNER)),
            pl.BlockSpec((None, bt, M2_CONV - 1, M2_CONV_DIM), lambda b, c: (state_layer, b, 0, 0)),
            pl.BlockSpec((None, bt, M2_PAIRS, LANE, M2_STATE), lambda b, c: (state_layer, b, 0, 0, 0)),
        ] + extra_specs,
        out_specs=[
            pl.BlockSpec((bt, chunk, M2_INNER), lambda b, c: (b, c, 0)),
            stacked((bt, M2_PAIRS, LANE, M2_STATE)),
            stacked((bt, M2_CONV - 1, M2_CONV_DIM)),
        ],
        out_shape=[
            jax.ShapeDtypeStruct((bsz, t, M2_INNER), F32),
            jax.ShapeDtypeStruct((DEPTH, bsz, M2_PAIRS, LANE, M2_STATE), F32),
            jax.ShapeDtypeStruct((DEPTH, bsz, M2_CONV - 1, M2_CONV_DIM), F32),
        ],
        scratch_shapes=[
            pltpu.VMEM((bt, M2_PAIRS, LANE, M2_STATE), F32),
            pltpu.VMEM((bt, SUBLANE, M2_CONV_DIM), F32),
        ],
        compiler_params=_cparams(("parallel", "arbitrary")),
    )(z3, z3, z3, cw, cb, dtb, alog, dsk, nw, conv0, s0, *extra_args)


def _merge_kernel(oa_ref, ob_ref, oc_ref, ga_ref, gb_ref, gc_ref, h_ref, wa_ref, wb_ref, wc_ref, wo_ref, o_ref):
    m = jax.nn.sigmoid(ga_ref[...]) * _dot(oa_ref[...].astype(BF16), wa_ref[...])
    m = m + jax.nn.sigmoid(gb_ref[...]) * _dot(ob_ref[...].astype(BF16), wb_ref[...])
    m = m + jax.nn.sigmoid(gc_ref[...]) * _dot(oc_ref[...].astype(BF16), wc_ref[...])
    o_ref[...] = h_ref[...] + _dot(m.astype(BF16), wo_ref[...])


def _merge(oa, ob, oc, z, h, wa, wb, wc, wo, layer):
    n = h.shape[0]
    tm = _row_tile(n, 512)
    gblk = COL_GATE // D_MODEL
    row = lambda w, j=0: pl.BlockSpec((tm, w), lambda i: (i, j))
    full = lambda shape: pl.BlockSpec((None,) + shape, lambda i: (layer,) + (0,) * len(shape))
    return pl.pallas_call(
        _merge_kernel,
        grid=(n // tm,),
        in_specs=[
            row(HG_W), row(S5_WIDTH), row(M2_INNER),
            row(D_MODEL, gblk), row(D_MODEL, gblk + 1), row(D_MODEL, gblk + 2),
            row(D_MODEL),
            full((HG_W, D_MODEL)), full((S5_WIDTH, D_MODEL)), full((M2_INNER, D_MODEL)),
            full((D_MODEL, D_MODEL)),
        ],
        out_specs=row(D_MODEL),
        out_shape=jax.ShapeDtypeStruct((n, D_MODEL), F32),
        compiler_params=_cparams(("parallel",)),
    )(oa, ob, oc, z, z, z, h, wa, wb, wc, wo)


R_E0 = E_GROUPS
MOE_EPS = 2


def _moe_kernel(h_ref, nw_ref, wr_ref, br_ref, wg_ref, wu_ref, wd_ref, o_ref, xn_ref, comb_ref, acc_ref):
    e = pl.program_id(1)
    tm = h_ref.shape[0]
    lane = lax.broadcasted_iota(jnp.int32, (tm, LANE), 1)
    neg = -jnp.inf

    @pl.when(e == 0)
    def _():
        x = h_ref[...]
        xn = x * _rms_scale(x) * nw_ref[...]
        xh = xn.astype(BF16)
        xn_ref[...] = xh
        xl = (xn - xh.astype(F32)).astype(BF16)
        wr = wr_ref[...]
        wrh = wr.astype(BF16)
        wrl = (wr - wrh.astype(F32)).astype(BF16)
        logits = _dot(xh, wrh) + (_dot(xl, wrh) + _dot(xh, wrl)) + br_ref[...]
        gl = jnp.where(lane < E_GROUPS, logits, neg)
        gmax = jnp.max(gl, axis=1, keepdims=True)
        gidx = jnp.min(jnp.where(gl == gmax, lane, LANE), axis=1, keepdims=True)
        p_group = 1.0 / jnp.sum(jnp.exp(gl - gmax), axis=1, keepdims=True)
        in_grp = (lane >= R_E0) & (lane < R_E0 + N_EXPERTS) & (((lane - R_E0) >> 2) == gidx)
        ev = jnp.where(in_grp, logits, neg)
        v1 = jnp.max(ev, axis=1, keepdims=True)
        i1 = jnp.min(jnp.where(ev == v1, lane, LANE), axis=1, keepdims=True)
        ev2 = jnp.where(lane == i1, neg, ev)
        v2 = jnp.max(ev2, axis=1, keepdims=True)
        i2 = jnp.min(jnp.where(ev2 == v2, lane, LANE), axis=1, keepdims=True)
        e2 = jnp.exp(v2 - v1)
        w1 = 1.0 / (1.0 + e2)
        comb_ref[...] = jnp.where(lane == i1, w1 * p_group, jnp.where(lane == i2, e2 * w1 * p_group, 0.0))
        acc_ref[...] = jnp.zeros_like(acc_ref)

    xn = xn_ref[...]
    comb = comb_ref[...]
    out = acc_ref[...]
    for j in range(MOE_EPS):
        hid = _silu(_dot(xn, wg_ref[j].astype(BF16))) * _dot(xn, wu_ref[j].astype(BF16))
        ce = jnp.sum(jnp.where(lane == R_E0 + e * MOE_EPS + j, comb, 0.0), axis=1, keepdims=True)
        out = out + ce * _dot(hid.astype(BF16), wd_ref[j].astype(BF16))
    acc_ref[...] = out

    @pl.when(e == N_EXPERTS // MOE_EPS - 1)
    def _():
        o_ref[...] = h_ref[...] + out


def _moe(h, nw, wr, br, wg, wu, wd, layer):
    n = h.shape[0]
    tm = _row_tile(n, 1024)
    return pl.pallas_call(
        _moe_kernel,
        grid=(n // tm, N_EXPERTS // MOE_EPS),
        in_specs=[
            pl.BlockSpec((tm, D_MODEL), lambda i, e: (i, 0)),
            pl.BlockSpec((1, D_MODEL), lambda i, e: (0, 0)),
            pl.BlockSpec((D_MODEL, LANE), lambda i, e: (0, 0)),
            pl.BlockSpec((1, LANE), lambda i, e: (0, 0)),
            pl.BlockSpec((None, MOE_EPS, D_MODEL, E_FF), lambda i, e: (layer, e, 0, 0)),
            pl.BlockSpec((None, MOE_EPS, D_MODEL, E_FF), lambda i, e: (layer, e, 0, 0)),
            pl.BlockSpec((None, MOE_EPS, E_FF, D_MODEL), lambda i, e: (layer, e, 0, 0)),
        ],
        out_specs=pl.BlockSpec((tm, D_MODEL), lambda i, e: (i, 0)),
        out_shape=jax.ShapeDtypeStruct((n, D_MODEL), F32),
        scratch_shapes=[
            pltpu.VMEM((tm, D_MODEL), BF16),
            pltpu.VMEM((tm, LANE), F32),
            pltpu.VMEM((tm, D_MODEL), F32),
        ],
        compiler_params=_cparams(("parallel", "arbitrary")),
    )(h, nw, wr, br, wg, wu, wd)


def _ple_kernel(h_ref, p_ref, nw_ref, wg_ref, wp_ref, nf_ref, o_ref, *, final):
    x = h_ref[...]
    xn = (x * _rms_scale(x) * nw_ref[...]).astype(BF16)
    out = x + jax.nn.sigmoid(_dot(xn, wg_ref[...])) * _dot(p_ref[...].astype(BF16), wp_ref[...])
    o_ref[...] = out * _rms_scale(out) * nf_ref[...] if final else out


def _ple(h, p, nw, wg, wp, nf, layer):
    n = h.shape[0]
    tm = _row_tile(n, 1024)
    row = lambda w: pl.BlockSpec((tm, w), lambda i: (i, 0))
    full = lambda shape: pl.BlockSpec(shape, lambda i: (0,) * len(shape))
    lfull = lambda shape: pl.BlockSpec((None,) + shape, lambda i: (layer,) + (0,) * len(shape))
    return pl.pallas_call(
        functools.partial(_ple_kernel, final=layer == DEPTH - 1),
        grid=(n // tm,),
        in_specs=[row(D_MODEL), pl.BlockSpec((None, tm, PLE_DIM), lambda i: (layer, i, 0)),
                  full((1, D_MODEL)), lfull((D_MODEL, D_MODEL)),
                  lfull((PLE_DIM, D_MODEL)), full((1, D_MODEL))],
        out_specs=row(D_MODEL),
        out_shape=jax.ShapeDtypeStruct((n, D_MODEL), F32),
        compiler_params=_cparams(("parallel",)),
    )(h, p, nw, wg, wp, nf)


def _prep_weights(w):
    win_r = _win_prep(w['w_in'])
    lane_pad = lambda a: jnp.pad(a, ((0, 0), (0, LANE - a.shape[1])))
    s5 = [_s5_params(w['s5_a_re'][i], w['s5_a_im'][i], w['s5_b_re'][i], w['s5_b_im'][i],
                     w['s5_c_re'][i], w['s5_c_im'][i], w['s5_log_dt'][i]) for i in range(DEPTH)]
    wr = jnp.concatenate([w['w_rg'], w['w_re'],
                          jnp.zeros((DEPTH, D_MODEL, LANE - E_GROUPS - N_EXPERTS), F32)], axis=2)
    br = jnp.concatenate([w['b_rg'], w['b_re'],
                          jnp.zeros((DEPTH, LANE - E_GROUPS - N_EXPERTS), F32)], axis=1)
    return dict(
        win=win_r,
        s5=s5,
        s5_d=w['s5_d'].reshape(DEPTH, 1, S5_WIDTH),
        wglu=w['s5_w_glu'].astype(BF16),
        bglu=w['s5_b_glu'].reshape(DEPTH, 1, 2 * S5_WIDTH),
        dtb=lane_pad(w['m2_dt_bias']).reshape(DEPTH, 1, LANE),
        alog=lane_pad(w['m2_a_log']).reshape(DEPTH, 1, LANE),
        dsk=jnp.repeat(w['m2_d'], M2_HEADDIM, axis=1).reshape(DEPTH, 1, M2_INNER),
        wa=w['w_br_hg'].astype(BF16), wb=w['w_br_s5'].astype(BF16), wc=w['w_br_m2'].astype(BF16),
        wo=w['w_out'].astype(BF16),
        wr=wr, br=br.reshape(DEPTH, 1, LANE),
        wpg=w['w_ple_gate'].astype(BF16), wpp=w['w_ple_proj'].astype(BF16),
    )


def _trunk(x, p, states, w, pw, t_valid, cfg):
    bsz, t, _ = x.shape
    n = bsz * t
    st_hg, st_re, st_im, st_ssm, st_conv = states
    per_layer = st_hg.shape[0] == DEPTH
    st_re = st_re.reshape(st_re.shape[0], bsz, S5_HW)
    st_im = st_im.reshape(st_im.shape[0], bsz, S5_HW)
    st_ssm = st_ssm.reshape(st_ssm.shape[0], bsz, M2_PAIRS, LANE, M2_STATE)
    h = x.reshape(n, D_MODEL)
    new = []
    s_hg = None
    ssd_prev = None
    for i in range(DEPTH):
        sl = i if per_layer else 0
        z = _norm_matmul(h, w['norm_mix'][i].reshape(1, D_MODEL), pw['win'], i)
        z3 = z.reshape(bsz, t, Z_COLS)
        oa, s_hg = _hgrn(z3, w['hg_lb_logits'], w['hg_gnorm'][i].reshape(1, HG_DIM), st_hg, s_hg, i, sl,
                         cfg['hg_chunk'], cfg['hg_bt'], t_valid)
        bmat, cmat, a_vec = pw['s5'][i]
        ob, s_re, s_im = _s5(z3, bmat, cmat, a_vec, pw['s5_d'][i], pw['wglu'][i], pw['bglu'][i],
                             st_re, st_im, sl, cfg['s5_tc'], cfg['s5_bt'], t_valid)
        oc, s_ssm, conv_new = _ssd(z3, w['m2_conv_w'][i], w['m2_conv_b'][i].reshape(1, M2_CONV_DIM),
                                   pw['dtb'][i], pw['alog'][i], pw['dsk'][i],
                                   w['m2_norm'][i].reshape(1, M2_INNER), st_conv, st_ssm, ssd_prev, i, sl,
                                   cfg['m2_chunk'], cfg['m2_bt'], t_valid)
        ssd_prev = (s_ssm, conv_new)
        h = _merge(oa.reshape(n, HG_W), ob.reshape(n, S5_WIDTH), oc.reshape(n, M2_INNER), z, h,
                   pw['wa'], pw['wb'], pw['wc'], pw['wo'], i)
        h = _moe(h, w['norm_ffn'][i].reshape(1, D_MODEL), pw['wr'][i], pw['br'][i],
                 w['w_e_gate'], w['w_e_up'], w['w_e_down'], i)
        h = _ple(h, p.reshape(DEPTH, n, PLE_DIM), w['norm_ple'][i].reshape(1, D_MODEL),
                 pw['wpg'], pw['wpp'], w['norm_final'].reshape(1, D_MODEL), i)
        new.append((s_re.reshape(bsz, S5_GROUPS, S5_STATE), s_im.reshape(bsz, S5_GROUPS, S5_STATE)))
    s5_re, s5_im = (jnp.stack([nl[j] for nl in new]) for j in range(2))
    stacked = (s_hg, s5_re, s5_im, s_ssm.reshape(DEPTH, bsz, M2_HEADS, M2_HEADDIM, M2_STATE), conv_new)
    return h.reshape(bsz, t, D_MODEL), stacked


def _zero_states(bsz):
    return (jnp.zeros((1, bsz, HG_HEADS, HG_DIM, HG_DIM), F32),
            jnp.zeros((1, bsz, S5_GROUPS, S5_STATE), F32),
            jnp.zeros((1, bsz, S5_GROUPS, S5_STATE), F32),
            jnp.zeros((1, bsz, M2_HEADS, M2_HEADDIM, M2_STATE), F32),
            jnp.zeros((1, bsz, M2_CONV - 1, M2_CONV_DIM), F32))


def _pad_time(a, axis, t_pad):
    t = a.shape[axis]
    if t == t_pad:
        return a
    widths = [(0, 0)] * a.ndim
    widths[axis] = (0, t_pad - t)
    return jnp.pad(a, widths)


def kernel(x_prompt, x_sample, state_hgrn, state_s5_re, state_s5_im, state_ssm, state_conv,
           p_prompt, p_sample,
           norm_mix, w_in, hg_lb_logits, hg_gnorm, w_br_hg,
           s5_a_re, s5_a_im, s5_b_re, s5_b_im, s5_c_re, s5_c_im, s5_d, s5_log_dt, s5_w_glu, s5_b_glu, w_br_s5,
           m2_conv_w, m2_conv_b, m2_dt_bias, m2_a_log, m2_d, m2_norm, w_br_m2,
           w_out,
           norm_ffn, w_rg, b_rg, w_re, b_re, w_e_gate, w_e_up, w_e_down,
           norm_ple, w_ple_gate, w_ple_proj,
           norm_final):
    w = dict(norm_mix=norm_mix, w_in=w_in, hg_lb_logits=hg_lb_logits, hg_gnorm=hg_gnorm, w_br_hg=w_br_hg,
             s5_a_re=s5_a_re, s5_a_im=s5_a_im, s5_b_re=s5_b_re, s5_b_im=s5_b_im, s5_c_re=s5_c_re,
             s5_c_im=s5_c_im, s5_d=s5_d, s5_log_dt=s5_log_dt, s5_w_glu=s5_w_glu, s5_b_glu=s5_b_glu,
             w_br_s5=w_br_s5, m2_conv_w=m2_conv_w, m2_conv_b=m2_conv_b, m2_dt_bias=m2_dt_bias,
             m2_a_log=m2_a_log, m2_d=m2_d, m2_norm=m2_norm, w_br_m2=w_br_m2, w_out=w_out,
             norm_ffn=norm_ffn, w_rg=w_rg, b_rg=b_rg, w_re=w_re, b_re=b_re, w_e_gate=w_e_gate,
             w_e_up=w_e_up, w_e_down=w_e_down, norm_ple=norm_ple, w_ple_gate=w_ple_gate,
             w_ple_proj=w_ple_proj, norm_final=norm_final)
    pw = _prep_weights(w)

    bp, tp, _ = x_prompt.shape
    cfg_p = dict(hg_chunk=min(128, tp), hg_bt=min(2, bp), s5_tc=min(64, tp), s5_bt=SUBLANE,
                 m2_chunk=min(128, tp), m2_bt=1)
    y_p, st_p = _trunk(x_prompt, p_prompt, _zero_states(bp), w, pw, tp, cfg_p)

    bs, ts, _ = x_sample.shape
    ts_pad = -(-ts // SUBLANE) * SUBLANE
    cfg_s = dict(hg_chunk=ts_pad, hg_bt=min(8, bs), s5_tc=ts_pad, s5_bt=min(32, bs),
                 m2_chunk=ts_pad, m2_bt=min(4, bs))
    y_s, st_s = _trunk(_pad_time(x_sample, 1, ts_pad), _pad_time(p_sample, 2, ts_pad),
                       (state_hgrn, state_s5_re, state_s5_im, state_ssm, state_conv), w, pw, ts, cfg_s)
    return (y_p, y_s[:, :ts]) + st_p + st_s
```

```python
import functools

import jax
import jax.numpy as jnp
from jax import lax
from jax.experimental import pallas as pl
from jax.experimental.pallas import tpu as pltpu

F32 = jnp.float32
BF16 = jnp.bfloat16

D_MODEL = 1024
DEPTH = 2
PLE_DIM = 256
NORM_EPS = 1e-6

HG_HEADS = 4
HG_DIM = 128
HG_W = HG_HEADS * HG_DIM

S5_WIDTH = 512
S5_GROUP = 16
S5_GROUPS = S5_WIDTH // S5_GROUP
S5_STATE = 64
S5_HW = S5_GROUPS * S5_STATE
S5_SLABS = 2
S5_SLAB_U = S5_WIDTH // S5_SLABS
S5_SLAB_H = S5_HW // S5_SLABS

M2_INNER = 1024
M2_HEADDIM = 64
M2_HEADS = M2_INNER // M2_HEADDIM
M2_NGROUPS = 4
M2_STATE = 128
M2_CONV = 4
M2_BC = M2_NGROUPS * M2_STATE
M2_CONV_DIM = M2_INNER + 2 * M2_BC
M2_PAIRS = M2_HEADS // 2
M2_NORM_W = M2_INNER // M2_NGROUPS

N_BRANCH = 3
E_GROUPS = 4
E_PER_GROUP = 4
N_EXPERTS = E_GROUPS * E_PER_GROUP
E_FF = 256

LANE = 128
SUBLANE = 8

COL_XBC = 0
COL_Z = COL_XBC + M2_CONV_DIM
COL_GATE = COL_Z + M2_INNER
COL_Q = COL_GATE + N_BRANCH * D_MODEL
COL_F = COL_Q + HG_W
COL_I = COL_F + HG_W
COL_G = COL_I + HG_W
COL_U = COL_G + HG_W
COL_DT = COL_U + S5_WIDTH
Z_TN = 1280
Z_COLS = 7 * Z_TN
assert COL_DT + LANE <= Z_COLS and COL_DT % LANE == 0

_O_Q, _O_F, _O_I, _O_G, _O_U = 0, 512, 1024, 1536, 2048
_O_Z = 2560
_O_XBC = _O_Z + M2_INNER
_O_DT = _O_XBC + M2_CONV_DIM
_O_GATE = _O_DT + M2_HEADS
IN_COLS = _O_GATE + N_BRANCH * D_MODEL

VMEM_LIMIT = 56 * 1024 * 1024


def _cparams(sem):
    return pltpu.CompilerParams(dimension_semantics=sem, vmem_limit_bytes=VMEM_LIMIT)


def _rms_scale(x):
    return lax.rsqrt(jnp.mean(x * x, axis=-1, keepdims=True) + NORM_EPS)


def _silu(x):
    return x * jax.nn.sigmoid(x)


def _dot(a, b):
    return jnp.dot(a, b, preferred_element_type=F32)


def _dot_nt(a, b):
    return lax.dot_general(a, b, (((1,), (1,)), ((), ())), preferred_element_type=F32)


def _dot_tn(a, b):
    return lax.dot_general(a, b, (((0,), (0,)), ((), ())), preferred_element_type=F32)


def _cumsum_rows(x):
    n = x.shape[0]
    row = lax.broadcasted_iota(jnp.int32, x.shape, 0)
    s = 1
    while s < n:
        x = x + jnp.where(row >= s, pltpu.roll(x, s, 0), 0.0)
        s *= 2
    return x


def _write_layer(ref, layer, first, value, idx=()):
    if not first:
        ref[idx if idx else ...] = value
        return
    for l in range(DEPTH):
        ref[(l,) + idx] = value if l == layer else jnp.zeros_like(value)


def _row_tile(n, pref):
    t = min(pref, n)
    assert n % t == 0
    return t


def _win_prep_kernel(wt_ref, o_ref):
    w = wt_ref[...]
    lanes = w.shape[1]
    pieces = [
        w[_O_XBC:_O_XBC + M2_CONV_DIM], w[_O_Z:_O_Z + M2_INNER], w[_O_GATE:IN_COLS],
        w[:_O_Z], w[_O_DT:_O_DT + M2_HEADS], jnp.zeros((Z_COLS - IN_COLS, lanes), F32)]
    o_ref[...] = jnp.concatenate(pieces, axis=0).T.astype(BF16)


def _win_prep(w_in):
    tk = 256
    return pl.pallas_call(
        _win_prep_kernel,
        grid=(DEPTH, D_MODEL // tk),
        in_specs=[pl.BlockSpec((None, IN_COLS, tk), lambda l, i: (l, 0, i))],
        out_specs=pl.BlockSpec((None, tk, Z_COLS), lambda l, i: (l, i, 0)),
        out_shape=jax.ShapeDtypeStruct((DEPTH, D_MODEL, Z_COLS), BF16),
        compiler_params=_cparams(("parallel", "parallel")),
    )(jnp.swapaxes(w_in, 1, 2))


def _norm_matmul_kernel(x_ref, nw_ref, w_ref, o_ref, xn_ref):
    @pl.when(pl.program_id(1) == 0)
    def _():
        x = x_ref[...]
        xn_ref[...] = (x * _rms_scale(x) * nw_ref[...]).astype(BF16)

    o_ref[...] = _dot(xn_ref[...], w_ref[...])


def _norm_matmul(h, nw, w, layer):
    n = h.shape[0]
    cols = w.shape[2]
    tm = _row_tile(n, 1024)
    tn = Z_TN
    return pl.pallas_call(
        _norm_matmul_kernel,
        grid=(n // tm, cols // tn),
        in_specs=[
            pl.BlockSpec((tm, D_MODEL), lambda i, j: (i, 0)),
            pl.BlockSpec((1, D_MODEL), lambda i, j: (0, 0)),
            pl.BlockSpec((None, D_MODEL, tn), lambda i, j: (layer, 0, j)),
        ],
        out_specs=pl.BlockSpec((tm, tn), lambda i, j: (i, j)),
        out_shape=jax.ShapeDtypeStruct((n, cols), F32),
        scratch_shapes=[pltpu.VMEM((tm, D_MODEL), BF16)],
        compiler_params=_cparams(("parallel", "arbitrary")),
    )(h, nw, w)


def _hgrn_chunk(q, k, v, logf, st, lev, dcode, rows_used):
    C = q.shape[0]
    b = _cumsum_rows(logf)
    b_end = b[C - 1:C, :]
    o = _dot((q * jnp.exp(b)).astype(BF16), st.astype(BF16))
    v16 = v.astype(BF16)

    def shifted_score(j):
        if j == 0:
            return jnp.sum(q * k, axis=1, keepdims=True)
        kr = pltpu.roll(k, j, 0)
        br = pltpu.roll(b, j, 0)
        return jnp.sum(q * kr * jnp.exp(jnp.minimum(b - br, 0.0)), axis=1, keepdims=True)

    if C == SUBLANE:
        row = lax.broadcasted_iota(jnp.int32, (C, HG_DIM), 0)
        o = o + shifted_score(0) * v
        for j in range(1, min(SUBLANE, rows_used)):
            o = o + jnp.where(row >= j, shifted_score(j), 0.0) * pltpu.roll(v, j, 0)
        ke = k * jnp.exp(b_end - b)
        decay = jnp.transpose(jnp.broadcast_to(jnp.exp(b_end), (SUBLANE, HG_DIM)))[:, 0:1]
        return o, st * decay + _dot_tn(ke.astype(BF16), v16)

    sc = jnp.where(dcode == 0, shifted_score(0), 0.0)
    for j in range(1, SUBLANE):
        sc = jnp.where(dcode == j, shifted_score(j), sc)

    m = C // 2
    while m >= SUBLANE:
        parts = []
        for p0 in range(0, C, 2 * m):
            parts.append(jnp.broadcast_to(b[p0 + m - 1:p0 + m, :], (2 * m, HG_DIM)))
        r = parts[0] if len(parts) == 1 else jnp.concatenate(parts, axis=0)
        qe = q * jnp.exp(jnp.minimum(b - r, 0.0))
        ke = k * jnp.exp(jnp.minimum(r - b, 0.0))
        pm = _dot_nt(qe.astype(BF16), ke.astype(BF16))
        sc = jnp.where((lev >> (m.bit_length() - 1)) == 1, pm, sc)
        m //= 2

    o = o + _dot(sc.astype(BF16), v16)
    ke = k * jnp.exp(b_end - b)
    decay = jnp.transpose(jnp.broadcast_to(jnp.exp(b_end), (SUBLANE, HG_DIM)))[:, 0:1]
    st_new = st * decay + _dot_tn(ke.astype(BF16), v16)
    return o, st_new


def _hgrn_kernel(zq_ref, zf_ref, zi_ref, zg_ref, lbl_ref, gw_ref, s0_ref, *rest,
                 layer, first, chunk, bt, t_valid, t_total):
    o_ref, s_ref, st_ref = rest[-3:]
    c = pl.program_id(1)
    C = chunk

    @pl.when(c == 0)
    def _():
        st_ref[...] = s0_ref[...]

    lg = lbl_ref[...]
    e = jnp.exp(lg - jnp.max(lg, axis=0, keepdims=True))
    prob = e / jnp.sum(e, axis=0, keepdims=True)
    lb_all = jnp.zeros((1, HG_W), F32)
    for j in range(1, layer + 1):
        lb_all = lb_all + prob[j:j + 1]

    rr = lax.broadcasted_iota(jnp.int32, (C, C), 0)
    cc = lax.broadcasted_iota(jnp.int32, (C, C), 1)
    xr = rr ^ cc
    lev = jnp.where(rr > cc, xr, -1)
    dcode = jnp.where(xr < SUBLANE, rr - cc, -1)
    if t_valid < t_total:
        valid = (c * C + lax.broadcasted_iota(jnp.int32, (C, HG_DIM), 0)) < t_valid
    gw = gw_ref[...]

    for bi in range(bt):
        for h in range(HG_HEADS):
            sl = slice(h * HG_DIM, (h + 1) * HG_DIM)
            lb = lb_all[:, sl]
            zf = zf_ref[bi, :, sl]
            q = _silu(zq_ref[bi, :, sl])
            logf = jnp.log(lb + (1.0 - lb) * jax.nn.sigmoid(zf))
            k = (1.0 - lb) * jax.nn.sigmoid(-zf)
            if t_valid < t_total:
                logf = jnp.where(valid, logf, 0.0)
                k = jnp.where(valid, k, 0.0)
            o, st_new = _hgrn_chunk(q, k, zi_ref[bi, :, sl], logf, st_ref[bi, h], lev, dcode,
                                    min(C, t_valid))
            o_ref[bi, :, sl] = o * _rms_scale(o) * gw * _silu(zg_ref[bi, :, sl])
            st_ref[bi, h] = st_new

    @pl.when(c == pl.num_programs(1) - 1)
    def _():
        _write_layer(s_ref, layer, first, st_ref[...])


def _hgrn(z3, lb_logits, gw, s0, prev, layer, state_layer, chunk, bt, t_valid):
    bsz, t, _ = z3.shape
    nc = t // chunk
    first = prev is None
    kern = functools.partial(_hgrn_kernel, layer=layer, first=first, chunk=chunk, bt=bt, t_valid=t_valid,
                             t_total=t)
    st_shape = (bt, HG_HEADS, HG_DIM, HG_DIM)
    st_spec = (pl.BlockSpec((DEPTH,) + st_shape, lambda b, c: (0, b, 0, 0, 0)) if first else
               pl.BlockSpec((None,) + st_shape, lambda b, c: (layer, b, 0, 0, 0)))

    def zspec(col):
        return pl.BlockSpec((bt, chunk, HG_W), lambda b, c: (b, c, col // HG_W))

    in_specs = [
        zspec(COL_Q), zspec(COL_F), zspec(COL_I), zspec(COL_G),
        pl.BlockSpec((DEPTH, HG_W), lambda b, c: (0, 0)),
        pl.BlockSpec((1, HG_DIM), lambda b, c: (0, 0)),
        pl.BlockSpec((None, bt, HG_HEADS, HG_DIM, HG_DIM), lambda b, c: (state_layer, b, 0, 0, 0)),
    ]
    args = [z3, z3, z3, z3, lb_logits, gw, s0]
    aliases = {}
    if prev is not None:
        in_specs.append(pl.BlockSpec(memory_space=pl.ANY))
        args.append(prev)
        aliases = {len(args) - 1: 1}
    return pl.pallas_call(
        kern,
        grid=(bsz // bt, nc),
        in_specs=in_specs,
        out_specs=[
            pl.BlockSpec((bt, chunk, HG_W), lambda b, c: (b, c, 0)),
            st_spec,
        ],
        out_shape=[
            jax.ShapeDtypeStruct((bsz, t, HG_W), F32),
            jax.ShapeDtypeStruct((DEPTH, bsz, HG_HEADS, HG_DIM, HG_DIM), F32),
        ],
        scratch_shapes=[pltpu.VMEM((bt, HG_HEADS, HG_DIM, HG_DIM), F32)],
        input_output_aliases=aliases,
        compiler_params=_cparams(("parallel", "arbitrary")),
    )(*args)


def _s5_kernel(u_ref, bmat_ref, cmat_ref, a_ref, d_ref, wglu_ref, bglu_ref, hre0_ref, him0_ref,
               o_ref, hre_ref, him_ref, hs_ref, hst_ref, *, tc, bt, steps, lane_chunk):
    c = pl.program_id(1)

    def re_off(s):
        return 2 * s * S5_SLAB_H

    def im_off(s):
        return (2 * s + 1) * S5_SLAB_H

    @pl.when(c == 0)
    def _():
        for s in range(S5_SLABS):
            src = slice(s * S5_SLAB_H, (s + 1) * S5_SLAB_H)
            hst_ref[:, re_off(s):re_off(s) + S5_SLAB_H] = hre0_ref[:, src]
            hst_ref[:, im_off(s):im_off(s) + S5_SLAB_H] = him0_ref[:, src]

    u = pltpu.einshape("btd->tbd", u_ref[...]).reshape(tc * bt, S5_WIDTH)
    u16 = u.astype(BF16)
    for s in range(S5_SLABS):
        hs_ref[:, re_off(s):re_off(s + 1)] = _dot(u16[:, s * S5_SLAB_U:(s + 1) * S5_SLAB_U], bmat_ref[s])

    for s in range(S5_SLABS):
        for l0 in range(0, S5_SLAB_H, lane_chunk):
            re_sl = slice(re_off(s) + l0, re_off(s) + l0 + lane_chunk)
            im_sl = slice(im_off(s) + l0, im_off(s) + l0 + lane_chunk)
            are = a_ref[:, re_sl]
            aim = a_ref[:, im_sl]

            def step(t, carry, re_sl=re_sl, im_sl=im_sl, are=are, aim=aim):
                hre, him = carry
                r0 = pl.multiple_of(t * bt, bt)
                nre = are * hre - aim * him + hs_ref[pl.ds(r0, bt), re_sl]
                nim = are * him + aim * hre + hs_ref[pl.ds(r0, bt), im_sl]
                hs_ref[pl.ds(r0, bt), re_sl] = nre
                hs_ref[pl.ds(r0, bt), im_sl] = nim
                return nre, nim

            hre, him = lax.fori_loop(0, steps, step, (hst_ref[:, re_sl], hst_ref[:, im_sl]), unroll=True)
            hst_ref[:, re_sl] = hre
            hst_ref[:, im_sl] = him

    ys = [_dot(hs_ref[:, re_off(s):re_off(s + 1)].astype(BF16), cmat_ref[s]) for s in range(S5_SLABS)]
    y = jnp.concatenate(ys, axis=1) + d_ref[...] * u
    g = jax.nn.gelu(y)
    gl = _dot(g.astype(BF16), wglu_ref[...]) + bglu_ref[...]
    o = gl[:, :S5_WIDTH] * jax.nn.sigmoid(gl[:, S5_WIDTH:])
    o_ref[...] = pltpu.einshape("tbd->btd", o.reshape(tc, bt, S5_WIDTH))

    @pl.when(c == pl.num_programs(1) - 1)
    def _():
        for s in range(S5_SLABS):
            dst = slice(s * S5_SLAB_H, (s + 1) * S5_SLAB_H)
            hre_ref[:, dst] = hst_ref[:, re_off(s):re_off(s) + S5_SLAB_H]
            him_ref[:, dst] = hst_ref[:, im_off(s):im_off(s) + S5_SLAB_H]


def _s5(z3, bmat, cmat, a_vec, d_vec, wglu, bglu, hre0, him0, state_layer, tc, bt, t_valid):
    bsz, t, _ = z3.shape
    nc = t // tc
    if t_valid < t:
        assert nc == 1
    steps = tc if t_valid == t else t_valid
    lane_chunk = min(S5_SLAB_H, max(LANE, (8 * SUBLANE * LANE) // bt))
    kern = functools.partial(_s5_kernel, tc=tc, bt=bt, steps=steps, lane_chunk=lane_chunk)
    full = lambda shape: pl.BlockSpec(shape, lambda b, c: (0,) * len(shape))
    return pl.pallas_call(
        kern,
        grid=(bsz // bt, nc),
        in_specs=[
            pl.BlockSpec((bt, tc, S5_WIDTH), lambda b, c: (b, c, COL_U // S5_WIDTH)),
            full((S5_SLABS, S5_SLAB_U, 2 * S5_SLAB_H)),
            full((S5_SLABS, 2 * S5_SLAB_H, S5_SLAB_U)),
            full((1, 2 * S5_HW)),
            full((1, S5_WIDTH)),
            full((S5_WIDTH, 2 * S5_WIDTH)),
            full((1, 2 * S5_WIDTH)),
            pl.BlockSpec((None, bt, S5_HW), lambda b, c: (state_layer, b, 0)),
            pl.BlockSpec((None, bt, S5_HW), lambda b, c: (state_layer, b, 0)),
        ],
        out_specs=[
            pl.BlockSpec((bt, tc, S5_WIDTH), lambda b, c: (b, c, 0)),
            pl.BlockSpec((bt, S5_HW), lambda b, c: (b, 0)),
            pl.BlockSpec((bt, S5_HW), lambda b, c: (b, 0)),
        ],
        out_shape=[
            jax.ShapeDtypeStruct((bsz, t, S5_WIDTH), F32),
            jax.ShapeDtypeStruct((bsz, S5_HW), F32),
            jax.ShapeDtypeStruct((bsz, S5_HW), F32),
        ],
        scratch_shapes=[
            pltpu.VMEM((tc * bt, 2 * S5_HW), F32),
            pltpu.VMEM((bt, 2 * S5_HW), F32),
        ],
        compiler_params=_cparams(("parallel", "arbitrary")),
    )(z3, bmat, cmat, a_vec, d_vec, wglu, bglu, hre0, him0)


def _s5_params(a_re, a_im, b_re, b_im, c_re, c_im, log_dt):
    g = S5_GROUPS
    dt = jnp.exp(log_dt)[:, None]
    er = jnp.exp(a_re * dt)
    abr = er * jnp.cos(a_im * dt)
    abi = er * jnp.sin(a_im * dt)
    den = a_re * a_re + a_im * a_im
    xr = abr - 1.0
    cr = (xr * a_re + abi * a_im) / den
    ci = (abi * a_re - xr * a_im) / den
    bbr = cr[..., None] * b_re - ci[..., None] * b_im
    bbi = cr[..., None] * b_im + ci[..., None] * b_re
    sg = g // S5_SLABS
    eye = jnp.eye(sg, dtype=F32)
    slab = lambda x: x.reshape((S5_SLABS, sg) + x.shape[1:])
    bd_in = lambda x: jnp.einsum('sgnc,gh->sgchn', slab(x), eye).reshape(S5_SLABS, S5_SLAB_U, S5_SLAB_H)
    bd_out = lambda x: jnp.einsum('sgcn,gh->sgnhc', slab(x), eye).reshape(S5_SLABS, S5_SLAB_H, S5_SLAB_U)
    bmat = jnp.concatenate([bd_in(bbr), bd_in(bbi)], axis=2).astype(BF16)
    cmat = jnp.concatenate([bd_out(c_re), -bd_out(c_im)], axis=1).astype(BF16)
    lanes = lambda x: x.reshape(S5_SLABS, 1, S5_SLAB_H)
    a_vec = jnp.concatenate([lanes(abr), lanes(abi)], axis=2).reshape(1, 2 * S5_HW)
    return bmat, cmat, a_vec


def _ssd_kernel(xbc_ref, zz_ref, zdt_ref, cw_ref, cb_ref, dtb_ref, alog_ref, dsk_ref, nw_ref,
                conv0_ref, s0_ref, *rest, layer, first, chunk, bt, t_valid, t_total):
    y_ref, s_ref, cn_ref, st_ref, tail_ref = rest[-5:]
    c = pl.program_id(1)
    nc = t_total // chunk
    L = chunk
    TAIL = SUBLANE
    last_valid = t_valid - (nc - 1) * L

    @pl.when(c == 0)
    def _():
        st_ref[...] = s0_ref[...]
        tail_ref[...] = jnp.zeros((bt, TAIL, M2_CONV_DIM), F32)
        tail_ref[:, TAIL - (M2_CONV - 1):TAIL, :] = conv0_ref[...]

    causal = lax.broadcasted_iota(jnp.int32, (L, L), 0) >= lax.broadcasted_iota(jnp.int32, (L, L), 1)
    lo_lane = lax.broadcasted_iota(jnp.int32, (L, LANE), 1) < M2_HEADDIM
    lo_row = lax.broadcasted_iota(jnp.int32, (LANE, LANE), 0) < M2_HEADDIM
    neg_a = -jnp.exp(alog_ref[...])
    nw = nw_ref[...]
    if t_valid < t_total:
        valid = (c * L + lax.broadcasted_iota(jnp.int32, (L, LANE), 0)) < t_valid

    def pair_cols(a, h0):
        return jnp.where(lo_lane, a[:, h0:h0 + 1], a[:, h0 + 1:h0 + 2])

    for bi in range(bt):
        xbc = xbc_ref[bi]
        ext = jnp.concatenate([tail_ref[bi], xbc], axis=0)
        conv = cb_ref[...]
        for j in range(M2_CONV):
            o0 = TAIL - (M2_CONV - 1) + j
            conv = conv + cw_ref[j:j + 1, :] * ext[o0:o0 + L, :]
        tail_ref[bi] = xbc[L - TAIL:L, :]

        @pl.when(c == nc - 1)
        def _(ext=ext, bi=bi):
            _write_layer(cn_ref, layer, first, ext[TAIL + last_valid - (M2_CONV - 1):TAIL + last_valid, :], (bi,))

        act = _silu(conv)
        xs = act[:, :M2_INNER]
        bm = act[:, M2_INNER:M2_INNER + M2_BC].astype(BF16)
        cm = act[:, M2_INNER + M2_BC:].astype(BF16)

        dtr = zdt_ref[bi] + dtb_ref[...]
        dt = jnp.maximum(dtr, 0.0) + jnp.log1p(jnp.exp(-jnp.abs(dtr)))
        if t_valid < t_total:
            dt = jnp.where(valid, dt, 0.0)
        acs = _cumsum_rows(dt * neg_a)
        acs_t = acs.T
        acs_end = acs[L - 1:L, :]
        e_acs = jnp.exp(acs)
        e_end = jnp.exp(acs_end - acs)
        e_tot = jnp.exp(acs_end)

        ys = []
        for g in range(M2_NGROUPS):
            bg = bm[:, g * M2_STATE:(g + 1) * M2_STATE]
            cg = cm[:, g * M2_STATE:(g + 1) * M2_STATE]
            cb = _dot_nt(cg, bg)
            for pr in range(2):
                p = g * 2 + pr
                h0 = 2 * p
                xp = xs[:, p * LANE:(p + 1) * LANE]
                xdt = xp * pair_cols(dt, h0)
                ydiag = None
                for hh in range(2):
                    h = h0 + hh
                    lm = jnp.where(causal, jnp.exp(jnp.minimum(acs[:, h:h + 1] - acs_t[h:h + 1, :], 0.0)), 0.0)
                    mk = lo_lane if hh == 0 else jnp.logical_not(lo_lane)
                    part = _dot((cb * lm).astype(BF16), jnp.where(mk, xdt, 0.0).astype(BF16))
                    ydiag = part if ydiag is None else ydiag + part
                sp = st_ref[bi, p]
                yoff = pair_cols(e_acs, h0) * _dot_nt(cg, sp.astype(BF16))
                ys.append(ydiag + yoff + dsk_ref[:, p * LANE:(p + 1) * LANE] * xp)
                xe = (xdt * pair_cols(e_end, h0)).astype(BF16)
                scale = jnp.where(lo_row, e_tot[:, h0:h0 + 1], e_tot[:, h0 + 1:h0 + 2])
                st_ref[bi, p] = scale * sp + _dot_tn(xe, bg)

        y = jnp.concatenate(ys, axis=1) * _silu(zz_ref[bi])
        outs = []
        for g in range(M2_NGROUPS):
            sl = slice(g * M2_NORM_W, (g + 1) * M2_NORM_W)
            yg = y[:, sl]
            outs.append(yg * _rms_scale(yg) * nw[:, sl])
        y_ref[bi] = jnp.concatenate(outs, axis=1)

    @pl.when(c == nc - 1)
    def _():
        _write_layer(s_ref, layer, first, st_ref[...])


def _ssd(z3, cw, cb, dtb, alog, dsk, nw, conv0, s0, prev, layer, state_layer, chunk, bt, t_valid):
    bsz, t, _ = z3.shape
    nc = t // chunk
    assert chunk >= SUBLANE and t_valid - (nc - 1) * chunk >= 1
    first = prev is None
    kern = functools.partial(_ssd_kernel, layer=layer, first=first, chunk=chunk, bt=bt, t_valid=t_valid,
                             t_total=t)

    def stacked(shape):
        if first:
            return pl.BlockSpec((DEPTH,) + shape, lambda b, c: (0, b) + (0,) * (len(shape) - 1))
        return pl.BlockSpec((None,) + shape, lambda b, c: (layer, b) + (0,) * (len(shape) - 1))
    full = lambda shape: pl.BlockSpec(shape, lambda b, c: (0,) * len(shape))
    extra_specs, extra_args, aliases = [], [], {}
    if prev is not None:
        extra_specs = [pl.BlockSpec(memory_space=pl.ANY)] * 2
        extra_args = list(prev)
        aliases = {11: 1, 12: 2}
    return pl.pallas_call(
        kern,
        grid=(bsz // bt, nc),
        input_output_aliases=aliases,
        in_specs=[
            pl.BlockSpec((bt, chunk, M2_CONV_DIM), lambda b, c: (b, c, COL_XBC // M2_CONV_DIM)),
            pl.BlockSpec((bt, chunk, M2_INNER), lambda b, c: (b, c, COL_Z // M2_INNER)),
            pl.BlockSpec((bt, chunk, LANE), lambda b, c: (b, c, COL_DT // LANE)),
            full((M2_CONV, M2_CONV_DIM)),
            full((1, M2_CONV_DIM)),
            full((1, LANE)),
            full((1, LANE)),
            full((1, M2_INNER)),
            full((1, M2_INNER)),
            pl.BlockSpec((None, bt, M2_CONV - 1, M2_CONV_DIM), lambda b, c: (state_layer, b, 0, 0)),
            pl.BlockSpec((None, bt, M2_PAIRS, LANE, M2_STATE), lambda b, c: (state_layer, b, 0, 0, 0)),
        ] + extra_specs,
        out_specs=[
            pl.BlockSpec((bt, chunk, M2_INNER), lambda b, c: (b, c, 0)),
            stacked((bt, M2_PAIRS, LANE, M2_STATE)),
            stacked((bt, M2_CONV - 1, M2_CONV_DIM)),
        ],
        out_shape=[
            jax.ShapeDtypeStruct((bsz, t, M2_INNER), F32),
            jax.ShapeDtypeStruct((DEPTH, bsz, M2_PAIRS, LANE, M2_STATE), F32),
            jax.ShapeDtypeStruct((DEPTH, bsz, M2_CONV - 1, M2_CONV_DIM), F32),
        ],
        scratch_shapes=[
            pltpu.VMEM((bt, M2_PAIRS, LANE, M2_STATE), F32),
            pltpu.VMEM((bt, SUBLANE, M2_CONV_DIM), F32),
        ],
        compiler_params=_cparams(("parallel", "arbitrary")),
    )(z3, z3, z3, cw, cb, dtb, alog, dsk, nw, conv0, s0, *extra_args)


def _merge_kernel(oa_ref, ob_ref, oc_ref, ga_ref, gb_ref, gc_ref, h_ref, wa_ref, wb_ref, wc_ref, wo_ref, o_ref):
    m = jax.nn.sigmoid(ga_ref[...]) * _dot(oa_ref[...].astype(BF16), wa_ref[...])
    m = m + jax.nn.sigmoid(gb_ref[...]) * _dot(ob_ref[...].astype(BF16), wb_ref[...])
    m = m + jax.nn.sigmoid(gc_ref[...]) * _dot(oc_ref[...].astype(BF16), wc_ref[...])
    o_ref[...] = h_ref[...] + _dot(m.astype(BF16), wo_ref[...])


def _merge(oa, ob, oc, z, h, wa, wb, wc, wo, layer):
    n = h.shape[0]
    tm = _row_tile(n, 512)
    gblk = COL_GATE // D_MODEL
    row = lambda w, j=0: pl.BlockSpec((tm, w), lambda i: (i, j))
    full = lambda shape: pl.BlockSpec((None,) + shape, lambda i: (layer,) + (0,) * len(shape))
    return pl.pallas_call(
        _merge_kernel,
        grid=(n // tm,),
        in_specs=[
            row(HG_W), row(S5_WIDTH), row(M2_INNER),
            row(D_MODEL, gblk), row(D_MODEL, gblk + 1), row(D_MODEL, gblk + 2),
            row(D_MODEL),
            full((HG_W, D_MODEL)), full((S5_WIDTH, D_MODEL)), full((M2_INNER, D_MODEL)),
            full((D_MODEL, D_MODEL)),
        ],
        out_specs=row(D_MODEL),
        out_shape=jax.ShapeDtypeStruct((n, D_MODEL), F32),
        compiler_params=_cparams(("parallel",)),
    )(oa, ob, oc, z, z, z, h, wa, wb, wc, wo)


R_E0 = E_GROUPS
MOE_EPS = 2


def _moe_kernel(h_ref, nw_ref, wr_ref, br_ref, wg_ref, wu_ref, wd_ref, o_ref, xn_ref, comb_ref, acc_ref):
    e = pl.program_id(1)
    tm = h_ref.shape[0]
    lane = lax.broadcasted_iota(jnp.int32, (tm, LANE), 1)
    neg = -jnp.inf

    @pl.when(e == 0)
    def _():
        x = h_ref[...]
        xn = x * _rms_scale(x) * nw_ref[...]
        xh = xn.astype(BF16)
        xn_ref[...] = xh
        xl = (xn - xh.astype(F32)).astype(BF16)
        wr = wr_ref[...]
        wrh = wr.astype(BF16)
        wrl = (wr - wrh.astype(F32)).astype(BF16)
        logits = _dot(xh, wrh) + (_dot(xl, wrh) + _dot(xh, wrl)) + br_ref[...]
        gl = jnp.where(lane < E_GROUPS, logits, neg)
        gmax = jnp.max(gl, axis=1, keepdims=True)
        gidx = jnp.min(jnp.where(gl == gmax, lane, LANE), axis=1, keepdims=True)
        p_group = 1.0 / jnp.sum(jnp.exp(gl - gmax), axis=1, keepdims=True)
        in_grp = (lane >= R_E0) & (lane < R_E0 + N_EXPERTS) & (((lane - R_E0) >> 2) == gidx)
        ev = jnp.where(in_grp, logits, neg)
        v1 = jnp.max(ev, axis=1, keepdims=True)
        i1 = jnp.min(jnp.where(ev == v1, lane, LANE), axis=1, keepdims=True)
        ev2 = jnp.where(lane == i1, neg, ev)
        v2 = jnp.max(ev2, axis=1, keepdims=True)
        i2 = jnp.min(jnp.where(ev2 == v2, lane, LANE), axis=1, keepdims=True)
        e2 = jnp.exp(v2 - v1)
        w1 = 1.0 / (1.0 + e2)
        comb_ref[...] = jnp.where(lane == i1, w1 * p_group, jnp.where(lane == i2, e2 * w1 * p_group, 0.0))
        acc_ref[...] = jnp.zeros_like(acc_ref)

    xn = xn_ref[...]
    comb = comb_ref[...]
    out = acc_ref[...]
    for j in range(MOE_EPS):
        hid = _silu(_dot(xn, wg_ref[j].astype(BF16))) * _dot(xn, wu_ref[j].astype(BF16))
        ce = jnp.sum(jnp.where(lane == R_E0 + e * MOE_EPS + j, comb, 0.0), axis=1, keepdims=True)
        out = out + ce * _dot(hid.astype(BF16), wd_ref[j].astype(BF16))
    acc_ref[...] = out

    @pl.when(e == N_EXPERTS // MOE_EPS - 1)
    def _():
        o_ref[...] = h_ref[...] + out


def _moe(h, nw, wr, br, wg, wu, wd, layer):
    n = h.shape[0]
    tm = _row_tile(n, 1024)
    return pl.pallas_call(
        _moe_kernel,
        grid=(n // tm, N_EXPERTS // MOE_EPS),
        in_specs=[
            pl.BlockSpec((tm, D_MODEL), lambda i, e: (i, 0)),
            pl.BlockSpec((1, D_MODEL), lambda i, e: (0, 0)),
            pl.BlockSpec((D_MODEL, LANE), lambda i, e: (0, 0)),
            pl.BlockSpec((1, LANE), lambda i, e: (0, 0)),
            pl.BlockSpec((None, MOE_EPS, D_MODEL, E_FF), lambda i, e: (layer, e, 0, 0)),
            pl.BlockSpec((None, MOE_EPS, D_MODEL, E_FF), lambda i, e: (layer, e, 0, 0)),
            pl.BlockSpec((None, MOE_EPS, E_FF, D_MODEL), lambda i, e: (layer, e, 0, 0)),
        ],
        out_specs=pl.BlockSpec((tm, D_MODEL), lambda i, e: (i, 0)),
        out_shape=jax.ShapeDtypeStruct((n, D_MODEL), F32),
        scratch_shapes=[
            pltpu.VMEM((tm, D_MODEL), BF16),
            pltpu.VMEM((tm, LANE), F32),
            pltpu.VMEM((tm, D_MODEL), F32),
        ],
        compiler_params=_cparams(("parallel", "arbitrary")),
    )(h, nw, wr, br, wg, wu, wd)


def _ple_kernel(h_ref, p_ref, nw_ref, wg_ref, wp_ref, nf_ref, o_ref, *, final):
    x = h_ref[...]
    xn = (x * _rms_scale(x) * nw_ref[...]).astype(BF16)
    out = x + jax.nn.sigmoid(_dot(xn, wg_ref[...])) * _dot(p_ref[...].astype(BF16), wp_ref[...])
    o_ref[...] = out * _rms_scale(out) * nf_ref[...] if final else out


def _ple(h, p, nw, wg, wp, nf, layer):
    n = h.shape[0]
    tm = _row_tile(n, 1024)
    row = lambda w: pl.BlockSpec((tm, w), lambda i: (i, 0))
    full = lambda shape: pl.BlockSpec(shape, lambda i: (0,) * len(shape))
    lfull = lambda shape: pl.BlockSpec((None,) + shape, lambda i: (layer,) + (0,) * len(shape))
    return pl.pallas_call(
        functools.partial(_ple_kernel, final=layer == DEPTH - 1),
        grid=(n // tm,),
        in_specs=[row(D_MODEL), pl.BlockSpec((None, tm, PLE_DIM), lambda i: (layer, i, 0)),
                  full((1, D_MODEL)), lfull((D_MODEL, D_MODEL)),
                  lfull((PLE_DIM, D_MODEL)), full((1, D_MODEL))],
        out_specs=row(D_MODEL),
        out_shape=jax.ShapeDtypeStruct((n, D_MODEL), F32),
        compiler_params=_cparams(("parallel",)),
    )(h, p, nw, wg, wp, nf)


def _prep_weights(w):
    win_r = _win_prep(w['w_in'])
    lane_pad = lambda a: jnp.pad(a, ((0, 0), (0, LANE - a.shape[1])))
    s5 = [_s5_params(w['s5_a_re'][i], w['s5_a_im'][i], w['s5_b_re'][i], w['s5_b_im'][i],
                     w['s5_c_re'][i], w['s5_c_im'][i], w['s5_log_dt'][i]) for i in range(DEPTH)]
    wr = jnp.concatenate([w['w_rg'], w['w_re'],
                          jnp.zeros((DEPTH, D_MODEL, LANE - E_GROUPS - N_EXPERTS), F32)], axis=2)
    br = jnp.concatenate([w['b_rg'], w['b_re'],
                          jnp.zeros((DEPTH, LANE - E_GROUPS - N_EXPERTS), F32)], axis=1)
    return dict(
        win=win_r,
        s5=s5,
        s5_d=w['s5_d'].reshape(DEPTH, 1, S5_WIDTH),
        wglu=w['s5_w_glu'].astype(BF16),
        bglu=w['s5_b_glu'].reshape(DEPTH, 1, 2 * S5_WIDTH),
        dtb=lane_pad(w['m2_dt_bias']).reshape(DEPTH, 1, LANE),
        alog=lane_pad(w['m2_a_log']).reshape(DEPTH, 1, LANE),
        dsk=jnp.repeat(w['m2_d'], M2_HEADDIM, axis=1).reshape(DEPTH, 1, M2_INNER),
        wa=w['w_br_hg'].astype(BF16), wb=w['w_br_s5'].astype(BF16), wc=w['w_br_m2'].astype(BF16),
        wo=w['w_out'].astype(BF16),
        wr=wr, br=br.reshape(DEPTH, 1, LANE),
        wpg=w['w_ple_gate'].astype(BF16), wpp=w['w_ple_proj'].astype(BF16),
    )


def _trunk(x, p, states, w, pw, t_valid, cfg):
    bsz, t, _ = x.shape
    n = bsz * t
    st_hg, st_re, st_im, st_ssm, st_conv = states
    per_layer = st_hg.shape[0] == DEPTH
    st_re = st_re.reshape(st_re.shape[0], bsz, S5_HW)
    st_im = st_im.reshape(st_im.shape[0], bsz, S5_HW)
    st_ssm = st_ssm.reshape(st_ssm.shape[0], bsz, M2_PAIRS, LANE, M2_STATE)
    h = x.reshape(n, D_MODEL)
    new = []
    s_hg = None
    ssd_prev = None
    for i in range(DEPTH):
        sl = i if per_layer else 0
        z = _norm_matmul(h, w['norm_mix'][i].reshape(1, D_MODEL), pw['win'], i)
        z3 = z.reshape(bsz, t, Z_COLS)
        oa, s_hg = _hgrn(z3, w['hg_lb_logits'], w['hg_gnorm'][i].reshape(1, HG_DIM), st_hg, s_hg, i, sl,
                         cfg['hg_chunk'], cfg['hg_bt'], t_valid)
        bmat, cmat, a_vec = pw['s5'][i]
        ob, s_re, s_im = _s5(z3, bmat, cmat, a_vec, pw['s5_d'][i], pw['wglu'][i], pw['bglu'][i],
                             st_re, st_im, sl, cfg['s5_tc'], cfg['s5_bt'], t_valid)
        oc, s_ssm, conv_new = _ssd(z3, w['m2_conv_w'][i], w['m2_conv_b'][i].reshape(1, M2_CONV_DIM),
                                   pw['dtb'][i], pw['alog'][i], pw['dsk'][i],
                                   w['m2_norm'][i].reshape(1, M2_INNER), st_conv, st_ssm, ssd_prev, i, sl,
                                   cfg['m2_chunk'], cfg['m2_bt'], t_valid)
        ssd_prev = (s_ssm, conv_new)
        h = _merge(oa.reshape(n, HG_W), ob.reshape(n, S5_WIDTH), oc.reshape(n, M2_INNER), z, h,
                   pw['wa'], pw['wb'], pw['wc'], pw['wo'], i)
        h = _moe(h, w['norm_ffn'][i].reshape(1, D_MODEL), pw['wr'][i], pw['br'][i],
                 w['w_e_gate'], w['w_e_up'], w['w_e_down'], i)
        h = _ple(h, p.reshape(DEPTH, n, PLE_DIM), w['norm_ple'][i].reshape(1, D_MODEL),
                 pw['wpg'], pw['wpp'], w['norm_final'].reshape(1, D_MODEL), i)
        new.append((s_re.reshape(bsz, S5_GROUPS, S5_STATE), s_im.reshape(bsz, S5_GROUPS, S5_STATE)))
    s5_re, s5_im = (jnp.stack([nl[j] for nl in new]) for j in range(2))
    stacked = (s_hg, s5_re, s5_im, s_ssm.reshape(DEPTH, bsz, M2_HEADS, M2_HEADDIM, M2_STATE), conv_new)
    return h.reshape(bsz, t, D_MODEL), stacked


def _zero_states(bsz):
    return (jnp.zeros((1, bsz, HG_HEADS, HG_DIM, HG_DIM), F32),
            jnp.zeros((1, bsz, S5_GROUPS, S5_STATE), F32),
            jnp.zeros((1, bsz, S5_GROUPS, S5_STATE), F32),
            jnp.zeros((1, bsz, M2_HEADS, M2_HEADDIM, M2_STATE), F32),
            jnp.zeros((1, bsz, M2_CONV - 1, M2_CONV_DIM), F32))


def _pad_time(a, axis, t_pad):
    t = a.shape[axis]
    if t == t_pad:
        return a
    widths = [(0, 0)] * a.ndim
    widths[axis] = (0, t_pad - t)
    return jnp.pad(a, widths)


def kernel(x_prompt, x_sample, state_hgrn, state_s5_re, state_s5_im, state_ssm, state_conv,
           p_prompt, p_sample,
           norm_mix, w_in, hg_lb_logits, hg_gnorm, w_br_hg,
           s5_a_re, s5_a_im, s5_b_re, s5_b_im, s5_c_re, s5_c_im, s5_d, s5_log_dt, s5_w_glu, s5_b_glu, w_br_s5,
           m2_conv_w, m2_conv_b, m2_dt_bias, m2_a_log, m2_d, m2_norm, w_br_m2,
           w_out,
           norm_ffn, w_rg, b_rg, w_re, b_re, w_e_gate, w_e_up, w_e_down,
           norm_ple, w_ple_gate, w_ple_proj,
           norm_final):
    w = dict(norm_mix=norm_mix, w_in=w_in, hg_lb_logits=hg_lb_logits, hg_gnorm=hg_gnorm, w_br_hg=w_br_hg,
             s5_a_re=s5_a_re, s5_a_im=s5_a_im, s5_b_re=s5_b_re, s5_b_im=s5_b_im, s5_c_re=s5_c_re,
             s5_c_im=s5_c_im, s5_d=s5_d, s5_log_dt=s5_log_dt, s5_w_glu=s5_w_glu, s5_b_glu=s5_b_glu,
             w_br_s5=w_br_s5, m2_conv_w=m2_conv_w, m2_conv_b=m2_conv_b, m2_dt_bias=m2_dt_bias,
             m2_a_log=m2_a_log, m2_d=m2_d, m2_norm=m2_norm, w_br_m2=w_br_m2, w_out=w_out,
             norm_ffn=norm_ffn, w_rg=w_rg, b_rg=b_rg, w_re=w_re, b_re=b_re, w_e_gate=w_e_gate,
             w_e_up=w_e_up, w_e_down=w_e_down, norm_ple=norm_ple, w_ple_gate=w_ple_gate,
             w_ple_proj=w_ple_proj, norm_final=norm_final)
    pw = _prep_weights(w)

    bp, tp, _ = x_prompt.shape
    cfg_p = dict(hg_chunk=min(128, tp), hg_bt=min(2, bp), s5_tc=min(128, tp), s5_bt=SUBLANE,
                 m2_chunk=min(128, tp), m2_bt=1)
    y_p, st_p = _trunk(x_prompt, p_prompt, _zero_states(bp), w, pw, tp, cfg_p)

    bs, ts, _ = x_sample.shape
    ts_pad = -(-ts // SUBLANE) * SUBLANE
    cfg_s = dict(hg_chunk=ts_pad, hg_bt=min(8, bs), s5_tc=ts_pad, s5_bt=min(32, bs),
                 m2_chunk=ts_pad, m2_bt=min(4, bs))
    y_s, st_s = _trunk(_pad_time(x_sample, 1, ts_pad), _pad_time(p_sample, 2, ts_pad),
                       (state_hgrn, state_s5_re, state_s5_im, state_ssm, state_conv), w, pw, ts, cfg_s)
    return (y_p, y_s[:, :ts]) + st_p + st_s
```

```python
import functools

import jax
import jax.numpy as jnp
from jax import lax
from jax.experimental import pallas as pl
from jax.experimental.pallas import tpu as pltpu

F32 = jnp.float32
BF16 = jnp.bfloat16

D_MODEL = 1024
DEPTH = 2
PLE_DIM = 256
NORM_EPS = 1e-6

HG_HEADS = 4
HG_DIM = 128
HG_W = HG_HEADS * HG_DIM

S5_WIDTH = 512
S5_GROUP = 16
S5_GROUPS = S5_WIDTH // S5_GROUP
S5_STATE = 64
S5_HW = S5_GROUPS * S5_STATE
S5_SLABS = 2
S5_SLAB_U = S5_WIDTH // S5_SLABS
S5_SLAB_H = S5_HW // S5_SLABS

M2_INNER = 1024
M2_HEADDIM = 64
M2_HEADS = M2_INNER // M2_HEADDIM
M2_NGROUPS = 4
M2_STATE = 128
M2_CONV = 4
M2_BC = M2_NGROUPS * M2_STATE
M2_CONV_DIM = M2_INNER + 2 * M2_BC
M2_PAIRS = M2_HEADS // 2
M2_NORM_W = M2_INNER // M2_NGROUPS

N_BRANCH = 3
E_GROUPS = 4
E_PER_GROUP = 4
N_EXPERTS = E_GROUPS * E_PER_GROUP
E_FF = 256

LANE = 128
SUBLANE = 8

COL_XBC = 0
COL_Z = COL_XBC + M2_CONV_DIM
COL_GATE = COL_Z + M2_INNER
COL_Q = COL_GATE + N_BRANCH * D_MODEL
COL_F = COL_Q + HG_W
COL_I = COL_F + HG_W
COL_G = COL_I + HG_W
COL_U = COL_G + HG_W
COL_DT = COL_U + S5_WIDTH
Z_TN = 1280
Z_COLS = 7 * Z_TN
assert COL_DT + LANE <= Z_COLS and COL_DT % LANE == 0

_O_Q, _O_F, _O_I, _O_G, _O_U = 0, 512, 1024, 1536, 2048
_O_Z = 2560
_O_XBC = _O_Z + M2_INNER
_O_DT = _O_XBC + M2_CONV_DIM
_O_GATE = _O_DT + M2_HEADS
IN_COLS = _O_GATE + N_BRANCH * D_MODEL

VMEM_LIMIT = 56 * 1024 * 1024


def _cparams(sem):
    return pltpu.CompilerParams(dimension_semantics=sem, vmem_limit_bytes=VMEM_LIMIT)


def _rms_scale(x):
    return lax.rsqrt(jnp.mean(x * x, axis=-1, keepdims=True) + NORM_EPS)


def _silu(x):
    return x * jax.nn.sigmoid(x)


def _dot(a, b):
    return jnp.dot(a, b, preferred_element_type=F32)


def _dot_nt(a, b):
    return lax.dot_general(a, b, (((1,), (1,)), ((), ())), preferred_element_type=F32)


def _dot_tn(a, b):
    return lax.dot_general(a, b, (((0,), (0,)), ((), ())), preferred_element_type=F32)


def _cumsum_rows(x):
    n = x.shape[0]
    row = lax.broadcasted_iota(jnp.int32, x.shape, 0)
    s = 1
    while s < n:
        x = x + jnp.where(row >= s, pltpu.roll(x, s, 0), 0.0)
        s *= 2
    return x


def _write_layer(ref, layer, first, value, idx=()):
    if not first:
        ref[idx if idx else ...] = value
        return
    for l in range(DEPTH):
        ref[(l,) + idx] = value if l == layer else jnp.zeros_like(value)


def _row_tile(n, pref):
    t = min(pref, n)
    assert n % t == 0
    return t


def _win_prep_kernel(wt_ref, o_ref):
    w = wt_ref[...]
    lanes = w.shape[1]
    pieces = [
        w[_O_XBC:_O_XBC + M2_CONV_DIM], w[_O_Z:_O_Z + M2_INNER], w[_O_GATE:IN_COLS],
        w[:_O_Z], w[_O_DT:_O_DT + M2_HEADS], jnp.zeros((Z_COLS - IN_COLS, lanes), F32)]
    o_ref[...] = jnp.concatenate(pieces, axis=0).T.astype(BF16)


def _win_prep(w_in):
    tk = 256
    return pl.pallas_call(
        _win_prep_kernel,
        grid=(DEPTH, D_MODEL // tk),
        in_specs=[pl.BlockSpec((None, IN_COLS, tk), lambda l, i: (l, 0, i))],
        out_specs=pl.BlockSpec((None, tk, Z_COLS), lambda l, i: (l, i, 0)),
        out_shape=jax.ShapeDtypeStruct((DEPTH, D_MODEL, Z_COLS), BF16),
        compiler_params=_cparams(("parallel", "parallel")),
    )(jnp.swapaxes(w_in, 1, 2))


def _norm_matmul_kernel(x_ref, nw_ref, w_ref, o_ref, xn_ref):
    @pl.when(pl.program_id(1) == 0)
    def _():
        x = x_ref[...]
        xn_ref[...] = (x * _rms_scale(x) * nw_ref[...]).astype(BF16)

    o_ref[...] = _dot(xn_ref[...], w_ref[...])


def _norm_matmul(h, nw, w, layer):
    n = h.shape[0]
    cols = w.shape[2]
    tm = _row_tile(n, 1024)
    tn = Z_TN
    return pl.pallas_call(
        _norm_matmul_kernel,
        grid=(n // tm, cols // tn),
        in_specs=[
            pl.BlockSpec((tm, D_MODEL), lambda i, j: (i, 0)),
            pl.BlockSpec((1, D_MODEL), lambda i, j: (0, 0)),
            pl.BlockSpec((None, D_MODEL, tn), lambda i, j: (layer, 0, j)),
        ],
        out_specs=pl.BlockSpec((tm, tn), lambda i, j: (i, j)),
        out_shape=jax.ShapeDtypeStruct((n, cols), F32),
        scratch_shapes=[pltpu.VMEM((tm, D_MODEL), BF16)],
        compiler_params=_cparams(("parallel", "arbitrary")),
    )(h, nw, w)


def _hgrn_chunk(q, k, v, logf, st, lev, dcode, rows_used):
    C = q.shape[0]
    b = _cumsum_rows(logf)
    b_end = b[C - 1:C, :]
    o = _dot((q * jnp.exp(b)).astype(BF16), st.astype(BF16))
    v16 = v.astype(BF16)

    def shifted_score(j):
        if j == 0:
            return jnp.sum(q * k, axis=1, keepdims=True)
        kr = pltpu.roll(k, j, 0)
        br = pltpu.roll(b, j, 0)
        return jnp.sum(q * kr * jnp.exp(jnp.minimum(b - br, 0.0)), axis=1, keepdims=True)

    if C == SUBLANE:
        row = lax.broadcasted_iota(jnp.int32, (C, HG_DIM), 0)
        o = o + shifted_score(0) * v
        for j in range(1, min(SUBLANE, rows_used)):
            o = o + jnp.where(row >= j, shifted_score(j), 0.0) * pltpu.roll(v, j, 0)
        ke = k * jnp.exp(b_end - b)
        decay = jnp.transpose(jnp.broadcast_to(jnp.exp(b_end), (SUBLANE, HG_DIM)))[:, 0:1]
        return o, st * decay + _dot_tn(ke.astype(BF16), v16)

    sc = jnp.where(dcode == 0, shifted_score(0), 0.0)
    for j in range(1, SUBLANE):
        sc = jnp.where(dcode == j, shifted_score(j), sc)

    m = C // 2
    while m >= SUBLANE:
        parts = []
        for p0 in range(0, C, 2 * m):
            parts.append(jnp.broadcast_to(b[p0 + m - 1:p0 + m, :], (2 * m, HG_DIM)))
        r = parts[0] if len(parts) == 1 else jnp.concatenate(parts, axis=0)
        qe = q * jnp.exp(jnp.minimum(b - r, 0.0))
        ke = k * jnp.exp(jnp.minimum(r - b, 0.0))
        pm = _dot_nt(qe.astype(BF16), ke.astype(BF16))
        sc = jnp.where((lev >> (m.bit_length() - 1)) == 1, pm, sc)
        m //= 2

    o = o + _dot(sc.astype(BF16), v16)
    ke = k * jnp.exp(b_end - b)
    decay = jnp.transpose(jnp.broadcast_to(jnp.exp(b_end), (SUBLANE, HG_DIM)))[:, 0:1]
    st_new = st * decay + _dot_tn(ke.astype(BF16), v16)
    return o, st_new


def _hgrn_kernel(zq_ref, zf_ref, zi_ref, zg_ref, lbl_ref, gw_ref, s0_ref, *rest,
                 layer, first, chunk, bt, t_valid, t_total):
    o_ref, s_ref, st_ref = rest[-3:]
    c = pl.program_id(1)
    C = chunk

    @pl.when(c == 0)
    def _():
        st_ref[...] = s0_ref[...]

    lg = lbl_ref[...]
    e = jnp.exp(lg - jnp.max(lg, axis=0, keepdims=True))
    prob = e / jnp.sum(e, axis=0, keepdims=True)
    lb_all = jnp.zeros((1, HG_W), F32)
    for j in range(1, layer + 1):
        lb_all = lb_all + prob[j:j + 1]

    rr = lax.broadcasted_iota(jnp.int32, (C, C), 0)
    cc = lax.broadcasted_iota(jnp.int32, (C, C), 1)
    xr = rr ^ cc
    lev = jnp.where(rr > cc, xr, -1)
    dcode = jnp.where(xr < SUBLANE, rr - cc, -1)
    if t_valid < t_total:
        valid = (c * C + lax.broadcasted_iota(jnp.int32, (C, HG_DIM), 0)) < t_valid
    gw = gw_ref[...]

    for bi in range(bt):
        for h in range(HG_HEADS):
            sl = slice(h * HG_DIM, (h + 1) * HG_DIM)
            lb = lb_all[:, sl]
            zf = zf_ref[bi, :, sl]
            q = _silu(zq_ref[bi, :, sl])
            logf = jnp.log(lb + (1.0 - lb) * jax.nn.sigmoid(zf))
            k = (1.0 - lb) * jax.nn.sigmoid(-zf)
            if t_valid < t_total:
                logf = jnp.where(valid, logf, 0.0)
                k = jnp.where(valid, k, 0.0)
            o, st_new = _hgrn_chunk(q, k, zi_ref[bi, :, sl], logf, st_ref[bi, h], lev, dcode,
                                    min(C, t_valid))
            o_ref[bi, :, sl] = o * _rms_scale(o) * gw * _silu(zg_ref[bi, :, sl])
            st_ref[bi, h] = st_new

    @pl.when(c == pl.num_programs(1) - 1)
    def _():
        _write_layer(s_ref, layer, first, st_ref[...])


def _hgrn(z3, lb_logits, gw, s0, prev, layer, state_layer, chunk, bt, t_valid):
    bsz, t, _ = z3.shape
    nc = t // chunk
    first = prev is None
    kern = functools.partial(_hgrn_kernel, layer=layer, first=first, chunk=chunk, bt=bt, t_valid=t_valid,
                             t_total=t)
    st_shape = (bt, HG_HEADS, HG_DIM, HG_DIM)
    st_spec = (pl.BlockSpec((DEPTH,) + st_shape, lambda b, c: (0, b, 0, 0, 0)) if first else
               pl.BlockSpec((None,) + st_shape, lambda b, c: (layer, b, 0, 0, 0)))

    def zspec(col):
        return pl.BlockSpec((bt, chunk, HG_W), lambda b, c: (b, c, col // HG_W))

    in_specs = [
        zspec(COL_Q), zspec(COL_F), zspec(COL_I), zspec(COL_G),
        pl.BlockSpec((DEPTH, HG_W), lambda b, c: (0, 0)),
        pl.BlockSpec((1, HG_DIM), lambda b, c: (0, 0)),
        pl.BlockSpec((None, bt, HG_HEADS, HG_DIM, HG_DIM), lambda b, c: (state_layer, b, 0, 0, 0)),
    ]
    args = [z3, z3, z3, z3, lb_logits, gw, s0]
    aliases = {}
    if prev is not None:
        in_specs.append(pl.BlockSpec(memory_space=pl.ANY))
        args.append(prev)
        aliases = {len(args) - 1: 1}
    return pl.pallas_call(
        kern,
        grid=(bsz // bt, nc),
        in_specs=in_specs,
        out_specs=[
            pl.BlockSpec((bt, chunk, HG_W), lambda b, c: (b, c, 0)),
            st_spec,
        ],
        out_shape=[
            jax.ShapeDtypeStruct((bsz, t, HG_W), F32),
            jax.ShapeDtypeStruct((DEPTH, bsz, HG_HEADS, HG_DIM, HG_DIM), F32),
        ],
        scratch_shapes=[pltpu.VMEM((bt, HG_HEADS, HG_DIM, HG_DIM), F32)],
        input_output_aliases=aliases,
        compiler_params=_cparams(("parallel", "arbitrary")),
    )(*args)


def _s5_kernel(u_ref, bmat_ref, cmat_ref, a_ref, d_ref, wglu_ref, bglu_ref, hre0_ref, him0_ref,
               o_ref, hre_ref, him_ref, hs_ref, hst_ref, *, tc, bt, steps, lane_chunk):
    c = pl.program_id(1)

    def re_off(s):
        return 2 * s * S5_SLAB_H

    def im_off(s):
        return (2 * s + 1) * S5_SLAB_H

    @pl.when(c == 0)
    def _():
        for s in range(S5_SLABS):
            src = slice(s * S5_SLAB_H, (s + 1) * S5_SLAB_H)
            hst_ref[:, re_off(s):re_off(s) + S5_SLAB_H] = hre0_ref[:, src]
            hst_ref[:, im_off(s):im_off(s) + S5_SLAB_H] = him0_ref[:, src]

    u = pltpu.einshape("btd->tbd", u_ref[...]).reshape(tc * bt, S5_WIDTH)
    u16 = u.astype(BF16)
    for s in range(S5_SLABS):
        hs_ref[:, re_off(s):re_off(s + 1)] = _dot(u16[:, s * S5_SLAB_U:(s + 1) * S5_SLAB_U], bmat_ref[s])

    for s in range(S5_SLABS):
        for l0 in range(0, S5_SLAB_H, lane_chunk):
            re_sl = slice(re_off(s) + l0, re_off(s) + l0 + lane_chunk)
            im_sl = slice(im_off(s) + l0, im_off(s) + l0 + lane_chunk)
            are = a_ref[:, re_sl]
            aim = a_ref[:, im_sl]

            def step(t, carry, re_sl=re_sl, im_sl=im_sl, are=are, aim=aim):
                hre, him = carry
                r0 = pl.multiple_of(t * bt, bt)
                nre = are * hre - aim * him + hs_ref[pl.ds(r0, bt), re_sl]
                nim = are * him + aim * hre + hs_ref[pl.ds(r0, bt), im_sl]
                hs_ref[pl.ds(r0, bt), re_sl] = nre
                hs_ref[pl.ds(r0, bt), im_sl] = nim
                return nre, nim

            hre, him = lax.fori_loop(0, steps, step, (hst_ref[:, re_sl], hst_ref[:, im_sl]), unroll=True)
            hst_ref[:, re_sl] = hre
            hst_ref[:, im_sl] = him

    ys = [_dot(hs_ref[:, re_off(s):re_off(s + 1)].astype(BF16), cmat_ref[s]) for s in range(S5_SLABS)]
    y = jnp.concatenate(ys, axis=1) + d_ref[...] * u
    g = jax.nn.gelu(y)
    gl = _dot(g.astype(BF16), wglu_ref[...]) + bglu_ref[...]
    o = gl[:, :S5_WIDTH] * jax.nn.sigmoid(gl[:, S5_WIDTH:])
    o_ref[...] = pltpu.einshape("tbd->btd", o.reshape(tc, bt, S5_WIDTH))

    @pl.when(c == pl.num_programs(1) - 1)
    def _():
        for s in range(S5_SLABS):
            dst = slice(s * S5_SLAB_H, (s + 1) * S5_SLAB_H)
            hre_ref[:, dst] = hst_ref[:, re_off(s):re_off(s) + S5_SLAB_H]
            him_ref[:, dst] = hst_ref[:, im_off(s):im_off(s) + S5_SLAB_H]


def _s5(z3, bmat, cmat, a_vec, d_vec, wglu, bglu, hre0, him0, state_layer, tc, bt, t_valid):
    bsz, t, _ = z3.shape
    nc = t // tc
    if t_valid < t:
        assert nc == 1
    steps = tc if t_valid == t else t_valid
    lane_chunk = min(S5_SLAB_H, max(LANE, (8 * SUBLANE * LANE) // bt))
    kern = functools.partial(_s5_kernel, tc=tc, bt=bt, steps=steps, lane_chunk=lane_chunk)
    full = lambda shape: pl.BlockSpec(shape, lambda b, c: (0,) * len(shape))
    return pl.pallas_call(
        kern,
        grid=(bsz // bt, nc),
        in_specs=[
            pl.BlockSpec((bt, tc, S5_WIDTH), lambda b, c: (b, c, COL_U // S5_WIDTH)),
            full((S5_SLABS, S5_SLAB_U, 2 * S5_SLAB_H)),
            full((S5_SLABS, 2 * S5_SLAB_H, S5_SLAB_U)),
            full((1, 2 * S5_HW)),
            full((1, S5_WIDTH)),
            full((S5_WIDTH, 2 * S5_WIDTH)),
            full((1, 2 * S5_WIDTH)),
            pl.BlockSpec((None, bt, S5_HW), lambda b, c: (state_layer, b, 0)),
            pl.BlockSpec((None, bt, S5_HW), lambda b, c: (state_layer, b, 0)),
        ],
        out_specs=[
            pl.BlockSpec((bt, tc, S5_WIDTH), lambda b, c: (b, c, 0)),
            pl.BlockSpec((bt, S5_HW), lambda b, c: (b, 0)),
            pl.BlockSpec((bt, S5_HW), lambda b, c: (b, 0)),
        ],
        out_shape=[
            jax.ShapeDtypeStruct((bsz, t, S5_WIDTH), F32),
            jax.ShapeDtypeStruct((bsz, S5_HW), F32),
            jax.ShapeDtypeStruct((bsz, S5_HW), F32),
        ],
        scratch_shapes=[
            pltpu.VMEM((tc * bt, 2 * S5_HW), F32),
            pltpu.VMEM((bt, 2 * S5_HW), F32),
        ],
        compiler_params=_cparams(("parallel", "arbitrary")),
    )(z3, bmat, cmat, a_vec, d_vec, wglu, bglu, hre0, him0)


def _s5_params(a_re, a_im, b_re, b_im, c_re, c_im, log_dt):
    g = S5_GROUPS
    dt = jnp.exp(log_dt)[:, None]
    er = jnp.exp(a_re * dt)
    abr = er * jnp.cos(a_im * dt)
    abi = er * jnp.sin(a_im * dt)
    den = a_re * a_re + a_im * a_im
    xr = abr - 1.0
    cr = (xr * a_re + abi * a_im) / den
    ci = (abi * a_re - xr * a_im) / den
    bbr = cr[..., None] * b_re - ci[..., None] * b_im
    bbi = cr[..., None] * b_im + ci[..., None] * b_re
    sg = g // S5_SLABS
    eye = jnp.eye(sg, dtype=F32)
    slab = lambda x: x.reshape((S5_SLABS, sg) + x.shape[1:])
    bd_in = lambda x: jnp.einsum('sgnc,gh->sgchn', slab(x), eye).reshape(S5_SLABS, S5_SLAB_U, S5_SLAB_H)
    bd_out = lambda x: jnp.einsum('sgcn,gh->sgnhc', slab(x), eye).reshape(S5_SLABS, S5_SLAB_H, S5_SLAB_U)
    bmat = jnp.concatenate([bd_in(bbr), bd_in(bbi)], axis=2).astype(BF16)
    cmat = jnp.concatenate([bd_out(c_re), -bd_out(c_im)], axis=1).astype(BF16)
    lanes = lambda x: x.reshape(S5_SLABS, 1, S5_SLAB_H)
    a_vec = jnp.concatenate([lanes(abr), lanes(abi)], axis=2).reshape(1, 2 * S5_HW)
    return bmat, cmat, a_vec


def _ssd_kernel(xbc_ref, zz_ref, zdt_ref, cw_ref, cb_ref, dtb_ref, alog_ref, dsk_ref, nw_ref,
                conv0_ref, s0_ref, *rest, layer, first, chunk, bt, t_valid, t_total):
    y_ref, s_ref, cn_ref, st_ref, tail_ref = rest[-5:]
    c = pl.program_id(1)
    nc = t_total // chunk
    L = chunk
    TAIL = SUBLANE
    last_valid = t_valid - (nc - 1) * L

    @pl.when(c == 0)
    def _():
        st_ref[...] = s0_ref[...]
        tail_ref[...] = jnp.zeros((bt, TAIL, M2_CONV_DIM), F32)
        tail_ref[:, TAIL - (M2_CONV - 1):TAIL, :] = conv0_ref[...]

    causal = lax.broadcasted_iota(jnp.int32, (L, L), 0) >= lax.broadcasted_iota(jnp.int32, (L, L), 1)
    lo_lane = lax.broadcasted_iota(jnp.int32, (L, LANE), 1) < M2_HEADDIM
    lo_row = lax.broadcasted_iota(jnp.int32, (LANE, LANE), 0) < M2_HEADDIM
    neg_a = -jnp.exp(alog_ref[...])
    nw = nw_ref[...]
    if t_valid < t_total:
        valid = (c * L + lax.broadcasted_iota(jnp.int32, (L, LANE), 0)) < t_valid

    def pair_cols(a, h0):
        return jnp.where(lo_lane, a[:, h0:h0 + 1], a[:, h0 + 1:h0 + 2])

    for bi in range(bt):
        xbc = xbc_ref[bi]
        ext = jnp.concatenate([tail_ref[bi], xbc], axis=0)
        conv = cb_ref[...]
        for j in range(M2_CONV):
            o0 = TAIL - (M2_CONV - 1) + j
            conv = conv + cw_ref[j:j + 1, :] * ext[o0:o0 + L, :]
        tail_ref[bi] = xbc[L - TAIL:L, :]

        @pl.when(c == nc - 1)
        def _(ext=ext, bi=bi):
            _write_layer(cn_ref, layer, first, ext[TAIL + last_valid - (M2_CONV - 1):TAIL + last_valid, :], (bi,))

        act = _silu(conv)
        xs = act[:, :M2_INNER]
        bm = act[:, M2_INNER:M2_INNER + M2_BC].astype(BF16)
        cm = act[:, M2_INNER + M2_BC:].astype(BF16)

        dtr = zdt_ref[bi] + dtb_ref[...]
        dt = jnp.maximum(dtr, 0.0) + jnp.log1p(jnp.exp(-jnp.abs(dtr)))
        if t_valid < t_total:
            dt = jnp.where(valid, dt, 0.0)
        acs = _cumsum_rows(dt * neg_a)
        acs_t = acs.T
        acs_end = acs[L - 1:L, :]
        e_acs = jnp.exp(acs)
        e_end = jnp.exp(acs_end - acs)
        e_tot = jnp.exp(acs_end)

        ys = []
        for g in range(M2_NGROUPS):
            bg = bm[:, g * M2_STATE:(g + 1) * M2_STATE]
            cg = cm[:, g * M2_STATE:(g + 1) * M2_STATE]
            cb = _dot_nt(cg, bg)
            for pr in range(2):
                p = g * 2 + pr
                h0 = 2 * p
                xp = xs[:, p * LANE:(p + 1) * LANE]
                xdt = xp * pair_cols(dt, h0)
                ydiag = None
                for hh in range(2):
                    h = h0 + hh
                    lm = jnp.where(causal, jnp.exp(jnp.minimum(acs[:, h:h + 1] - acs_t[h:h + 1, :], 0.0)), 0.0)
                    mk = lo_lane if hh == 0 else jnp.logical_not(lo_lane)
                    part = _dot((cb * lm).astype(BF16), jnp.where(mk, xdt, 0.0).astype(BF16))
                    ydiag = part if ydiag is None else ydiag + part
                sp = st_ref[bi, p]
                yoff = pair_cols(e_acs, h0) * _dot_nt(cg, sp.astype(BF16))
                ys.append(ydiag + yoff + dsk_ref[:, p * LANE:(p + 1) * LANE] * xp)
                xe = (xdt * pair_cols(e_end, h0)).astype(BF16)
                scale = jnp.where(lo_row, e_tot[:, h0:h0 + 1], e_tot[:, h0 + 1:h0 + 2])
                st_ref[bi, p] = scale * sp + _dot_tn(xe, bg)

        y = jnp.concatenate(ys, axis=1) * _silu(zz_ref[bi])
        outs = []
        for g in range(M2_NGROUPS):
            sl = slice(g * M2_NORM_W, (g + 1) * M2_NORM_W)
            yg = y[:, sl]
            outs.append(yg * _rms_scale(yg) * nw[:, sl])
        y_ref[bi] = jnp.concatenate(outs, axis=1)

    @pl.when(c == nc - 1)
    def _():
        _write_layer(s_ref, layer, first, st_ref[...])


def _ssd(z3, cw, cb, dtb, alog, dsk, nw, conv0, s0, prev, layer, state_layer, chunk, bt, t_valid):
    bsz, t, _ = z3.shape
    nc = t // chunk
    assert chunk >= SUBLANE and t_valid - (nc - 1) * chunk >= 1
    first = prev is None
    kern = functools.partial(_ssd_kernel, layer=layer, first=first, chunk=chunk, bt=bt, t_valid=t_valid,
                             t_total=t)

    def stacked(shape):
        if first:
            return pl.BlockSpec((DEPTH,) + shape, lambda b, c: (0, b) + (0,) * (len(shape) - 1))
        return pl.BlockSpec((None,) + shape, lambda b, c: (layer, b) + (0,) * (len(shape) - 1))
    full = lambda shape: pl.BlockSpec(shape, lambda b, c: (0,) * len(shape))
    extra_specs, extra_args, aliases = [], [], {}
    if prev is not None:
        extra_specs = [pl.BlockSpec(memory_space=pl.ANY)] * 2
        extra_args = list(prev)
        aliases = {11: 1, 12: 2}
    return pl.pallas_call(
        kern,
        grid=(bsz // bt, nc),
        input_output_aliases=aliases,
        in_specs=[
            pl.BlockSpec((bt, chunk, M2_CONV_DIM), lambda b, c: (b, c, COL_XBC // M2_CONV_DIM)),
            pl.BlockSpec((bt, chunk, M2_INNER), lambda b, c: (b, c, COL_Z // M2_INNER)),
            pl.BlockSpec((bt, chunk, LANE), lambda b, c: (b, c, COL_DT // LANE)),
            full((M2_CONV, M2_CONV_DIM)),
            full((1, M2_CONV_DIM)),
            full((1, LANE)),
            full((1, LANE)),
            full((1, M2_INNER)),
            full((1, M2_INNER)),
            pl.BlockSpec((None, bt, M2_CONV - 1, M2_CONV_DIM), lambda b, c: (state_layer, b, 0, 0)),
            pl.BlockSpec((None, bt, M2_PAIRS, LANE, M2_STATE), lambda b, c: (state_layer, b, 0, 0, 0)),
        ] + extra_specs,
        out_specs=[
            pl.BlockSpec((bt, chunk, M2_INNER), lambda b, c: (b, c, 0)),
            stacked((bt, M2_PAIRS, LANE, M2_STATE)),
            stacked((bt, M2_CONV - 1, M2_CONV_DIM)),
        ],
        out_shape=[
            jax.ShapeDtypeStruct((bsz, t, M2_INNER), F32),
            jax.ShapeDtypeStruct((DEPTH, bsz, M2_PAIRS, LANE, M2_STATE), F32),
            jax.ShapeDtypeStruct((DEPTH, bsz, M2_CONV - 1, M2_CONV_DIM), F32),
        ],
        scratch_shapes=[
            pltpu.VMEM((bt, M2_PAIRS, LANE, M2_STATE), F32),
            pltpu.VMEM((bt, SUBLANE, M2_CONV_DIM), F32),
        ],
        compiler_params=_cparams(("parallel", "arbitrary")),
    )(z3, z3, z3, cw, cb, dtb, alog, dsk, nw, conv0, s0, *extra_args)


def _merge_kernel(oa_ref, ob_ref, oc_ref, ga_ref, gb_ref, gc_ref, h_ref, wa_ref, wb_ref, wc_ref, wo_ref, o_ref):
    m = jax.nn.sigmoid(ga_ref[...]) * _dot(oa_ref[...].astype(BF16), wa_ref[...])
    m = m + jax.nn.sigmoid(gb_ref[...]) * _dot(ob_ref[...].astype(BF16), wb_ref[...])
    m = m + jax.nn.sigmoid(gc_ref[...]) * _dot(oc_ref[...].astype(BF16), wc_ref[...])
    o_ref[...] = h_ref[...] + _dot(m.astype(BF16), wo_ref[...])


def _merge(oa, ob, oc, z, h, wa, wb, wc, wo, layer):
    n = h.shape[0]
    tm = _row_tile(n, 512)
    gblk = COL_GATE // D_MODEL
    row = lambda w, j=0: pl.BlockSpec((tm, w), lambda i: (i, j))
    full = lambda shape: pl.BlockSpec((None,) + shape, lambda i: (layer,) + (0,) * len(shape))
    return pl.pallas_call(
        _merge_kernel,
        grid=(n // tm,),
        in_specs=[
            row(HG_W), row(S5_WIDTH), row(M2_INNER),
            row(D_MODEL, gblk), row(D_MODEL, gblk + 1), row(D_MODEL, gblk + 2),
            row(D_MODEL),
            full((HG_W, D_MODEL)), full((S5_WIDTH, D_MODEL)), full((M2_INNER, D_MODEL)),
            full((D_MODEL, D_MODEL)),
        ],
        out_specs=row(D_MODEL),
        out_shape=jax.ShapeDtypeStruct((n, D_MODEL), F32),
        compiler_params=_cparams(("parallel",)),
    )(oa, ob, oc, z, z, z, h, wa, wb, wc, wo)


R_E0 = E_GROUPS
MOE_EPS = 2


def _moe_kernel(h_ref, nw_ref, wr_ref, br_ref, wg_ref, wu_ref, wd_ref, o_ref, xn_ref, comb_ref, acc_ref):
    e = pl.program_id(1)
    tm = h_ref.shape[0]
    lane = lax.broadcasted_iota(jnp.int32, (tm, LANE), 1)
    neg = -jnp.inf

    @pl.when(e == 0)
    def _():
        x = h_ref[...]
        xn = x * _rms_scale(x) * nw_ref[...]
        xh = xn.astype(BF16)
        xn_ref[...] = xh
        xl = (xn - xh.astype(F32)).astype(BF16)
        wr = wr_ref[...]
        wrh = wr.astype(BF16)
        wrl = (wr - wrh.astype(F32)).astype(BF16)
        logits = _dot(xh, wrh) + (_dot(xl, wrh) + _dot(xh, wrl)) + br_ref[...]
        gl = jnp.where(lane < E_GROUPS, logits, neg)
        gmax = jnp.max(gl, axis=1, keepdims=True)
        gidx = jnp.min(jnp.where(gl == gmax, lane, LANE), axis=1, keepdims=True)
        p_group = 1.0 / jnp.sum(jnp.exp(gl - gmax), axis=1, keepdims=True)
        in_grp = (lane >= R_E0) & (lane < R_E0 + N_EXPERTS) & (((lane - R_E0) >> 2) == gidx)
        ev = jnp.where(in_grp, logits, neg)
        v1 = jnp.max(ev, axis=1, keepdims=True)
        i1 = jnp.min(jnp.where(ev == v1, lane, LANE), axis=1, keepdims=True)
        ev2 = jnp.where(lane == i1, neg, ev)
        v2 = jnp.max(ev2, axis=1, keepdims=True)
        i2 = jnp.min(jnp.where(ev2 == v2, lane, LANE), axis=1, keepdims=True)
        e2 = jnp.exp(v2 - v1)
        w1 = 1.0 / (1.0 + e2)
        comb_ref[...] = jnp.where(lane == i1, w1 * p_group, jnp.where(lane == i2, e2 * w1 * p_group, 0.0))
        acc_ref[...] = jnp.zeros_like(acc_ref)

    xn = xn_ref[...]
    comb = comb_ref[...]
    half = D_MODEL // 2
    for j in range(MOE_EPS):
        hid = _silu(_dot(xn, wg_ref[j].astype(BF16))) * _dot(xn, wu_ref[j].astype(BF16))
        ce = jnp.sum(jnp.where(lane == R_E0 + e * MOE_EPS + j, comb, 0.0), axis=1, keepdims=True)
        hid16 = hid.astype(BF16)
        for c0 in (0, half):
            acc_ref[:, c0:c0 + half] += ce * _dot(hid16, wd_ref[j, :, c0:c0 + half].astype(BF16))

    @pl.when(e == N_EXPERTS // MOE_EPS - 1)
    def _():
        o_ref[...] = h_ref[...] + acc_ref[...]


def _moe(h, nw, wr, br, wg, wu, wd, layer):
    n = h.shape[0]
    tm = _row_tile(n, 1024)
    return pl.pallas_call(
        _moe_kernel,
        grid=(n // tm, N_EXPERTS // MOE_EPS),
        in_specs=[
            pl.BlockSpec((tm, D_MODEL), lambda i, e: (i, 0)),
            pl.BlockSpec((1, D_MODEL), lambda i, e: (0, 0)),
            pl.BlockSpec((D_MODEL, LANE), lambda i, e: (0, 0)),
            pl.BlockSpec((1, LANE), lambda i, e: (0, 0)),
            pl.BlockSpec((None, MOE_EPS, D_MODEL, E_FF), lambda i, e: (layer, e, 0, 0)),
            pl.BlockSpec((None, MOE_EPS, D_MODEL, E_FF), lambda i, e: (layer, e, 0, 0)),
            pl.BlockSpec((None, MOE_EPS, E_FF, D_MODEL), lambda i, e: (layer, e, 0, 0)),
        ],
        out_specs=pl.BlockSpec((tm, D_MODEL), lambda i, e: (i, 0)),
        out_shape=jax.ShapeDtypeStruct((n, D_MODEL), F32),
        scratch_shapes=[
            pltpu.VMEM((tm, D_MODEL), BF16),
            pltpu.VMEM((tm, LANE), F32),
            pltpu.VMEM((tm, D_MODEL), F32),
        ],
        compiler_params=_cparams(("parallel", "arbitrary")),
    )(h, nw, wr, br, wg, wu, wd)


def _ple_kernel(h_ref, p_ref, nw_ref, wg_ref, wp_ref, nf_ref, o_ref, *, final):
    x = h_ref[...]
    xn = (x * _rms_scale(x) * nw_ref[...]).astype(BF16)
    out = x + jax.nn.sigmoid(_dot(xn, wg_ref[...])) * _dot(p_ref[...].astype(BF16), wp_ref[...])
    o_ref[...] = out * _rms_scale(out) * nf_ref[...] if final else out


def _ple(h, p, nw, wg, wp, nf, layer):
    n = h.shape[0]
    tm = _row_tile(n, 1024)
    row = lambda w: pl.BlockSpec((tm, w), lambda i: (i, 0))
    full = lambda shape: pl.BlockSpec(shape, lambda i: (0,) * len(shape))
    lfull = lambda shape: pl.BlockSpec((None,) + shape, lambda i: (layer,) + (0,) * len(shape))
    return pl.pallas_call(
        functools.partial(_ple_kernel, final=layer == DEPTH - 1),
        grid=(n // tm,),
        in_specs=[row(D_MODEL), pl.BlockSpec((None, tm, PLE_DIM), lambda i: (layer, i, 0)),
                  full((1, D_MODEL)), lfull((D_MODEL, D_MODEL)),
                  lfull((PLE_DIM, D_MODEL)), full((1, D_MODEL))],
        out_specs=row(D_MODEL),
        out_shape=jax.ShapeDtypeStruct((n, D_MODEL), F32),
        compiler_params=_cparams(("parallel",)),
    )(h, p, nw, wg, wp, nf)


def _prep_weights(w):
    win_r = _win_prep(w['w_in'])
    lane_pad = lambda a: jnp.pad(a, ((0, 0), (0, LANE - a.shape[1])))
    s5 = [_s5_params(w['s5_a_re'][i], w['s5_a_im'][i], w['s5_b_re'][i], w['s5_b_im'][i],
                     w['s5_c_re'][i], w['s5_c_im'][i], w['s5_log_dt'][i]) for i in range(DEPTH)]
    wr = jnp.concatenate([w['w_rg'], w['w_re'],
                          jnp.zeros((DEPTH, D_MODEL, LANE - E_GROUPS - N_EXPERTS), F32)], axis=2)
    br = jnp.concatenate([w['b_rg'], w['b_re'],
                          jnp.zeros((DEPTH, LANE - E_GROUPS - N_EXPERTS), F32)], axis=1)
    return dict(
        win=win_r,
        s5=s5,
        s5_d=w['s5_d'].reshape(DEPTH, 1, S5_WIDTH),
        wglu=w['s5_w_glu'].astype(BF16),
        bglu=w['s5_b_glu'].reshape(DEPTH, 1, 2 * S5_WIDTH),
        dtb=lane_pad(w['m2_dt_bias']).reshape(DEPTH, 1, LANE),
        alog=lane_pad(w['m2_a_log']).reshape(DEPTH, 1, LANE),
        dsk=jnp.repeat(w['m2_d'], M2_HEADDIM, axis=1).reshape(DEPTH, 1, M2_INNER),
        wa=w['w_br_hg'].astype(BF16), wb=w['w_br_s5'].astype(BF16), wc=w['w_br_m2'].astype(BF16),
        wo=w['w_out'].astype(BF16),
        wr=wr, br=br.reshape(DEPTH, 1, LANE),
        wpg=w['w_ple_gate'].astype(BF16), wpp=w['w_ple_proj'].astype(BF16),
    )


def _trunk(x, p, states, w, pw, t_valid, cfg):
    bsz, t, _ = x.shape
    n = bsz * t
    st_hg, st_re, st_im, st_ssm, st_conv = states
    per_layer = st_hg.shape[0] == DEPTH
    st_re = st_re.reshape(st_re.shape[0], bsz, S5_HW)
    st_im = st_im.reshape(st_im.shape[0], bsz, S5_HW)
    st_ssm = st_ssm.reshape(st_ssm.shape[0], bsz, M2_PAIRS, LANE, M2_STATE)
    h = x.reshape(n, D_MODEL)
    new = []
    s_hg = None
    ssd_prev = None
    for i in range(DEPTH):
        sl = i if per_layer else 0
        z = _norm_matmul(h, w['norm_mix'][i].reshape(1, D_MODEL), pw['win'], i)
        z3 = z.reshape(bsz, t, Z_COLS)
        oa, s_hg = _hgrn(z3, w['hg_lb_logits'], w['hg_gnorm'][i].reshape(1, HG_DIM), st_hg, s_hg, i, sl,
                         cfg['hg_chunk'], cfg['hg_bt'], t_valid)
        bmat, cmat, a_vec = pw['s5'][i]
        ob, s_re, s_im = _s5(z3, bmat, cmat, a_vec, pw['s5_d'][i], pw['wglu'][i], pw['bglu'][i],
                             st_re, st_im, sl, cfg['s5_tc'], cfg['s5_bt'], t_valid)
        oc, s_ssm, conv_new = _ssd(z3, w['m2_conv_w'][i], w['m2_conv_b'][i].reshape(1, M2_CONV_DIM),
                                   pw['dtb'][i], pw['alog'][i], pw['dsk'][i],
                                   w['m2_norm'][i].reshape(1, M2_INNER), st_conv, st_ssm, ssd_prev, i, sl,
                                   cfg['m2_chunk'], cfg['m2_bt'], t_valid)
        ssd_prev = (s_ssm, conv_new)
        h = _merge(oa.reshape(n, HG_W), ob.reshape(n, S5_WIDTH), oc.reshape(n, M2_INNER), z, h,
                   pw['wa'], pw['wb'], pw['wc'], pw['wo'], i)
        h = _moe(h, w['norm_ffn'][i].reshape(1, D_MODEL), pw['wr'][i], pw['br'][i],
                 w['w_e_gate'], w['w_e_up'], w['w_e_down'], i)
        h = _ple(h, p.reshape(DEPTH, n, PLE_DIM), w['norm_ple'][i].reshape(1, D_MODEL),
                 pw['wpg'], pw['wpp'], w['norm_final'].reshape(1, D_MODEL), i)
        new.append((s_re.reshape(bsz, S5_GROUPS, S5_STATE), s_im.reshape(bsz, S5_GROUPS, S5_STATE)))
    s5_re, s5_im = (jnp.stack([nl[j] for nl in new]) for j in range(2))
    stacked = (s_hg, s5_re, s5_im, s_ssm.reshape(DEPTH, bsz, M2_HEADS, M2_HEADDIM, M2_STATE), conv_new)
    return h.reshape(bsz, t, D_MODEL), stacked


def _zero_states(bsz):
    return (jnp.zeros((1, bsz, HG_HEADS, HG_DIM, HG_DIM), F32),
            jnp.zeros((1, bsz, S5_GROUPS, S5_STATE), F32),
            jnp.zeros((1, bsz, S5_GROUPS, S5_STATE), F32),
            jnp.zeros((1, bsz, M2_HEADS, M2_HEADDIM, M2_STATE), F32),
            jnp.zeros((1, bsz, M2_CONV - 1, M2_CONV_DIM), F32))


def _pad_time(a, axis, t_pad):
    t = a.shape[axis]
    if t == t_pad:
        return a
    widths = [(0, 0)] * a.ndim
    widths[axis] = (0, t_pad - t)
    return jnp.pad(a, widths)


def kernel(x_prompt, x_sample, state_hgrn, state_s5_re, state_s5_im, state_ssm, state_conv,
           p_prompt, p_sample,
           norm_mix, w_in, hg_lb_logits, hg_gnorm, w_br_hg,
           s5_a_re, s5_a_im, s5_b_re, s5_b_im, s5_c_re, s5_c_im, s5_d, s5_log_dt, s5_w_glu, s5_b_glu, w_br_s5,
           m2_conv_w, m2_conv_b, m2_dt_bias, m2_a_log, m2_d, m2_norm, w_br_m2,
           w_out,
           norm_ffn, w_rg, b_rg, w_re, b_re, w_e_gate, w_e_up, w_e_down,
           norm_ple, w_ple_gate, w_ple_proj,
           norm_final):
    w = dict(norm_mix=norm_mix, w_in=w_in, hg_lb_logits=hg_lb_logits, hg_gnorm=hg_gnorm, w_br_hg=w_br_hg,
             s5_a_re=s5_a_re, s5_a_im=s5_a_im, s5_b_re=s5_b_re, s5_b_im=s5_b_im, s5_c_re=s5_c_re,
             s5_c_im=s5_c_im, s5_d=s5_d, s5_log_dt=s5_log_dt, s5_w_glu=s5_w_glu, s5_b_glu=s5_b_glu,
             w_br_s5=w_br_s5, m2_conv_w=m2_conv_w, m2_conv_b=m2_conv_b, m2_dt_bias=m2_dt_bias,
             m2_a_log=m2_a_log, m2_d=m2_d, m2_norm=m2_norm, w_br_m2=w_br_m2, w_out=w_out,
             norm_ffn=norm_ffn, w_rg=w_rg, b_rg=b_rg, w_re=w_re, b_re=b_re, w_e_gate=w_e_gate,
             w_e_up=w_e_up, w_e_down=w_e_down, norm_ple=norm_ple, w_ple_gate=w_ple_gate,
             w_ple_proj=w_ple_proj, norm_final=norm_final)
    pw = _prep_weights(w)

    bp, tp, _ = x_prompt.shape
    cfg_p = dict(hg_chunk=min(128, tp), hg_bt=min(2, bp), s5_tc=min(128, tp), s5_bt=SUBLANE,
                 m2_chunk=min(128, tp), m2_bt=1)
    y_p, st_p = _trunk(x_prompt, p_prompt, _zero_states(bp), w, pw, tp, cfg_p)

    bs, ts, _ = x_sample.shape
    ts_pad = -(-ts // SUBLANE) * SUBLANE
    cfg_s = dict(hg_chunk=ts_pad, hg_bt=min(8, bs), s5_tc=ts_pad, s5_bt=min(32, bs),
                 m2_chunk=ts_pad, m2_bt=min(4, bs))
    y_s, st_s = _trunk(_pad_time(x_sample, 1, ts_pad), _pad_time(p_sample, 2, ts_pad),
                       (state_hgrn, state_s5_re, state_s5_im, state_ssm, state_conv), w, pw, ts, cfg_s)
    return (y_p, y_s[:, :ts]) + st_p + st_s
```
